```python
import math
import jax
import jax.numpy as jnp
from jax import lax
import numpy as np

D_MODEL = 1024
BATCH = 2
SEQ = 8192
DEPTH = 4

HEAD_DIM = 64
A_HEADS = 8
A_WIDTH = A_HEADS * HEAD_DIM
DECAY_LORA = 64
ICLR_LORA = 64
VRES_LORA = 32
GATE_LORA = 128
GN_EPS = 64e-5
B_HEADS = 8
B_WIDTH = B_HEADS * HEAD_DIM
KV_RANK = 128
IDX_HEADS = 8
IDX_DIM = 64
IDX_TOPK_MAX = 256
C_HEADS = 16
C_WIDTH = C_HEADS * HEAD_DIM
N_GROUPS = 4
EXPERTS_PER_GROUP = 8
N_EXPERTS = N_GROUPS * EXPERTS_PER_GROUP
EXPERT_FF = 512
TOP_K_EXPERTS = 2
MOE_BLOCK = 128
Q_BLOCK = 128
RMS_EPS = 1e-6
N_EVEN = (DEPTH + 1) // 2
N_ODD = DEPTH // 2
EVEN_SPLIT_SIZES = (A_WIDTH, A_WIDTH, A_WIDTH, B_WIDTH, KV_RANK, IDX_HEADS * IDX_DIM, IDX_DIM, IDX_HEADS)
IN_EVEN = sum(EVEN_SPLIT_SIZES)

kernel_name = "hybrid_rwkv7_dsa_stickbreak_hmoe"


def rmsnorm(x, g):
    xf = x.astype(jnp.float32)
    y = xf * lax.rsqrt(jnp.mean(xf * xf, axis=-1, keepdims=True) + RMS_EPS)
    return (y * g.astype(jnp.float32)).astype(x.dtype)


def token_shift(z):
    return jnp.concatenate([jnp.zeros_like(z[:, :1]), z[:, :-1]], axis=1)


def rwkv7_scan(r, w, k, v, kk, a):
    b, t, h, n = r.shape

    def step(state, inp):
        r_t, w_t, k_t, v_t, kk_t, a_t = inp
        s_kk = jnp.einsum('bhvk,bhk->bhv', state, kk_t)
        state = (state * w_t[:, :, None, :]
                 - s_kk[..., None] * (kk_t * a_t)[:, :, None, :]
                 + v_t[..., None] * k_t[:, :, None, :])
        return state, jnp.einsum('bhvk,bhk->bhv', state, r_t)

    xs = tuple(jnp.moveaxis(z, 1, 0) for z in (r, w, k, v, kk, a))
    _, ys = lax.scan(step, jnp.zeros((b, h, n, n), jnp.float32), xs)
    return jnp.moveaxis(ys, 0, 1)


def rwkv7_time_mix(h, r, k, v, mu_rkv, mu_lora, w0, w1, w2, a0, a1, a2, g1, g2,
                   k_k, k_a, r_k, ln_g, ln_b, vres):
    f32 = jnp.float32
    b, t, _ = h.shape
    rkv = jnp.stack([r, k, v], axis=2)
    rkv = rkv + (token_shift(rkv) - rkv) * mu_rkv
    r, k, v = rkv[:, :, 0], rkv[:, :, 1], rkv[:, :, 2]
    dh = token_shift(h) - h
    xw = h + dh * mu_lora[0]
    xa = h + dh * mu_lora[1]
    xg = h + dh * mu_lora[2]
    decay = jnp.exp(-math.exp(-0.5) * jax.nn.sigmoid((w0 + jnp.tanh(xw @ w1) @ w2).astype(f32)))
    iclr = jax.nn.sigmoid((a0 + (xa @ a1) @ a2).astype(f32))
    gate = jax.nn.sigmoid(xg @ g1) @ g2
    if vres is not None:
        v_first, mu_v, v0, v1, v2 = vres
        xv = h + dh * mu_v
        v = v + (v_first - v) * jax.nn.sigmoid(v0 + (xv @ v1) @ v2)

    def heads(z):
        return z.astype(f32).reshape(b, t, A_HEADS, HEAD_DIM)

    kk = heads(k * k_k)
    kk = kk * lax.rsqrt(jnp.maximum(jnp.sum(kk * kk, axis=-1, keepdims=True), 1e-12))
    k_mod = k.astype(f32) * (1.0 + (iclr - 1.0) * k_a)
    r_h, k_h, v_h = heads(r), heads(k_mod), heads(v)
    y = rwkv7_scan(r_h, heads(decay), k_h, v_h, kk, heads(iclr))
    yc = y - jnp.mean(y, axis=-1, keepdims=True)
    y = yc * lax.rsqrt(jnp.mean(yc * yc, axis=-1, keepdims=True) + GN_EPS)
    bonus = jnp.sum(r_h * k_h * r_k, axis=-1, keepdims=True) * v_h
    y = y.reshape(b, t, A_WIDTH) * ln_g + ln_b + bonus.reshape(b, t, A_WIDTH)
    return (y * gate).astype(h.dtype), v


def dsa_sparse_attention(q, c_kv, q_idx, k_idx, w_idx, kv_norm, w_kv_up, idx_k_norm):
    b, t = q.shape[0], q.shape[1]
    n_blk = t // Q_BLOCK
    topk = min(IDX_TOPK_MAX, t // 4)
    kv = (rmsnorm(c_kv, kv_norm) @ w_kv_up).reshape(b, t, 2, B_HEADS, HEAD_DIM)
    k_i = rmsnorm(k_idx, idx_k_norm)
    q_blk = q.reshape(b, n_blk, Q_BLOCK, B_HEADS, HEAD_DIM).swapaxes(0, 1)
    qi_blk = q_idx.reshape(b, n_blk, Q_BLOCK, IDX_HEADS, IDX_DIM).swapaxes(0, 1)
    wi_blk = w_idx.reshape(b, n_blk, Q_BLOCK, IDX_HEADS).swapaxes(0, 1)
    key_pos = jnp.arange(t)

    def one_block(args):
        blk, qb, qib, wib = args
        q_pos = blk * Q_BLOCK + jnp.arange(Q_BLOCK)
        dots = jnp.einsum('bqhd,bsd->bqhs', qib, k_i).astype(jnp.float32) * IDX_DIM ** -0.5
        score = jnp.einsum('bqhs,bqh->bqs', jax.nn.relu(dots), wib.astype(jnp.float32)) * IDX_HEADS ** -0.5
        causal = key_pos[None, :] <= q_pos[:, None]
        score = jnp.where(causal[None], score, -jnp.inf)
        _, sel = lax.top_k(score, topk)
        valid = sel <= q_pos[None, :, None]
        kv_sel = jax.vmap(lambda kv_b, sel_b: kv_b[sel_b])(kv, sel)
        logits = jnp.einsum('bqhd,bqkhd->bhqk', qb, kv_sel[:, :, :, 0]).astype(jnp.float32) * HEAD_DIM ** -0.5
        logits = jnp.where(valid[:, None], logits, -jnp.inf)
        p = jax.nn.softmax(logits, axis=-1).astype(kv.dtype)
        return jnp.einsum('bhqk,bqkhd->bqhd', p, kv_sel[:, :, :, 1])

    o = lax.map(one_block, (jnp.arange(n_blk), q_blk, qi_blk, wi_blk))
    return o.swapaxes(0, 1).reshape(b, t, B_WIDTH)


def stick_breaking_attention(q, k, v):
    b, t, _, _ = q.shape
    n_blk = t // Q_BLOCK
    q_blk = q.reshape(b, n_blk, Q_BLOCK, C_HEADS, HEAD_DIM).swapaxes(0, 1)
    key_pos = jnp.arange(t)

    def one_block(args):
        blk, qb = args
        q_pos = blk * Q_BLOCK + jnp.arange(Q_BLOCK)
        z = jnp.einsum('bqhd,bshd->bhqs', qb, k).astype(jnp.float32) * HEAD_DIM ** -0.5
        before = key_pos[None, :] < q_pos[:, None]
        log_keep = jnp.where(before, jax.nn.log_sigmoid(-z), 0.0)
        log_tail = lax.cumsum(log_keep, axis=3, reverse=True) - log_keep
        weight = jnp.where(before, jnp.exp(jax.nn.log_sigmoid(z) + log_tail), 0.0)
        return jnp.einsum('bhqs,bshd->bqhd', weight.astype(v.dtype), v)

    o = lax.map(one_block, (jnp.arange(n_blk), q_blk))
    return o.swapaxes(0, 1).reshape(b, t, C_WIDTH)


def routed_experts(xf, expert_idx, gates, w1, w3, w2):
    n_tok, d = xf.shape
    n_asg = n_tok * TOP_K_EXPERTS
    flat_e = expert_idx.reshape(n_asg)
    order = jnp.argsort(flat_e)
    sorted_e = flat_e[order]
    counts = jnp.zeros((N_EXPERTS,), jnp.int32).at[flat_e].add(1)
    padded = (counts + MOE_BLOCK - 1) // MOE_BLOCK * MOE_BLOCK
    seg_start = jnp.cumsum(counts) - counts
    pad_end = jnp.cumsum(padded)
    pad_start = pad_end - padded
    dest = pad_start[sorted_e] + (jnp.arange(n_asg) - seg_start[sorted_e])
    n_blocks = (n_asg + MOE_BLOCK - 1) // MOE_BLOCK + N_EXPERTS
    n_rows = n_blocks * MOE_BLOCK
    x_rows = jnp.zeros((n_rows, d), xf.dtype).at[dest].set(xf[order // TOP_K_EXPERTS])
    blk_expert = jnp.minimum(jnp.searchsorted(pad_end, jnp.arange(n_blocks) * MOE_BLOCK, side='right'), N_EXPERTS - 1)

    def expert_block(args):
        xb, e = args
        return (jax.nn.silu(xb @ w1[e]) * (xb @ w3[e])) @ w2[e]

    y_rows = lax.map(expert_block, (x_rows.reshape(n_blocks, MOE_BLOCK, d), blk_expert)).reshape(n_rows, d)
    y_asg = jnp.zeros((n_asg, d), y_rows.dtype).at[order].set(y_rows[dest])
    y = jnp.sum(y_asg.reshape(n_tok, TOP_K_EXPERTS, d).astype(jnp.float32) * gates[..., None], axis=1)
    return y.astype(xf.dtype)


def hierarchical_moe(h, router_group, router_expert, w1, w3, w2):
    b, t, d = h.shape
    xf = h.reshape(b * t, d)
    g_logits = (xf @ router_group).astype(jnp.float32)
    g_prob = jax.nn.softmax(g_logits, axis=-1)
    g_sel = jnp.argmax(g_logits, axis=-1)
    g_gate = jnp.take_along_axis(g_prob, g_sel[:, None], axis=1)
    e_logits = jnp.einsum('nd,gde->nge', xf, router_expert).astype(jnp.float32)
    e_logits = jnp.take_along_axis(e_logits, g_sel[:, None, None], axis=1)[:, 0]
    top_p, top_i = lax.top_k(jax.nn.softmax(e_logits, axis=-1), TOP_K_EXPERTS)
    gates = g_gate * top_p / jnp.sum(top_p, axis=-1, keepdims=True)
    expert_idx = g_sel[:, None].astype(jnp.int32) * EXPERTS_PER_GROUP + top_i.astype(jnp.int32)
    return routed_experts(xf, expert_idx, gates, w1, w3, w2).reshape(b, t, d)


def setup_inputs(seed: int = 0) -> dict:
    key = jax.random.key(seed)
    ks = iter(jax.random.split(key, 48))
    D = D_MODEL
    NE, NO = N_EVEN, N_ODD
    res_scale = (2.0 * DEPTH) ** -0.5

    def nrm(shape, scale):
        return jax.random.normal(next(ks), shape, jnp.float32) * scale

    def gain(shape):
        return 1.0 + nrm(shape, 0.02)

    def unif(shape):
        return jax.random.uniform(next(ks), shape, jnp.float32, 0.0, 1.0)

    return {
        'x': nrm((BATCH, SEQ, D), 1.0),
        'ev_norm': gain((NE, D)),
        'ev_w_in': nrm((NE, D, IN_EVEN), D ** -0.5),
        'ev_w_out': nrm((NE, A_WIDTH + B_WIDTH, D), (A_WIDTH + B_WIDTH) ** -0.5 * res_scale),
        'a_mu_rkv': unif((NE, 3, A_WIDTH)),
        'a_mu_lora': unif((NE, 3, D)),
        'a_w0': nrm((NE, A_WIDTH), 0.5),
        'a_w1': nrm((NE, D, DECAY_LORA), D ** -0.5),
        'a_w2': nrm((NE, DECAY_LORA, A_WIDTH), 0.5 * DECAY_LORA ** -0.5),
        'a_a0': nrm((NE, A_WIDTH), 0.1),
        'a_a1': nrm((NE, D, ICLR_LORA), D ** -0.5),
        'a_a2': nrm((NE, ICLR_LORA, A_WIDTH), 0.5 * ICLR_LORA ** -0.5),
        'a_g1': nrm((NE, D, GATE_LORA), D ** -0.5),
        'a_g2': nrm((NE, GATE_LORA, A_WIDTH), GATE_LORA ** -0.5),
        'a_k_k': 0.85 + nrm((NE, A_WIDTH), 0.02),
        'a_k_a': 1.0 + nrm((NE, A_WIDTH), 0.02),
        'a_r_k': nrm((NE, A_HEADS, HEAD_DIM), 0.1),
        'a_ln_g': gain((NE, A_WIDTH)),
        'a_ln_b': nrm((NE, A_WIDTH), 0.02),
        'a_mu_vres': unif((NE - 1, D)),
        'a_v0': nrm((NE - 1, A_WIDTH), 0.5),
        'a_v1': nrm((NE - 1, D, VRES_LORA), D ** -0.5),
        'a_v2': nrm((NE - 1, VRES_LORA, A_WIDTH), 0.5 * VRES_LORA ** -0.5),
        'b_kv_norm': gain((NE, KV_RANK)),
        'b_w_kv_up': nrm((NE, KV_RANK, 2 * B_WIDTH), KV_RANK ** -0.5),
        'b_idx_k_norm': gain((NE, IDX_DIM)),
        'od_norm': gain((NO, D)),
        'od_w_qkv': nrm((NO, D, 3 * C_WIDTH), D ** -0.5),
        'od_w_out': nrm((NO, C_WIDTH, D), C_WIDTH ** -0.5 * res_scale),
        'moe_norm': gain((DEPTH, D)),
        'moe_router_group': nrm((DEPTH, D, N_GROUPS), D ** -0.5),
        'moe_router_expert': nrm((DEPTH, N_GROUPS, D, EXPERTS_PER_GROUP), D ** -0.5),
        'moe_w1': nrm((DEPTH, N_EXPERTS, D, EXPERT_FF), D ** -0.5),
        'moe_w3': nrm((DEPTH, N_EXPERTS, D, EXPERT_FF), D ** -0.5),
        'moe_w2': nrm((DEPTH, N_EXPERTS, EXPERT_FF, D), EXPERT_FF ** -0.5 * res_scale),
        'final_norm': gain((D,)),
    }


def reference(x, ev_norm, ev_w_in, ev_w_out, a_mu_rkv, a_mu_lora, a_w0, a_w1, a_w2,
              a_a0, a_a1, a_a2, a_g1, a_g2, a_k_k, a_k_a, a_r_k, a_ln_g, a_ln_b,
              a_mu_vres, a_v0, a_v1, a_v2, b_kv_norm, b_w_kv_up, b_idx_k_norm,
              od_norm, od_w_qkv, od_w_out, moe_norm, moe_router_group, moe_router_expert,
              moe_w1, moe_w3, moe_w2, final_norm):
    b, t, _ = x.shape
    split_points = [int(s) for s in np.cumsum(EVEN_SPLIT_SIZES)[:-1]]
    v_first = None
    for i in range(DEPTH):
        if i % 2 == 0:
            e = i // 2
            h = rmsnorm(x, ev_norm[e])
            r, k, v, q_b, c_kv, q_idx, k_idx, w_idx = jnp.split(h @ ev_w_in[e], split_points, axis=-1)
            vres = None if e == 0 else (v_first, a_mu_vres[e - 1], a_v0[e - 1], a_v1[e - 1], a_v2[e - 1])
            y_a, v_used = rwkv7_time_mix(h, r, k, v, a_mu_rkv[e], a_mu_lora[e], a_w0[e], a_w1[e], a_w2[e],
                                         a_a0[e], a_a1[e], a_a2[e], a_g1[e], a_g2[e], a_k_k[e], a_k_a[e],
                                         a_r_k[e], a_ln_g[e], a_ln_b[e], vres)
            if e == 0:
                v_first = v_used
            y_b = dsa_sparse_attention(q_b.reshape(b, t, B_HEADS, HEAD_DIM), c_kv,
                                       q_idx.reshape(b, t, IDX_HEADS, IDX_DIM), k_idx, w_idx,
                                       b_kv_norm[e], b_w_kv_up[e], b_idx_k_norm[e])
            x = x + jnp.concatenate([y_a, y_b], axis=-1) @ ev_w_out[e]
        else:
            o = i // 2
            h = rmsnorm(x, od_norm[o])
            q, k, v = jnp.split((h @ od_w_qkv[o]).reshape(b, t, 3, C_HEADS, HEAD_DIM), 3, axis=2)
            y_c = stick_breaking_attention(q[:, :, 0], k[:, :, 0], v[:, :, 0])
            x = x + y_c @ od_w_out[o]
        x = x + hierarchical_moe(rmsnorm(x, moe_norm[i]), moe_router_group[i], moe_router_expert[i],
                                 moe_w1[i], moe_w3[i], moe_w2[i])
    return rmsnorm(x, final_norm)
```

```python
import functools
import math

import jax
import jax.numpy as jnp
from jax import lax
from jax.experimental import pallas as pl
from jax.experimental.pallas import tpu as pltpu

F32 = jnp.float32
BF16 = jnp.bfloat16
I32 = jnp.int32
HI = lax.Precision.HIGHEST

LANES = 128
HEAD_DIM = 64
PAIR = 2 * HEAD_DIM
RMS_EPS = 1e-6
GN_EPS = 64e-5
N_GROUPS = 4
EXPERTS_PER_GROUP = 8
N_EXPERTS = N_GROUPS * EXPERTS_PER_GROUP
TOP_K_EXPERTS = 2
IDX_TOPK_MAX = 256
INT_MIN = -2147483648
NEG_BIG = -1e30
VMEM_LIMIT = 56 * 1024 * 1024

_NT = (((1,), (1,)), ((), ()))
_TN = (((0,), (0,)), ((), ()))


def _cparams(sem):
    return pltpu.CompilerParams(dimension_semantics=sem, vmem_limit_bytes=VMEM_LIMIT)


def _rms(x, g):
    return x * lax.rsqrt(jnp.mean(x * x, axis=-1, keepdims=True) + RMS_EPS) * g


def _dot(a, b):
    return jnp.dot(a, b, preferred_element_type=F32)


def _dot_hi(a, b):
    return jnp.dot(a, b, preferred_element_type=F32, precision=HI)


def _dot_nt(a, b):
    return lax.dot_general(a, b, _NT, preferred_element_type=F32)


def _lane_is_first_head(width=PAIR):
    return lax.broadcasted_iota(I32, (1, width), 1) % PAIR < HEAD_DIM


def _norm_proj_kernel(x_ref, g_ref, *refs, n_seg):
    w_refs, o_refs = refs[:n_seg], refs[n_seg:]
    h = _rms(x_ref[...], g_ref[...]).astype(BF16)
    for w_ref, o_ref in zip(w_refs, o_refs):
        o_ref[...] = _dot(h, w_ref[...]).astype(o_ref.dtype)


def norm_proj(x2d, g, ws, out_dtypes, tm=256):
    n, d = x2d.shape
    in_specs = [pl.BlockSpec((tm, d), lambda i: (i, 0)), pl.BlockSpec((1, d), lambda i: (0, 0))]
    in_specs += [pl.BlockSpec(w.shape, lambda i: (0, 0)) for w in ws]
    out_specs = [pl.BlockSpec((tm, w.shape[1]), lambda i: (i, 0)) for w in ws]
    out_shape = [jax.ShapeDtypeStruct((n, w.shape[1]), dt) for w, dt in zip(ws, out_dtypes)]
    return pl.pallas_call(
        functools.partial(_norm_proj_kernel, n_seg=len(ws)),
        grid=(n // tm,), in_specs=in_specs, out_specs=out_specs, out_shape=out_shape,
        compiler_params=_cparams(("parallel",)), name="norm_proj",
    )(x2d, g.reshape(1, d), *ws)


def _shift_rows(z, carry_ref):
    tm = z.shape[0]
    first = lax.broadcasted_iota(I32, (tm, 1), 0) == 0
    prev = jnp.where(first, carry_ref[...], pltpu.roll(z, 1, 0))
    carry_ref[...] = z[tm - 1:tm, :]
    return prev


def _rwkv_prep_kernel(*refs, has_vres):
    it = iter(refs)
    x_ref, rkv_ref = next(it), next(it)
    vfirst_ref = next(it) if has_vres else None
    g_ref, mu_rkv_ref, mu_lora_ref = next(it), next(it), next(it)
    w0_ref, w1_ref, w2_ref = next(it), next(it), next(it)
    a0_ref, a1_ref, a2_ref = next(it), next(it), next(it)
    g1_ref, g2_ref = next(it), next(it)
    if has_vres:
        muv_ref, v0_ref, v1_ref, v2_ref = next(it), next(it), next(it), next(it)
    kk_w_ref, ka_ref, rk_ref, seg_ref = next(it), next(it), next(it), next(it)
    r_o, lw_o, k_o, v_o, kk_o, a_o, gate_o, bonus_o = (next(it) for _ in range(8))
    hcarry, rkvcarry = next(it), next(it)

    @pl.when(pl.program_id(1) == 0)
    def _():
        hcarry[...] = jnp.zeros_like(hcarry)
        rkvcarry[...] = jnp.zeros_like(rkvcarry)

    aw = r_o.shape[-1]
    h = _rms(x_ref[0], g_ref[...])
    dh = _shift_rows(h, hcarry) - h
    rkv = rkv_ref[0]
    rkv = rkv + (_shift_rows(rkv, rkvcarry) - rkv) * mu_rkv_ref[...]
    r, k, v = rkv[:, :aw], rkv[:, aw:2 * aw], rkv[:, 2 * aw:]

    def lora_in(row):
        return (h + dh * mu_lora_ref[row:row + 1, :]).astype(BF16)

    dec = w0_ref[...] + _dot(jnp.tanh(_dot(lora_in(0), w1_ref[...])).astype(BF16), w2_ref[...])
    logw = -math.exp(-0.5) * jax.nn.sigmoid(dec)
    iclr = jax.nn.sigmoid(a0_ref[...] + _dot(_dot(lora_in(1), a1_ref[...]).astype(BF16), a2_ref[...]))
    gate = _dot(jax.nn.sigmoid(_dot(lora_in(2), g1_ref[...])).astype(BF16), g2_ref[...])
    if has_vres:
        xv = (h + dh * muv_ref[...]).astype(BF16)
        mix = jax.nn.sigmoid(v0_ref[...] + _dot(_dot(xv, v1_ref[...]).astype(BF16), v2_ref[...]))
        v = v + (vfirst_ref[0] - v) * mix

    seg = seg_ref[...]
    kk = k * kk_w_ref[...]
    kk = kk * lax.rsqrt(jnp.maximum(_dot_hi(kk * kk, seg), 1e-12))
    k_mod = k * (1.0 + (iclr - 1.0) * ka_ref[...])
    bonus = _dot_hi(r * k_mod * rk_ref[...], seg) * v

    r_o[0], lw_o[0], k_o[0], v_o[0] = r, logw, k_mod, v
    kk_o[0], a_o[0], gate_o[0], bonus_o[0] = kk, iclr, gate, bonus


def _pad_cols(w, n):
    return jnp.pad(w, ((0, 0), (0, n - w.shape[1])))


def _pad_rows(w, n):
    return jnp.pad(w, ((0, n - w.shape[0]), (0, 0)))


def _seg_ones(width):
    i = jnp.arange(width) // HEAD_DIM
    return (i[:, None] == i[None, :]).astype(F32)


def rwkv_prep(x, rkv, v_first, p, tm=256):
    b, t, d = x.shape
    aw = rkv.shape[-1] // 3
    has_vres = v_first is not None
    row = lambda a: a.reshape(1, -1)
    const = lambda a: pl.BlockSpec(a.shape, lambda i, j: (0,) * a.ndim)
    tile = lambda c: pl.BlockSpec((1, tm, c), lambda i, j: (i, j, 0))

    args, specs = [x, rkv], [tile(d), tile(3 * aw)]
    if has_vres:
        args.append(v_first)
        specs.append(tile(aw))
    consts = [row(p['norm']), p['mu_rkv'].reshape(1, 3 * aw), p['mu_lora'],
              row(p['w0']), _pad_cols(p['w1'], LANES).astype(BF16), _pad_rows(p['w2'], LANES).astype(BF16),
              row(p['a0']), _pad_cols(p['a1'], LANES).astype(BF16), _pad_rows(p['a2'], LANES).astype(BF16),
              p['g1'].astype(BF16), p['g2'].astype(BF16)]
    if has_vres:
        consts += [row(p['mu_v']), row(p['v0']), _pad_cols(p['v1'], LANES).astype(BF16),
                   _pad_rows(p['v2'], LANES).astype(BF16)]
    consts += [row(p['k_k']), row(p['k_a']), row(p['r_k']), _seg_ones(aw)]
    args += consts
    specs += [const(a) for a in consts]
    out_shape = [jax.ShapeDtypeStruct((b, t, aw), F32)] * 8
    return pl.pallas_call(
        functools.partial(_rwkv_prep_kernel, has_vres=has_vres),
        grid=(b, t // tm), in_specs=specs, out_specs=[tile(aw)] * 8, out_shape=out_shape,
        scratch_shapes=[pltpu.VMEM((1, d), F32), pltpu.VMEM((1, 3 * aw), F32)],
        compiler_params=_cparams(("arbitrary", "arbitrary")), name="rwkv_prep",
    )(*args)


def _rwkv_scan_kernel(r_ref, lw_ref, k_ref, v_ref, kk_ref, a_ref, y_ref, s_ref, *, chunk):
    L = chunk

    @pl.when(pl.program_id(1) == 0)
    def _():
        s_ref[...] = jnp.zeros_like(s_ref)

    n_pair = r_ref.shape[-1] // PAIR
    row = lax.broadcasted_iota(I32, (L, L), 0)
    col = lax.broadcasted_iota(I32, (L, L), 1)
    tri_incl = (col <= row).astype(F32)
    strict = col < row
    incl = col <= row
    eye = (col == row).astype(F32)
    first = _lane_is_first_head()
    blockdiag = (lax.broadcasted_iota(I32, (PAIR, PAIR), 0) // HEAD_DIM
                 == lax.broadcasted_iota(I32, (PAIR, PAIR), 1) // HEAD_DIM)

    for p in range(n_pair):
        sl = slice(p * PAIR, (p + 1) * PAIR)
        lw = lw_ref[0, :, sl]
        c = _dot_hi(tri_incl, lw)
        g = jnp.exp(c)
        g_prev = jnp.exp(c - lw)
        g_inv = jnp.exp(-c)
        g_last = g[L - 1:L, :]
        kk = kk_ref[0, :, sl]
        at = -kk * g_prev
        bt = kk * a_ref[0, :, sl] * g_inv
        kt = k_ref[0, :, sl] * g_inv
        rt = r_ref[0, :, sl] * g
        vv = v_ref[0, :, sl]
        s2 = s_ref[p]

        rhs_bk = jnp.concatenate([bt, kt], axis=0)
        a_ak, a_rbk, t_inv = [], [], []
        for m in (first, ~first):
            lhs = jnp.concatenate([jnp.where(m, at, 0.0), jnp.where(m, rt, 0.0)], axis=0)
            gm = _dot_nt_hi(lhs, rhs_bk)
            a_ab = jnp.where(strict, gm[:L, :L], 0.0)
            a_ak.append(jnp.where(strict, gm[:L, L:], 0.0))
            a_rbk.append(jnp.concatenate([jnp.where(incl, gm[L:, :L], 0.0),
                                          jnp.where(incl, gm[L:, L:], 0.0)], axis=1))
            inv = eye + a_ab
            apow = a_ab
            n_sq = max(1, (L - 1).bit_length()) - 1
            for _ in range(n_sq):
                apow = _dot_hi(apow, apow)
                inv = inv + _dot_hi(inv, apow)
            t_inv.append(inv)

        ars = _dot_nt_hi(jnp.concatenate([at, rt], axis=0), s2)
        a_s, r_s = ars[:L], ars[L:]
        rhs_u = a_s + jnp.where(first, _dot_hi(a_ak[0], vv), _dot_hi(a_ak[1], vv))
        u = jnp.where(first, _dot_hi(t_inv[0], rhs_u), _dot_hi(t_inv[1], rhs_u))
        uv = jnp.concatenate([u, vv], axis=0)
        y = r_s + jnp.where(first, _dot_hi(a_rbk[0], uv), _dot_hi(a_rbk[1], uv))
        y_ref[0, :, sl] = y
        upd = lax.dot_general(uv, jnp.concatenate([bt * g_last, kt * g_last], axis=0), _TN,
                              preferred_element_type=F32, precision=HI)
        s_ref[p] = s2 * g_last + jnp.where(blockdiag, upd, 0.0)


def _dot_nt_hi(a, b):
    return lax.dot_general(a, b, _NT, preferred_element_type=F32, precision=HI)


def rwkv_scan(r, lw, k, v, kk, a, chunk=64):
    b, t, aw = r.shape
    tile = pl.BlockSpec((1, chunk, aw), lambda i, j: (i, j, 0))
    return pl.pallas_call(
        functools.partial(_rwkv_scan_kernel, chunk=chunk),
        grid=(b, t // chunk), in_specs=[tile] * 6, out_specs=tile,
        out_shape=jax.ShapeDtypeStruct((b, t, aw), F32),
        scratch_shapes=[pltpu.VMEM((aw // PAIR, PAIR, PAIR), F32)],
        compiler_params=_cparams(("arbitrary", "arbitrary")), name="rwkv_scan",
    )(r, lw, k, v, kk, a)


def _kv_prep_kernel(c_ref, k_ref, gc_ref, gk_ref, cn_ref, kn_ref):
    cn_ref[...] = _rms(c_ref[...], gc_ref[...]).astype(cn_ref.dtype)
    kn_ref[...] = _rms(k_ref[...], gk_ref[...]).astype(kn_ref.dtype)


def kv_prep(c_kv, k_dup, g_c, g_k_dup, tm=512):
    n, w = c_kv.shape
    tile = pl.BlockSpec((tm, w), lambda i: (i, 0))
    const = pl.BlockSpec((1, w), lambda i: (0, 0))
    return pl.pallas_call(
        _kv_prep_kernel, grid=(n // tm,), in_specs=[tile, tile, const, const], out_specs=[tile, tile],
        out_shape=[jax.ShapeDtypeStruct((n, w), BF16)] * 2,
        compiler_params=_cparams(("parallel",)), name="kv_prep",
    )(c_kv, k_dup, g_c.reshape(1, w), g_k_dup.reshape(1, w))


def _dsa_kernel(qi_ref, wi_ref, q_ref, kn_ref, cn_ref, wk_ref, wv_ref, o_ref,
                skey_ref, qp_ref, acc_ref, m_ref, l_ref, *, qb, kt, topk, n_heads, idx_bits):
    blk = pl.program_id(1)
    nt = ((blk + 1) * qb + kt - 1) // kt
    first = _lane_is_first_head()
    q_pos = blk * qb + lax.broadcasted_iota(I32, (qb, 1), 0)
    col0 = lax.broadcasted_iota(I32, (1, kt), 1)

    def head_lanes(z2, h):
        return jnp.where(first if h % 2 == 0 else ~first, z2, 0.0)

    qi = qi_ref[0]
    wi = wi_ref[0]
    qi_h = [(head_lanes(qi[:, (h // 2) * PAIR:(h // 2 + 1) * PAIR], h) * HEAD_DIM ** -0.5).astype(BF16)
            for h in range(n_heads)]
    wcol = [wi[:, h:h + 1] for h in range(n_heads)]

    def score_tile(j, carry):
        off = pl.multiple_of(j * kt, kt)
        kn = kn_ref[0, pl.ds(off, kt), :]
        sc = jnp.zeros((qb, kt), F32)
        for h in range(n_heads):
            sc = sc + wcol[h] * jnp.maximum(_dot_nt(qi_h[h], kn), 0.0)
        sc = sc * n_heads ** -0.5
        bits = pltpu.bitcast(sc, I32)
        key = jnp.where(bits >= 0, bits, bits ^ 0x7FFFFFFF)
        key = jnp.where(sc == 0.0, 0, key)
        skey_ref[j] = jnp.where(off + col0 <= q_pos, key, INT_MIN)
        return carry

    lax.fori_loop(0, nt, score_tile, 0)

    def count(pred):
        def body(j, acc):
            hit = jnp.where(pred(skey_ref[j], j * kt + col0), 1.0, 0.0)
            part = hit[:, :LANES]
            for c in range(1, kt // LANES):
                part = part + hit[:, c * LANES:(c + 1) * LANES]
            return acc + part
        acc = lax.fori_loop(0, nt, body, jnp.zeros((qb, LANES), F32))
        return jnp.sum(acc, axis=1, keepdims=True)

    def thr_bit(i, prefix):
        cand = prefix | jnp.left_shift(jnp.int32(1), 31 - i)
        cnt = count(lambda sk, _: sk >= (cand ^ INT_MIN))
        return jnp.where(cnt >= topk, cand, prefix)

    thr = lax.fori_loop(0, 32, thr_bit, jnp.zeros((qb, 1), I32)) ^ INT_MIN
    n_gt = count(lambda sk, _: sk > thr)
    n_eq = count(lambda sk, _: sk == thr)
    need = topk - n_gt
    tie = (n_eq > need) & (thr != INT_MIN)
    any_tie = jnp.max(jnp.where(tie, 1, 0))

    def idx_bit(i, prefix):
        cand = prefix | jnp.left_shift(jnp.int32(1), idx_bits - 1 - i)
        cnt = count(lambda sk, cidx: (sk == thr) & (cidx < cand))
        return jnp.where(cnt < need, cand, prefix)

    jmax = lax.fori_loop(0, idx_bits * any_tie, idx_bit, jnp.zeros((qb, 1), I32))
    jmax = jnp.where(tie, jmax, 0x7FFFFFFF)

    q = q_ref[0]
    for h in range(n_heads):
        pr = slice((h // 2) * PAIR, (h // 2 + 1) * PAIR)
        qh = head_lanes(q[:, pr], h).astype(BF16)
        qp_ref[h] = (_dot_nt(qh, wk_ref[:, pr]) * HEAD_DIM ** -0.5).astype(BF16)
    m_ref[...] = jnp.full_like(m_ref, NEG_BIG)
    l_ref[...] = jnp.zeros_like(l_ref)
    acc_ref[...] = jnp.zeros_like(acc_ref)

    def attn_tile(j, carry):
        off = pl.multiple_of(j * kt, kt)
        sk = skey_ref[j]
        cidx = off + col0
        sel = ((sk > thr) | ((sk == thr) & (cidx <= jmax))) & (cidx <= q_pos)
        bias = jnp.where(sel, 0.0, NEG_BIG)
        cn = cn_ref[0, pl.ds(off, kt), :]
        for h in range(n_heads):
            s = _dot_nt(qp_ref[h], cn) + bias
            m_old = m_ref[h]
            m_new = jnp.maximum(m_old, jnp.max(s, axis=1, keepdims=True))
            alpha = jnp.exp(m_old - m_new)
            pexp = jnp.where(sel, jnp.exp(s - m_new), 0.0)
            l_ref[h] = alpha * l_ref[h] + jnp.sum(pexp, axis=1, keepdims=True)
            acc_ref[h] = alpha * acc_ref[h] + _dot(pexp.astype(BF16), cn)
            m_ref[h] = m_new
        return carry

    lax.fori_loop(0, nt, attn_tile, 0)

    for p in range(n_heads // 2):
        pr = slice(p * PAIR, (p + 1) * PAIR)
        wv = wv_ref[:, pr]
        out = jnp.zeros((qb, PAIR), F32)
        for h in (2 * p, 2 * p + 1):
            lat = (acc_ref[h] / l_ref[h]).astype(BF16)
            out = out + _dot(lat, head_lanes(wv, h).astype(BF16))
        o_ref[0, :, pr] = out.astype(o_ref.dtype)


def dsa_attention(q_idx, w_idx, q_b, kn, cn, wk_t, wv, qb=128, kt=512):
    b, t, width = q_b.shape
    n_heads = width // HEAD_DIM
    kt = min(kt, t)
    topk = min(IDX_TOPK_MAX, t // 4)
    qtile = lambda c: pl.BlockSpec((1, qb, c), lambda i, j: (i, j, 0))
    full = lambda c: pl.BlockSpec((1, t, c), lambda i, j: (i, 0, 0))
    const = lambda a: pl.BlockSpec(a.shape, lambda i, j: (0, 0))
    kern = functools.partial(_dsa_kernel, qb=qb, kt=kt, topk=topk, n_heads=n_heads,
                             idx_bits=max(1, (t - 1).bit_length()))
    return pl.pallas_call(
        kern, grid=(b, t // qb),
        in_specs=[qtile(width), qtile(LANES), qtile(width), full(LANES), full(LANES), const(wk_t), const(wv)],
        out_specs=qtile(width), out_shape=jax.ShapeDtypeStruct((b, t, width), BF16),
        scratch_shapes=[pltpu.VMEM((t // kt, qb, kt), I32), pltpu.VMEM((n_heads, qb, LANES), BF16),
                        pltpu.VMEM((n_heads, qb, LANES), F32), pltpu.VMEM((n_heads, qb, 1), F32),
                        pltpu.VMEM((n_heads, qb, 1), F32)],
        compiler_params=_cparams(("parallel", "arbitrary")), name="dsa_attention",
    )(q_idx, w_idx, q_b, kn, cn, wk_t, wv)


def _even_out_kernel(x_ref, y_ref, bonus_ref, gate_ref, yb_ref, lng_ref, lnb_ref, seg_ref, wa_ref, wb_ref, o_ref):
    seg = seg_ref[...]
    y = y_ref[...]
    yc = y - _dot_hi(y, seg) * (1.0 / HEAD_DIM)
    yn = yc * lax.rsqrt(_dot_hi(yc * yc, seg) * (1.0 / HEAD_DIM) + GN_EPS)
    ya = (yn * lng_ref[...] + lnb_ref[...] + bonus_ref[...]) * gate_ref[...]
    o_ref[...] = x_ref[...] + _dot(ya.astype(BF16), wa_ref[...]) + _dot(yb_ref[...], wb_ref[...])


def even_out(x2d, y, bonus, gate, y_b, ln_g, ln_b, w_out, tm=256):
    n, d = x2d.shape
    aw = y.shape[1]
    tile = lambda c: pl.BlockSpec((tm, c), lambda i: (i, 0))
    const = lambda a: pl.BlockSpec(a.shape, lambda i: (0, 0))
    consts = [ln_g.reshape(1, aw), ln_b.reshape(1, aw), _seg_ones(aw),
              w_out[:aw].astype(BF16), w_out[aw:].astype(BF16)]
    return pl.pallas_call(
        _even_out_kernel, grid=(n // tm,),
        in_specs=[tile(d), tile(aw), tile(aw), tile(aw), tile(y_b.shape[1])] + [const(a) for a in consts],
        out_specs=tile(d), out_shape=jax.ShapeDtypeStruct((n, d), F32),
        compiler_params=_cparams(("parallel",)), name="even_out",
    )(x2d, y, bonus, gate, y_b, *consts)


def _stickbreak_kernel(q_ref, k_ref, v_ref, o_ref, acc_ref, carry_ref, *, tq):
    qi = pl.program_id(2)
    first = _lane_is_first_head()
    row = lax.broadcasted_iota(I32, (tq, tq), 0)
    col = lax.broadcasted_iota(I32, (tq, tq), 1)
    before = col < row
    later = (row > col).astype(BF16)
    q2 = q_ref[0]
    acc_ref[...] = jnp.zeros_like(acc_ref)

    for m in (first, ~first):
        qh = jnp.where(m, q2, 0.0) * HEAD_DIM ** -0.5

        def tile(j, diag):
            off = pl.multiple_of(j * tq, tq)
            z = _dot_nt(qh, k_ref[0, pl.ds(off, tq), :])
            lk = -(jnp.maximum(z, 0.0) + jnp.log1p(jnp.exp(-jnp.abs(z))))
            if diag:
                lk = jnp.where(before, lk, 0.0)
            lk_hi = lk.astype(BF16)
            lk_lo = (lk - lk_hi.astype(F32)).astype(BF16)
            tail = _dot(lk_hi, later) + _dot(lk_lo, later) + carry_ref[...]
            w = jnp.exp(z + lk + tail)
            if diag:
                w = jnp.where(before, w, 0.0)
            vh = jnp.where(m, v_ref[0, pl.ds(off, tq), :], 0.0)
            acc_ref[...] += _dot(w.astype(BF16), vh)
            carry_ref[...] += jnp.sum(lk, axis=1, keepdims=True)
            return (jnp.max(carry_ref[...]) > -110.0).astype(I32)

        carry_ref[...] = jnp.zeros_like(carry_ref)
        live = tile(qi, True)

        def cond(st):
            return (st[0] >= 0) & (st[1] > 0)

        def body(st):
            return st[0] - 1, tile(st[0], False)

        lax.while_loop(cond, body, (qi - 1, live))

    o_ref[0] = acc_ref[...].astype(o_ref.dtype)


def stickbreak_attention(qkv, n_heads, tq=256):
    b, t, _ = qkv.shape
    n_pair = n_heads // 2
    tq = min(tq, t)
    q_spec = pl.BlockSpec((1, tq, PAIR), lambda i, p, j: (i, j, p))
    k_spec = pl.BlockSpec((1, t, PAIR), lambda i, p, j: (i, 0, n_pair + p))
    v_spec = pl.BlockSpec((1, t, PAIR), lambda i, p, j: (i, 0, 2 * n_pair + p))
    return pl.pallas_call(
        functools.partial(_stickbreak_kernel, tq=tq), grid=(b, n_pair, t // tq),
        in_specs=[q_spec, k_spec, v_spec], out_specs=q_spec,
        out_shape=jax.ShapeDtypeStruct((b, t, n_heads * HEAD_DIM), BF16),
        scratch_shapes=[pltpu.VMEM((tq, PAIR), F32), pltpu.VMEM((tq, 1), F32)],
        compiler_params=_cparams(("parallel", "parallel", "arbitrary")), name="stickbreak",
    )(qkv, qkv, qkv)


def _proj_residual_kernel(x_ref, y_ref, w_ref, o_ref):
    o_ref[...] = x_ref[...] + _dot(y_ref[...], w_ref[...])


def proj_residual(x2d, y, w, tm=256):
    n, d = x2d.shape
    c = y.shape[1]
    return pl.pallas_call(
        _proj_residual_kernel, grid=(n // tm,),
        in_specs=[pl.BlockSpec((tm, d), lambda i: (i, 0)), pl.BlockSpec((tm, c), lambda i: (i, 0)),
                  pl.BlockSpec(w.shape, lambda i: (0, 0))],
        out_specs=pl.BlockSpec((tm, d), lambda i: (i, 0)), out_shape=jax.ShapeDtypeStruct((n, d), F32),
        compiler_params=_cparams(("parallel",)), name="proj_residual",
    )(x2d, y, w)


def _router_kernel(x_ref, g_ref, wr_ref, h_ref, ids_ref, gates_ref):
    h = _rms(x_ref[...], g_ref[...])
    h_ref[...] = h.astype(h_ref.dtype)
    logits = _dot_hi(h, wr_ref[...])
    lane = lax.broadcasted_iota(I32, logits.shape, 1)
    lane_f = lane.astype(F32)

    def top1(vals):
        mx = jnp.max(vals, axis=1, keepdims=True)
        return mx, jnp.min(jnp.where(vals == mx, lane_f, 1e9), axis=1, keepdims=True).astype(I32)

    is_g = lane < N_GROUPS
    g_max, g_sel = top1(jnp.where(is_g, logits, -jnp.inf))
    g_gate = 1.0 / jnp.sum(jnp.where(is_g, jnp.exp(logits - g_max), 0.0), axis=1, keepdims=True)
    e_lane = lane - N_GROUPS
    in_grp = (e_lane >= g_sel * EXPERTS_PER_GROUP) & (e_lane < (g_sel + 1) * EXPERTS_PER_GROUP)
    el = jnp.where(in_grp, logits, -jnp.inf)
    m1, i1 = top1(el)
    m2, i2 = top1(jnp.where(lane == i1, -jnp.inf, el))
    e21 = jnp.exp(m2 - m1)
    p1 = 1.0 / (1.0 + e21)
    ids_ref[...] = jnp.where(lane == 0, i1 - N_GROUPS, jnp.where(lane == 1, i2 - N_GROUPS, 0))
    gates_ref[...] = jnp.where(lane == 0, g_gate * p1, jnp.where(lane == 1, g_gate * (e21 * p1), 0.0))


def moe_router(x2d, g, w_router, tm=256):
    n, d = x2d.shape
    tile = lambda c: pl.BlockSpec((tm, c), lambda i: (i, 0))
    return pl.pallas_call(
        _router_kernel, grid=(n // tm,),
        in_specs=[tile(d), pl.BlockSpec((1, d), lambda i: (0, 0)), pl.BlockSpec(w_router.shape, lambda i: (0, 0))],
        out_specs=[tile(d), tile(LANES), tile(LANES)],
        out_shape=[jax.ShapeDtypeStruct((n, d), BF16), jax.ShapeDtypeStruct((n, LANES), I32),
                   jax.ShapeDtypeStruct((n, LANES), F32)],
        compiler_params=_cparams(("parallel",)), name="moe_router",
    )(x2d, g.reshape(1, d), w_router)


def _expert_kernel(be_ref, nb_ref, x_ref, w1_ref, w3_ref, w2_ref, o_ref):
    @pl.when(pl.program_id(0) < nb_ref[0])
    def _():
        x = x_ref[...]
        a = _dot(x, w1_ref[0])
        hdn = (a * jax.nn.sigmoid(a)) * _dot(x, w3_ref[0])
        o_ref[...] = _dot(hdn.astype(BF16), w2_ref[0]).astype(o_ref.dtype)


def moe_experts(x_rows, blk_expert, n_used, w1, w3, w2, tm):
    n_rows, d = x_rows.shape
    ff = w1.shape[-1]
    n_blocks = n_rows // tm
    grid_spec = pltpu.PrefetchScalarGridSpec(
        num_scalar_prefetch=2, grid=(n_blocks,),
        in_specs=[pl.BlockSpec((tm, d), lambda i, be, nb: (jnp.minimum(i, nb[0] - 1), 0)),
                  pl.BlockSpec((1, d, ff), lambda i, be, nb: (be[i], 0, 0)),
                  pl.BlockSpec((1, d, ff), lambda i, be, nb: (be[i], 0, 0)),
                  pl.BlockSpec((1, ff, d), lambda i, be, nb: (be[i], 0, 0))],
        out_specs=pl.BlockSpec((tm, d), lambda i, be, nb: (jnp.minimum(i, nb[0] - 1), 0)))
    return pl.pallas_call(
        _expert_kernel, grid_spec=grid_spec, out_shape=jax.ShapeDtypeStruct((n_rows, d), F32),
        compiler_params=_cparams(("arbitrary",)), name="moe_experts",
    )(blk_expert, n_used, x_rows, w1, w3, w2)


def _final_norm_kernel(x_ref, g_ref, o_ref):
    o_ref[...] = _rms(x_ref[...], g_ref[...])


def final_norm(x2d, g, tm=512):
    n, d = x2d.shape
    return pl.pallas_call(
        _final_norm_kernel, grid=(n // tm,),
        in_specs=[pl.BlockSpec((tm, d), lambda i: (i, 0)), pl.BlockSpec((1, d), lambda i: (0, 0))],
        out_specs=pl.BlockSpec((tm, d), lambda i: (i, 0)), out_shape=jax.ShapeDtypeStruct((n, d), F32),
        compiler_params=_cparams(("parallel",)), name="final_norm",
    )(x2d, g.reshape(1, d))


MOE_TM = 256


def moe_layer(x2d, norm_g, router_group, router_expert, w1, w3, w2):
    n, d = x2d.shape
    w_router = jnp.concatenate([router_group, jnp.moveaxis(router_expert, 0, 1).reshape(d, N_EXPERTS)], axis=1)
    h, ids, gates = moe_router(x2d, norm_g, _pad_cols(w_router, LANES))
    expert_idx, gate = ids[:, :TOP_K_EXPERTS], gates[:, :TOP_K_EXPERTS]
    n_asg = n * TOP_K_EXPERTS
    flat_e = expert_idx.reshape(n_asg)
    onehot = (flat_e[:, None] == jnp.arange(N_EXPERTS)[None, :]).astype(I32)
    csum = jnp.cumsum(onehot, axis=0)
    counts = csum[-1]
    rank = jnp.sum((csum - onehot) * onehot, axis=1)
    padded = (counts + MOE_TM - 1) // MOE_TM * MOE_TM
    pad_end = jnp.cumsum(padded)
    dest = (pad_end - padded)[flat_e] + rank
    n_blocks = n_asg // MOE_TM + N_EXPERTS
    src_tok = jnp.zeros((n_blocks * MOE_TM,), I32).at[dest].set(jnp.arange(n_asg, dtype=I32) // TOP_K_EXPERTS)
    blk_expert = jnp.minimum(jnp.searchsorted(pad_end, jnp.arange(n_blocks) * MOE_TM, side='right'),
                             N_EXPERTS - 1).astype(I32)
    n_used = (pad_end[-1:] // MOE_TM).astype(I32)
    y_rows = moe_experts(h[src_tok], blk_expert, n_used, w1.astype(BF16), w3.astype(BF16), w2.astype(BF16), MOE_TM)
    y_asg = y_rows[dest].reshape(n, TOP_K_EXPERTS, d)
    return x2d + jnp.sum(y_asg * gate[..., None], axis=1)


def even_layer(x, v_first, p):
    b, t, d = x.shape
    n = b * t
    w_in = p['w_in']
    aw = p['k_k'].shape[0]
    bw = p['w_kv_up'].shape[1] // 2
    rank = p['w_kv_up'].shape[0]
    n_idx = w_in.shape[1] - (3 * aw + bw + rank + bw + HEAD_DIM)
    o = 0
    segs = []
    for width in (3 * aw, bw, rank, bw, HEAD_DIM, n_idx):
        segs.append(w_in[:, o:o + width])
        o += width
    segs[4] = jnp.concatenate([segs[4], segs[4]], axis=1)
    segs[5] = _pad_cols(segs[5], LANES)
    rkv, q_b, c_kv, q_idx, k_dup, w_idx = norm_proj(
        x.reshape(n, d), p['norm'], [s.astype(BF16) for s in segs], [F32] * 6)

    r, lw, k_mod, v, kk, iclr, gate, bonus = rwkv_prep(x, rkv.reshape(b, t, 3 * aw), v_first, p)
    y = rwkv_scan(r, lw, k_mod, v, kk, iclr)

    cn, kn = kv_prep(c_kv, k_dup, p['kv_norm'], jnp.concatenate([p['idx_k_norm']] * 2))
    wk_t = p['w_kv_up'][:, :bw].astype(BF16)
    wv = p['w_kv_up'][:, bw:].astype(BF16)
    y_b = dsa_attention(q_idx.reshape(b, t, bw), w_idx.reshape(b, t, LANES), q_b.reshape(b, t, bw),
                        kn.reshape(b, t, LANES), cn.reshape(b, t, rank), wk_t, wv)

    x_new = even_out(x.reshape(n, d), y.reshape(n, aw), bonus.reshape(n, aw), gate.reshape(n, aw),
                     y_b.reshape(n, bw), p['ln_g'], p['ln_b'], p['w_out'])
    return x_new.reshape(b, t, d), v


def odd_layer(x, norm_g, w_qkv, w_out):
    b, t, d = x.shape
    n = b * t
    cw = w_qkv.shape[1] // 3
    (qkv,) = norm_proj(x.reshape(n, d), norm_g, [w_qkv.astype(BF16)], [BF16])
    y_c = stickbreak_attention(qkv.reshape(b, t, 3 * cw), cw // HEAD_DIM)
    return proj_residual(x.reshape(n, d), y_c.reshape(n, cw), w_out.astype(BF16)).reshape(b, t, d)


@jax.jit
def _forward(x, ev_norm, ev_w_in, ev_w_out, a_mu_rkv, a_mu_lora, a_w0, a_w1, a_w2, a_a0, a_a1, a_a2, a_g1,
             a_g2, a_k_k, a_k_a, a_r_k, a_ln_g, a_ln_b, a_mu_vres, a_v0, a_v1, a_v2, b_kv_norm, b_w_kv_up,
             b_idx_k_norm, od_norm, od_w_qkv, od_w_out, moe_norm, moe_router_group, moe_router_expert,
             moe_w1, moe_w3, moe_w2, final_norm_g):
    b, t, d = x.shape
    depth = moe_norm.shape[0]
    v_first = None
    for i in range(depth):
        if i % 2 == 0:
            e = i // 2
            p = dict(norm=ev_norm[e], w_in=ev_w_in[e], w_out=ev_w_out[e], mu_rkv=a_mu_rkv[e], mu_lora=a_mu_lora[e],
                     w0=a_w0[e], w1=a_w1[e], w2=a_w2[e], a0=a_a0[e], a1=a_a1[e], a2=a_a2[e], g1=a_g1[e], g2=a_g2[e],
                     k_k=a_k_k[e], k_a=a_k_a[e], r_k=a_r_k[e].reshape(-1), ln_g=a_ln_g[e], ln_b=a_ln_b[e],
                     kv_norm=b_kv_norm[e], w_kv_up=b_w_kv_up[e], idx_k_norm=b_idx_k_norm[e])
            if e > 0:
                p.update(mu_v=a_mu_vres[e - 1], v0=a_v0[e - 1], v1=a_v1[e - 1], v2=a_v2[e - 1])
            x, v_used = even_layer(x, v_first if e > 0 else None, p)
            if e == 0:
                v_first = v_used
        else:
            o = i // 2
            x = odd_layer(x, od_norm[o], od_w_qkv[o], od_w_out[o])
        x = moe_layer(x.reshape(b * t, d), moe_norm[i], moe_router_group[i], moe_router_expert[i],
                      moe_w1[i], moe_w3[i], moe_w2[i]).reshape(b, t, d)
    return final_norm(x.reshape(b * t, d), final_norm_g).reshape(b, t, d)


def kernel(x, ev_norm, ev_w_in, ev_w_out, a_mu_rkv, a_mu_lora, a_w0, a_w1, a_w2, a_a0, a_a1, a_a2, a_g1, a_g2, a_k_k, a_k_a, a_r_k, a_ln_g, a_ln_b, a_mu_vres, a_v0, a_v1, a_v2, b_kv_norm, b_w_kv_up, b_idx_k_norm, od_norm, od_w_qkv, od_w_out, moe_norm, moe_router_group, moe_router_expert, moe_w1, moe_w3, moe_w2, final_norm):
    return _forward(x, ev_norm, ev_w_in, ev_w_out, a_mu_rkv, a_mu_lora, a_w0, a_w1, a_w2, a_a0, a_a1, a_a2, a_g1,
                    a_g2, a_k_k, a_k_a, a_r_k, a_ln_g, a_ln_b, a_mu_vres, a_v0, a_v1, a_v2, b_kv_norm, b_w_kv_up,
                    b_idx_k_norm, od_norm, od_w_qkv, od_w_out, moe_norm, moe_router_group, moe_router_expert,
                    moe_w1, moe_w3, moe_w2, final_norm)
```

```python
import functools
import math

import jax
import jax.numpy as jnp
from jax import lax
from jax.experimental import pallas as pl
from jax.experimental.pallas import tpu as pltpu

F32 = jnp.float32
BF16 = jnp.bfloat16
I32 = jnp.int32
HI = lax.Precision.HIGHEST

LANES = 128
HEAD_DIM = 64
PAIR = 2 * HEAD_DIM
RMS_EPS = 1e-6
GN_EPS = 64e-5
N_GROUPS = 4
EXPERTS_PER_GROUP = 8
N_EXPERTS = N_GROUPS * EXPERTS_PER_GROUP
TOP_K_EXPERTS = 2
IDX_TOPK_MAX = 256
INT_MIN = -2147483648
NEG_BIG = -1e30
VMEM_LIMIT = 56 * 1024 * 1024

_NT = (((1,), (1,)), ((), ()))
_TN = (((0,), (0,)), ((), ()))


def _cparams(sem):
    return pltpu.CompilerParams(dimension_semantics=sem, vmem_limit_bytes=VMEM_LIMIT)


def _rms(x, g):
    return x * lax.rsqrt(jnp.mean(x * x, axis=-1, keepdims=True) + RMS_EPS) * g


def _dot(a, b):
    return jnp.dot(a, b, preferred_element_type=F32)


def _dot_hi(a, b):
    return jnp.dot(a, b, preferred_element_type=F32, precision=HI)


def _dot_nt(a, b):
    return lax.dot_general(a, b, _NT, preferred_element_type=F32)


def _lane_is_first_head(width=PAIR):
    return lax.broadcasted_iota(I32, (1, width), 1) % PAIR < HEAD_DIM


def _norm_proj_kernel(x_ref, g_ref, *refs, n_seg):
    w_refs, o_refs = refs[:n_seg], refs[n_seg:]
    h = _rms(x_ref[...], g_ref[...]).astype(BF16)
    for w_ref, o_ref in zip(w_refs, o_refs):
        o_ref[...] = _dot(h, w_ref[...]).astype(o_ref.dtype)


def norm_proj(x2d, g, ws, out_dtypes, tm=256):
    n, d = x2d.shape
    in_specs = [pl.BlockSpec((tm, d), lambda i: (i, 0)), pl.BlockSpec((1, d), lambda i: (0, 0))]
    in_specs += [pl.BlockSpec(w.shape, lambda i: (0, 0)) for w in ws]
    out_specs = [pl.BlockSpec((tm, w.shape[1]), lambda i: (i, 0)) for w in ws]
    out_shape = [jax.ShapeDtypeStruct((n, w.shape[1]), dt) for w, dt in zip(ws, out_dtypes)]
    return pl.pallas_call(
        functools.partial(_norm_proj_kernel, n_seg=len(ws)),
        grid=(n // tm,), in_specs=in_specs, out_specs=out_specs, out_shape=out_shape,
        compiler_params=_cparams(("parallel",)), name="norm_proj",
    )(x2d, g.reshape(1, d), *ws)


def _shift_rows(z, carry_ref):
    tm = z.shape[0]
    first = lax.broadcasted_iota(I32, (tm, 1), 0) == 0
    prev = jnp.where(first, carry_ref[...], pltpu.roll(z, 1, 0))
    carry_ref[...] = z[tm - 1:tm, :]
    return prev


def _rwkv_prep_kernel(*refs, has_vres):
    it = iter(refs)
    x_ref, rkv_ref = next(it), next(it)
    vfirst_ref = next(it) if has_vres else None
    g_ref, mu_rkv_ref, mu_lora_ref = next(it), next(it), next(it)
    w0_ref, w1_ref, w2_ref = next(it), next(it), next(it)
    a0_ref, a1_ref, a2_ref = next(it), next(it), next(it)
    g1_ref, g2_ref = next(it), next(it)
    if has_vres:
        muv_ref, v0_ref, v1_ref, v2_ref = next(it), next(it), next(it), next(it)
    kk_w_ref, ka_ref, rk_ref, seg_ref = next(it), next(it), next(it), next(it)
    r_o, lw_o, k_o, v_o, kk_o, a_o, gate_o, bonus_o = (next(it) for _ in range(8))
    hcarry, rkvcarry = next(it), next(it)

    @pl.when(pl.program_id(1) == 0)
    def _():
        hcarry[...] = jnp.zeros_like(hcarry)
        rkvcarry[...] = jnp.zeros_like(rkvcarry)

    aw = r_o.shape[-1]
    h = _rms(x_ref[0], g_ref[...])
    dh = _shift_rows(h, hcarry) - h
    rkv = rkv_ref[0]
    rkv = rkv + (_shift_rows(rkv, rkvcarry) - rkv) * mu_rkv_ref[...]
    r, k, v = rkv[:, :aw], rkv[:, aw:2 * aw], rkv[:, 2 * aw:]

    def lora_in(row):
        return (h + dh * mu_lora_ref[row:row + 1, :]).astype(BF16)

    dec = w0_ref[...] + _dot(jnp.tanh(_dot(lora_in(0), w1_ref[...])).astype(BF16), w2_ref[...])
    logw = -math.exp(-0.5) * jax.nn.sigmoid(dec)
    iclr = jax.nn.sigmoid(a0_ref[...] + _dot(_dot(lora_in(1), a1_ref[...]).astype(BF16), a2_ref[...]))
    gate = _dot(jax.nn.sigmoid(_dot(lora_in(2), g1_ref[...])).astype(BF16), g2_ref[...])
    if has_vres:
        xv = (h + dh * muv_ref[...]).astype(BF16)
        mix = jax.nn.sigmoid(v0_ref[...] + _dot(_dot(xv, v1_ref[...]).astype(BF16), v2_ref[...]))
        v = v + (vfirst_ref[0] - v) * mix

    seg = seg_ref[...]
    kk = k * kk_w_ref[...]
    kk = kk * lax.rsqrt(jnp.maximum(_dot_hi(kk * kk, seg), 1e-12))
    k_mod = k * (1.0 + (iclr - 1.0) * ka_ref[...])
    bonus = _dot_hi(r * k_mod * rk_ref[...], seg) * v

    r_o[0], lw_o[0], k_o[0], v_o[0] = r, logw, k_mod, v
    kk_o[0], a_o[0], gate_o[0], bonus_o[0] = kk, iclr, gate, bonus


def _pad_cols(w, n):
    return jnp.pad(w, ((0, 0), (0, n - w.shape[1])))


def _pad_rows(w, n):
    return jnp.pad(w, ((0, n - w.shape[0]), (0, 0)))


def _seg_ones(width):
    i = jnp.arange(width) // HEAD_DIM
    return (i[:, None] == i[None, :]).astype(F32)


def rwkv_prep(x, rkv, v_first, p, tm=256):
    b, t, d = x.shape
    aw = rkv.shape[-1] // 3
    has_vres = v_first is not None
    row = lambda a: a.reshape(1, -1)
    const = lambda a: pl.BlockSpec(a.shape, lambda i, j: (0,) * a.ndim)
    tile = lambda c: pl.BlockSpec((1, tm, c), lambda i, j: (i, j, 0))

    args, specs = [x, rkv], [tile(d), tile(3 * aw)]
    if has_vres:
        args.append(v_first)
        specs.append(tile(aw))
    consts = [row(p['norm']), p['mu_rkv'].reshape(1, 3 * aw), p['mu_lora'],
              row(p['w0']), _pad_cols(p['w1'], LANES).astype(BF16), _pad_rows(p['w2'], LANES).astype(BF16),
              row(p['a0']), _pad_cols(p['a1'], LANES).astype(BF16), _pad_rows(p['a2'], LANES).astype(BF16),
              p['g1'].astype(BF16), p['g2'].astype(BF16)]
    if has_vres:
        consts += [row(p['mu_v']), row(p['v0']), _pad_cols(p['v1'], LANES).astype(BF16),
                   _pad_rows(p['v2'], LANES).astype(BF16)]
    consts += [row(p['k_k']), row(p['k_a']), row(p['r_k']), _seg_ones(aw)]
    args += consts
    specs += [const(a) for a in consts]
    out_shape = [jax.ShapeDtypeStruct((b, t, aw), F32)] * 8
    return pl.pallas_call(
        functools.partial(_rwkv_prep_kernel, has_vres=has_vres),
        grid=(b, t // tm), in_specs=specs, out_specs=[tile(aw)] * 8, out_shape=out_shape,
        scratch_shapes=[pltpu.VMEM((1, d), F32), pltpu.VMEM((1, 3 * aw), F32)],
        compiler_params=_cparams(("arbitrary", "arbitrary")), name="rwkv_prep",
    )(*args)


def _rwkv_scan_kernel(r_ref, lw_ref, k_ref, v_ref, kk_ref, a_ref, y_ref, s_ref, *, chunk):
    L = chunk

    @pl.when(pl.program_id(1) == 0)
    def _():
        s_ref[...] = jnp.zeros_like(s_ref)

    n_pair = r_ref.shape[-1] // PAIR
    row = lax.broadcasted_iota(I32, (L, L), 0)
    col = lax.broadcasted_iota(I32, (L, L), 1)
    tri_incl = (col <= row).astype(F32)
    strict = col < row
    incl = col <= row
    eye = (col == row).astype(F32)
    first = _lane_is_first_head()
    blockdiag = (lax.broadcasted_iota(I32, (PAIR, PAIR), 0) // HEAD_DIM
                 == lax.broadcasted_iota(I32, (PAIR, PAIR), 1) // HEAD_DIM)

    for p in range(n_pair):
        sl = slice(p * PAIR, (p + 1) * PAIR)
        lw = lw_ref[0, :, sl]
        c = _dot_hi(tri_incl, lw)
        g = jnp.exp(c)
        g_prev = jnp.exp(c - lw)
        g_inv = jnp.exp(-c)
        g_last = g[L - 1:L, :]
        kk = kk_ref[0, :, sl]
        at = -kk * g_prev
        bt = kk * a_ref[0, :, sl] * g_inv
        kt = k_ref[0, :, sl] * g_inv
        rt = r_ref[0, :, sl] * g
        vv = v_ref[0, :, sl]
        s2 = s_ref[p]

        rhs_bk = jnp.concatenate([bt, kt], axis=0)
        a_ak, a_rbk, t_inv = [], [], []
        for m in (first, ~first):
            lhs = jnp.concatenate([jnp.where(m, at, 0.0), jnp.where(m, rt, 0.0)], axis=0)
            gm = _dot_nt_hi(lhs, rhs_bk)
            a_ab = jnp.where(strict, gm[:L, :L], 0.0)
            a_ak.append(jnp.where(strict, gm[:L, L:], 0.0))
            a_rbk.append(jnp.concatenate([jnp.where(incl, gm[L:, :L], 0.0),
                                          jnp.where(incl, gm[L:, L:], 0.0)], axis=1))
            inv = eye + a_ab
            apow = a_ab
            n_sq = max(1, (L - 1).bit_length()) - 1
            for _ in range(n_sq):
                apow = _dot_hi(apow, apow)
                inv = inv + _dot_hi(inv, apow)
            t_inv.append(inv)

        ars = _dot_nt_hi(jnp.concatenate([at, rt], axis=0), s2)
        a_s, r_s = ars[:L], ars[L:]
        rhs_u = a_s + jnp.where(first, _dot_hi(a_ak[0], vv), _dot_hi(a_ak[1], vv))
        u = jnp.where(first, _dot_hi(t_inv[0], rhs_u), _dot_hi(t_inv[1], rhs_u))
        uv = jnp.concatenate([u, vv], axis=0)
        y = r_s + jnp.where(first, _dot_hi(a_rbk[0], uv), _dot_hi(a_rbk[1], uv))
        y_ref[0, :, sl] = y
        upd = lax.dot_general(uv, jnp.concatenate([bt * g_last, kt * g_last], axis=0), _TN,
                              preferred_element_type=F32, precision=HI)
        s_ref[p] = s2 * g_last + jnp.where(blockdiag, upd, 0.0)


def _dot_nt_hi(a, b):
    return lax.dot_general(a, b, _NT, preferred_element_type=F32, precision=HI)


def rwkv_scan(r, lw, k, v, kk, a, chunk=64):
    b, t, aw = r.shape
    tile = pl.BlockSpec((1, chunk, aw), lambda i, j: (i, j, 0))
    return pl.pallas_call(
        functools.partial(_rwkv_scan_kernel, chunk=chunk),
        grid=(b, t // chunk), in_specs=[tile] * 6, out_specs=tile,
        out_shape=jax.ShapeDtypeStruct((b, t, aw), F32),
        scratch_shapes=[pltpu.VMEM((aw // PAIR, PAIR, PAIR), F32)],
        compiler_params=_cparams(("arbitrary", "arbitrary")), name="rwkv_scan",
    )(r, lw, k, v, kk, a)


def _kv_prep_kernel(c_ref, k_ref, gc_ref, gk_ref, cn_ref, kn_ref):
    cn_ref[...] = _rms(c_ref[...], gc_ref[...]).astype(cn_ref.dtype)
    kn_ref[...] = _rms(k_ref[...], gk_ref[...]).astype(kn_ref.dtype)


def kv_prep(c_kv, k_dup, g_c, g_k_dup, tm=512):
    n, w = c_kv.shape
    tile = pl.BlockSpec((tm, w), lambda i: (i, 0))
    const = pl.BlockSpec((1, w), lambda i: (0, 0))
    return pl.pallas_call(
        _kv_prep_kernel, grid=(n // tm,), in_specs=[tile, tile, const, const], out_specs=[tile, tile],
        out_shape=[jax.ShapeDtypeStruct((n, w), BF16)] * 2,
        compiler_params=_cparams(("parallel",)), name="kv_prep",
    )(c_kv, k_dup, g_c.reshape(1, w), g_k_dup.reshape(1, w))


def _dsa_kernel(qi_ref, wi_ref, q_ref, kn_ref, cn_ref, wk_ref, wv_ref, o_ref,
                skey_ref, qp_ref, acc_ref, m_ref, l_ref, *, qb, kt, topk, n_heads, idx_bits):
    blk = pl.program_id(1)
    nt = ((blk + 1) * qb + kt - 1) // kt
    first = _lane_is_first_head()
    q_pos = blk * qb + lax.broadcasted_iota(I32, (qb, 1), 0)
    col0 = lax.broadcasted_iota(I32, (1, kt), 1)

    def head_lanes(z2, h):
        return jnp.where(first if h % 2 == 0 else ~first, z2, 0.0)

    qi = qi_ref[0]
    wi = wi_ref[0]
    qi_h = [(head_lanes(qi[:, (h // 2) * PAIR:(h // 2 + 1) * PAIR], h) * HEAD_DIM ** -0.5).astype(BF16)
            for h in range(n_heads)]
    wcol = [wi[:, h:h + 1] for h in range(n_heads)]

    def score_tile(j, carry):
        off = pl.multiple_of(j * kt, kt)
        kn = kn_ref[0, pl.ds(off, kt), :]
        sc = jnp.zeros((qb, kt), F32)
        for h in range(n_heads):
            sc = sc + wcol[h] * jnp.maximum(_dot_nt(qi_h[h], kn), 0.0)
        sc = sc * n_heads ** -0.5
        bits = pltpu.bitcast(sc, I32)
        key = jnp.where(bits >= 0, bits, bits ^ 0x7FFFFFFF)
        key = jnp.where(sc == 0.0, 0, key)
        skey_ref[j] = jnp.where(off + col0 <= q_pos, key, INT_MIN)
        return carry

    lax.fori_loop(0, nt, score_tile, 0)

    def count(pred):
        def body(j, acc):
            hit = jnp.where(pred(skey_ref[j], j * kt + col0), 1.0, 0.0)
            part = hit[:, :LANES]
            for c in range(1, kt // LANES):
                part = part + hit[:, c * LANES:(c + 1) * LANES]
            return acc + part
        acc = lax.fori_loop(0, nt, body, jnp.zeros((qb, LANES), F32))
        return jnp.sum(acc, axis=1, keepdims=True)

    def thr_bit(i, prefix):
        cand = prefix | jnp.left_shift(jnp.int32(1), 31 - i)
        cnt = count(lambda sk, _: sk >= (cand ^ INT_MIN))
        return jnp.where(cnt >= topk, cand, prefix)

    thr = lax.fori_loop(0, 32, thr_bit, jnp.zeros((qb, 1), I32)) ^ INT_MIN
    n_gt = count(lambda sk, _: sk > thr)
    n_eq = count(lambda sk, _: sk == thr)
    need = topk - n_gt
    tie = (n_eq > need) & (thr != INT_MIN)
    any_tie = jnp.max(jnp.where(tie, 1, 0))

    def idx_bit(i, prefix):
        cand = prefix | jnp.left_shift(jnp.int32(1), idx_bits - 1 - i)
        cnt = count(lambda sk, cidx: (sk == thr) & (cidx < cand))
        return jnp.where(cnt < need, cand, prefix)

    jmax = lax.fori_loop(0, idx_bits * any_tie, idx_bit, jnp.zeros((qb, 1), I32))
    jmax = jnp.where(tie, jmax, 0x7FFFFFFF)

    q = q_ref[0]
    for h in range(n_heads):
        pr = slice((h // 2) * PAIR, (h // 2 + 1) * PAIR)
        qh = head_lanes(q[:, pr], h).astype(BF16)
        qp_ref[h] = (_dot_nt(qh, wk_ref[:, pr]) * HEAD_DIM ** -0.5).astype(BF16)
    m_ref[...] = jnp.full_like(m_ref, NEG_BIG)
    l_ref[...] = jnp.zeros_like(l_ref)
    acc_ref[...] = jnp.zeros_like(acc_ref)

    def attn_tile(j, carry):
        off = pl.multiple_of(j * kt, kt)
        sk = skey_ref[j]
        cidx = off + col0
        sel = ((sk > thr) | ((sk == thr) & (cidx <= jmax))) & (cidx <= q_pos)
        bias = jnp.where(sel, 0.0, NEG_BIG)
        cn = cn_ref[0, pl.ds(off, kt), :]
        for h in range(n_heads):
            s = _dot_nt(qp_ref[h], cn) + bias
            m_old = m_ref[h]
            m_new = jnp.maximum(m_old, jnp.max(s, axis=1, keepdims=True))
            alpha = jnp.exp(m_old - m_new)
            pexp = jnp.where(sel, jnp.exp(s - m_new), 0.0)
            l_ref[h] = alpha * l_ref[h] + jnp.sum(pexp, axis=1, keepdims=True)
            acc_ref[h] = alpha * acc_ref[h] + _dot(pexp.astype(BF16), cn)
            m_ref[h] = m_new
        return carry

    lax.fori_loop(0, nt, attn_tile, 0)

    for p in range(n_heads // 2):
        pr = slice(p * PAIR, (p + 1) * PAIR)
        wv = wv_ref[:, pr]
        out = jnp.zeros((qb, PAIR), F32)
        for h in (2 * p, 2 * p + 1):
            lat = (acc_ref[h] / l_ref[h]).astype(BF16)
            out = out + _dot(lat, head_lanes(wv, h).astype(BF16))
        o_ref[0, :, pr] = out.astype(o_ref.dtype)


def dsa_attention(q_idx, w_idx, q_b, kn, cn, wk_t, wv, qb=128, kt=512):
    b, t, width = q_b.shape
    n_heads = width // HEAD_DIM
    kt = min(kt, t)
    topk = min(IDX_TOPK_MAX, t // 4)
    qtile = lambda c: pl.BlockSpec((1, qb, c), lambda i, j: (i, j, 0))
    full = lambda c: pl.BlockSpec((1, t, c), lambda i, j: (i, 0, 0))
    const = lambda a: pl.BlockSpec(a.shape, lambda i, j: (0, 0))
    kern = functools.partial(_dsa_kernel, qb=qb, kt=kt, topk=topk, n_heads=n_heads,
                             idx_bits=max(1, (t - 1).bit_length()))
    return pl.pallas_call(
        kern, grid=(b, t // qb),
        in_specs=[qtile(width), qtile(LANES), qtile(width), full(LANES), full(LANES), const(wk_t), const(wv)],
        out_specs=qtile(width), out_shape=jax.ShapeDtypeStruct((b, t, width), BF16),
        scratch_shapes=[pltpu.VMEM((t // kt, qb, kt), I32), pltpu.VMEM((n_heads, qb, LANES), BF16),
                        pltpu.VMEM((n_heads, qb, LANES), F32), pltpu.VMEM((n_heads, qb, 1), F32),
                        pltpu.VMEM((n_heads, qb, 1), F32)],
        compiler_params=_cparams(("parallel", "arbitrary")), name="dsa_attention",
    )(q_idx, w_idx, q_b, kn, cn, wk_t, wv)


def _even_out_kernel(x_ref, y_ref, bonus_ref, gate_ref, yb_ref, lng_ref, lnb_ref, seg_ref, wa_ref, wb_ref, o_ref):
    seg = seg_ref[...]
    y = y_ref[...]
    yc = y - _dot_hi(y, seg) * (1.0 / HEAD_DIM)
    yn = yc * lax.rsqrt(_dot_hi(yc * yc, seg) * (1.0 / HEAD_DIM) + GN_EPS)
    ya = (yn * lng_ref[...] + lnb_ref[...] + bonus_ref[...]) * gate_ref[...]
    o_ref[...] = x_ref[...] + _dot(ya.astype(BF16), wa_ref[...]) + _dot(yb_ref[...], wb_ref[...])


def even_out(x2d, y, bonus, gate, y_b, ln_g, ln_b, w_out, tm=256):
    n, d = x2d.shape
    aw = y.shape[1]
    tile = lambda c: pl.BlockSpec((tm, c), lambda i: (i, 0))
    const = lambda a: pl.BlockSpec(a.shape, lambda i: (0, 0))
    consts = [ln_g.reshape(1, aw), ln_b.reshape(1, aw), _seg_ones(aw),
              w_out[:aw].astype(BF16), w_out[aw:].astype(BF16)]
    return pl.pallas_call(
        _even_out_kernel, grid=(n // tm,),
        in_specs=[tile(d), tile(aw), tile(aw), tile(aw), tile(y_b.shape[1])] + [const(a) for a in consts],
        out_specs=tile(d), out_shape=jax.ShapeDtypeStruct((n, d), F32),
        compiler_params=_cparams(("parallel",)), name="even_out",
    )(x2d, y, bonus, gate, y_b, *consts)


def _stickbreak_kernel(q_ref, k_ref, v_ref, o_ref, acc_ref, carry_ref, *, tq):
    qi = pl.program_id(2)
    first = _lane_is_first_head()
    row = lax.broadcasted_iota(I32, (tq, tq), 0)
    col = lax.broadcasted_iota(I32, (tq, tq), 1)
    before = col < row
    later = (row > col).astype(BF16)
    q2 = q_ref[0]
    acc_ref[...] = jnp.zeros_like(acc_ref)

    for m in (first, ~first):
        qh = jnp.where(m, q2, 0.0) * HEAD_DIM ** -0.5

        def tile(j, diag):
            off = pl.multiple_of(j * tq, tq)
            z = _dot_nt(qh, k_ref[0, pl.ds(off, tq), :])
            lk = -(jnp.maximum(z, 0.0) + jnp.log1p(jnp.exp(-jnp.abs(z))))
            if diag:
                lk = jnp.where(before, lk, 0.0)
            lk_hi = lk.astype(BF16)
            lk_lo = (lk - lk_hi.astype(F32)).astype(BF16)
            tail = _dot(lk_hi, later) + _dot(lk_lo, later) + carry_ref[...]
            w = jnp.exp(z + lk + tail)
            if diag:
                w = jnp.where(before, w, 0.0)
            vh = jnp.where(m, v_ref[0, pl.ds(off, tq), :], 0.0)
            acc_ref[...] += _dot(w.astype(BF16), vh)
            carry_ref[...] += jnp.sum(lk, axis=1, keepdims=True)
            return (jnp.max(carry_ref[...]) > -110.0).astype(I32)

        carry_ref[...] = jnp.zeros_like(carry_ref)
        live = tile(qi, True)

        def cond(st):
            return (st[0] >= 0) & (st[1] > 0)

        def body(st):
            return st[0] - 1, tile(st[0], False)

        lax.while_loop(cond, body, (qi - 1, live))

    o_ref[0] = acc_ref[...].astype(o_ref.dtype)


def stickbreak_attention(qkv, n_heads, tq=256):
    b, t, _ = qkv.shape
    n_pair = n_heads // 2
    tq = min(tq, t)
    q_spec = pl.BlockSpec((1, tq, PAIR), lambda i, p, j: (i, j, p))
    k_spec = pl.BlockSpec((1, t, PAIR), lambda i, p, j: (i, 0, n_pair + p))
    v_spec = pl.BlockSpec((1, t, PAIR), lambda i, p, j: (i, 0, 2 * n_pair + p))
    return pl.pallas_call(
        functools.partial(_stickbreak_kernel, tq=tq), grid=(b, n_pair, t // tq),
        in_specs=[q_spec, k_spec, v_spec], out_specs=q_spec,
        out_shape=jax.ShapeDtypeStruct((b, t, n_heads * HEAD_DIM), BF16),
        scratch_shapes=[pltpu.VMEM((tq, PAIR), F32), pltpu.VMEM((tq, 1), F32)],
        compiler_params=_cparams(("parallel", "parallel", "arbitrary")), name="stickbreak",
    )(qkv, qkv, qkv)


def _proj_residual_kernel(x_ref, y_ref, w_ref, o_ref):
    o_ref[...] = x_ref[...] + _dot(y_ref[...], w_ref[...])


def proj_residual(x2d, y, w, tm=256):
    n, d = x2d.shape
    c = y.shape[1]
    return pl.pallas_call(
        _proj_residual_kernel, grid=(n // tm,),
        in_specs=[pl.BlockSpec((tm, d), lambda i: (i, 0)), pl.BlockSpec((tm, c), lambda i: (i, 0)),
                  pl.BlockSpec(w.shape, lambda i: (0, 0))],
        out_specs=pl.BlockSpec((tm, d), lambda i: (i, 0)), out_shape=jax.ShapeDtypeStruct((n, d), F32),
        compiler_params=_cparams(("parallel",)), name="proj_residual",
    )(x2d, y, w)


def _router_kernel(x_ref, g_ref, wr_ref, h_ref, ids_ref, gates_ref):
    h = _rms(x_ref[...], g_ref[...])
    h_ref[...] = h.astype(h_ref.dtype)
    logits = _dot_hi(h, wr_ref[...])
    lane = lax.broadcasted_iota(I32, logits.shape, 1)
    lane_f = lane.astype(F32)

    def top1(vals):
        mx = jnp.max(vals, axis=1, keepdims=True)
        return mx, jnp.min(jnp.where(vals == mx, lane_f, 1e9), axis=1, keepdims=True).astype(I32)

    is_g = lane < N_GROUPS
    g_max, g_sel = top1(jnp.where(is_g, logits, -jnp.inf))
    g_gate = 1.0 / jnp.sum(jnp.where(is_g, jnp.exp(logits - g_max), 0.0), axis=1, keepdims=True)
    e_lane = lane - N_GROUPS
    in_grp = (e_lane >= g_sel * EXPERTS_PER_GROUP) & (e_lane < (g_sel + 1) * EXPERTS_PER_GROUP)
    el = jnp.where(in_grp, logits, -jnp.inf)
    m1, i1 = top1(el)
    m2, i2 = top1(jnp.where(lane == i1, -jnp.inf, el))
    e21 = jnp.exp(m2 - m1)
    p1 = 1.0 / (1.0 + e21)
    ids_ref[...] = jnp.where(lane == 0, i1 - N_GROUPS, jnp.where(lane == 1, i2 - N_GROUPS, 0))
    gates_ref[...] = jnp.where(lane == 0, g_gate * p1, jnp.where(lane == 1, g_gate * (e21 * p1), 0.0))


def moe_router(x2d, g, w_router, tm=256):
    n, d = x2d.shape
    tile = lambda c: pl.BlockSpec((tm, c), lambda i: (i, 0))
    return pl.pallas_call(
        _router_kernel, grid=(n // tm,),
        in_specs=[tile(d), pl.BlockSpec((1, d), lambda i: (0, 0)), pl.BlockSpec(w_router.shape, lambda i: (0, 0))],
        out_specs=[tile(d), tile(LANES), tile(LANES)],
        out_shape=[jax.ShapeDtypeStruct((n, d), BF16), jax.ShapeDtypeStruct((n, LANES), I32),
                   jax.ShapeDtypeStruct((n, LANES), F32)],
        compiler_params=_cparams(("parallel",)), name="moe_router",
    )(x2d, g.reshape(1, d), w_router)


MOE_TT = 2048
MOE_CH = 128
MOE_KC = 512
_PLAN_BLK = 256


def _moe_plan_kernel(ids_ref, idst_ref, rank_ref, dest_ref, meta_ref, *, tt):
    nb = tt // _PLAN_BLK
    row = lax.broadcasted_iota(I32, (_PLAN_BLK, _PLAN_BLK), 0)
    col = lax.broadcasted_iota(I32, (_PLAN_BLK, _PLAN_BLK), 1)
    earlier_t = (row < col).astype(BF16)
    earlier = (col < row).astype(BF16)
    e_sub = lax.broadcasted_iota(I32, (N_EXPERTS, _PLAN_BLK), 0)
    carry_t = jnp.zeros((N_EXPERTS, 1), F32)
    for b in range(nb):
        sl = slice(b * _PLAN_BLK, (b + 1) * _PLAN_BLK)
        member = (e_sub == idst_ref[0:1, sl]) | (e_sub == idst_ref[1:2, sl])
        m_t = jnp.where(member, 1.0, 0.0)
        before = _dot(m_t.astype(BF16), earlier_t) + carry_t
        rank_ref[0, :, sl] = jnp.where(member, before, -1.0)
        carry_t = carry_t + jnp.sum(m_t, axis=1, keepdims=True)
    lane = lax.broadcasted_iota(I32, (_PLAN_BLK, LANES), 1)
    carry = jnp.zeros((1, LANES), F32)
    prefix = []
    for b in range(nb):
        sl = slice(b * _PLAN_BLK, (b + 1) * _PLAN_BLK)
        m = jnp.where((lane == ids_ref[sl, 0:1]) | (lane == ids_ref[sl, 1:2]), 1.0, 0.0)
        prefix.append(_dot(earlier, m.astype(BF16)) + carry)
        carry = carry + jnp.sum(m, axis=0, keepdims=True)
    counts = carry
    padded = jnp.ceil(counts * (1.0 / MOE_CH)) * MOE_CH
    l_row = lax.broadcasted_iota(I32, (LANES, LANES), 0)
    l_col = lax.broadcasted_iota(I32, (LANES, LANES), 1)
    offs = _dot_hi(padded, (l_row < l_col).astype(F32))
    for b in range(nb):
        sl = slice(b * _PLAN_BLK, (b + 1) * _PLAN_BLK)
        where_row = prefix[b] + offs
        d0 = jnp.sum(jnp.where(lane == ids_ref[sl, 0:1], where_row, 0.0), axis=1, keepdims=True)
        d1 = jnp.sum(jnp.where(lane == ids_ref[sl, 1:2], where_row, 0.0), axis=1, keepdims=True)
        dest_ref[sl, :] = jnp.where(lane == 0, d0, jnp.where(lane == 1, d1, 0.0)).astype(I32)
    r_used = jnp.sum(padded, axis=1, keepdims=True)
    sub8 = lax.broadcasted_iota(I32, (8, LANES), 0)
    meta_ref[0] = jnp.where(sub8 == 0, counts, jnp.where(sub8 == 1, offs, jnp.where(sub8 == 2, r_used, 0.0))).astype(I32)


def moe_plan(ids, ids_t, tt):
    n = ids.shape[0]
    n_tiles = n // tt
    return pl.pallas_call(
        functools.partial(_moe_plan_kernel, tt=tt), grid=(n_tiles,),
        in_specs=[pl.BlockSpec((tt, LANES), lambda i: (i, 0)), pl.BlockSpec((8, tt), lambda i: (0, i))],
        out_specs=[pl.BlockSpec((1, N_EXPERTS, tt), lambda i: (i, 0, 0)), pl.BlockSpec((tt, LANES), lambda i: (i, 0)),
                   pl.BlockSpec((1, 8, LANES), lambda i: (i, 0, 0))],
        out_shape=[jax.ShapeDtypeStruct((n_tiles, N_EXPERTS, tt), F32), jax.ShapeDtypeStruct((n, LANES), I32),
                   jax.ShapeDtypeStruct((n_tiles, 8, LANES), I32)],
        compiler_params=_cparams(("parallel",)), name="moe_plan",
    )(ids, ids_t)


def _moe_ffn_kernel(cnt_ref, off_ref, h_ref, rank_ref, idst_ref, gt_ref, w1_ref, w3_ref, w2_ref, y_ref, *, tt):
    t, e = pl.program_id(0), pl.program_id(1)

    @pl.when(e == 0)
    def _():
        y_ref[...] = jnp.zeros_like(y_ref)

    cnt = cnt_ref[t * N_EXPERTS + e]
    off = off_ref[t * N_EXPERTS + e]
    rank_row = rank_ref[0, pl.ds(e, 1), :]
    gate_row = jnp.where(idst_ref[0:1, :] == e, gt_ref[0:1, :], gt_ref[1:2, :])
    sub = lax.broadcasted_iota(I32, (MOE_CH, 1), 0)

    def chunk(c, carry):
        r0 = c * MOE_CH
        pick = rank_row == (r0 + sub).astype(F32)
        x = _dot(jnp.where(pick, 1.0, 0.0).astype(BF16), h_ref[...]).astype(BF16)
        a = _dot(x, w1_ref[0])
        hdn = (a * jax.nn.sigmoid(a)) * _dot(x, w3_ref[0])
        y = _dot(hdn.astype(BF16), w2_ref[0])
        gate = jnp.sum(jnp.where(pick, gate_row, 0.0), axis=1, keepdims=True)
        y_ref[0, pl.ds(pl.multiple_of(off + r0, MOE_CH), MOE_CH), :] = (y * gate).astype(y_ref.dtype)
        return carry

    lax.fori_loop(0, (cnt + MOE_CH - 1) // MOE_CH, chunk, 0)


def moe_ffn(h, rank_t, ids_t, gates_t, counts, offs, w1, w3, w2, tt):
    n, d = h.shape
    ff = w1.shape[-1]
    n_tiles = n // tt
    r_max = TOP_K_EXPERTS * tt + N_EXPERTS * MOE_CH
    grid_spec = pltpu.PrefetchScalarGridSpec(
        num_scalar_prefetch=2, grid=(n_tiles, N_EXPERTS),
        in_specs=[pl.BlockSpec((tt, d), lambda i, e, c, o: (i, 0)),
                  pl.BlockSpec((1, N_EXPERTS, tt), lambda i, e, c, o: (i, 0, 0)),
                  pl.BlockSpec((8, tt), lambda i, e, c, o: (0, i)),
                  pl.BlockSpec((8, tt), lambda i, e, c, o: (0, i)),
                  pl.BlockSpec((1, d, ff), lambda i, e, c, o: (e, 0, 0)),
                  pl.BlockSpec((1, d, ff), lambda i, e, c, o: (e, 0, 0)),
                  pl.BlockSpec((1, ff, d), lambda i, e, c, o: (e, 0, 0))],
        out_specs=pl.BlockSpec((1, r_max, d), lambda i, e, c, o: (i, 0, 0)))
    return pl.pallas_call(
        functools.partial(_moe_ffn_kernel, tt=tt), grid_spec=grid_spec,
        out_shape=jax.ShapeDtypeStruct((n_tiles, r_max, d), BF16),
        compiler_params=_cparams(("arbitrary", "arbitrary")), name="moe_ffn",
    )(counts, offs, h, rank_t, ids_t, gates_t, w1, w3, w2)


def _moe_combine_kernel(used_ref, x_ref, dest_ref, y_ref, o_ref):
    t = pl.program_id(0)
    d0, d1 = dest_ref[:, 0:1], dest_ref[:, 1:2]
    col = lax.broadcasted_iota(I32, (1, MOE_KC), 1)
    o_ref[...] = x_ref[...]

    def step(k, carry):
        r0 = pl.multiple_of(k * MOE_KC, MOE_KC)
        rows = r0 + col
        pick = jnp.where((rows == d0) | (rows == d1), 1.0, 0.0).astype(BF16)
        o_ref[...] += _dot(pick, y_ref[0, pl.ds(r0, MOE_KC), :])
        return carry

    lax.fori_loop(0, (used_ref[t] + MOE_KC - 1) // MOE_KC, step, 0)


def moe_combine(x2d, dest, y_all, r_used, tt, tm=256):
    n, d = x2d.shape
    r_max = y_all.shape[1]
    per = tt // tm
    grid_spec = pltpu.PrefetchScalarGridSpec(
        num_scalar_prefetch=1, grid=(n // tt, per),
        in_specs=[pl.BlockSpec((tm, d), lambda i, j, u: (i * per + j, 0)),
                  pl.BlockSpec((tm, LANES), lambda i, j, u: (i * per + j, 0)),
                  pl.BlockSpec((1, r_max, d), lambda i, j, u: (i, 0, 0))],
        out_specs=pl.BlockSpec((tm, d), lambda i, j, u: (i * per + j, 0)))
    return pl.pallas_call(
        _moe_combine_kernel, grid_spec=grid_spec, out_shape=jax.ShapeDtypeStruct((n, d), F32),
        compiler_params=_cparams(("arbitrary", "arbitrary")), name="moe_combine",
    )(r_used, x2d, dest, y_all)


def _final_norm_kernel(x_ref, g_ref, o_ref):
    o_ref[...] = _rms(x_ref[...], g_ref[...])


def final_norm(x2d, g, tm=512):
    n, d = x2d.shape
    return pl.pallas_call(
        _final_norm_kernel, grid=(n // tm,),
        in_specs=[pl.BlockSpec((tm, d), lambda i: (i, 0)), pl.BlockSpec((1, d), lambda i: (0, 0))],
        out_specs=pl.BlockSpec((tm, d), lambda i: (i, 0)), out_shape=jax.ShapeDtypeStruct((n, d), F32),
        compiler_params=_cparams(("parallel",)), name="final_norm",
    )(x2d, g.reshape(1, d))


def moe_layer(x2d, norm_g, router_group, router_expert, w1, w3, w2):
    n, d = x2d.shape
    tt = min(MOE_TT, n)
    w_router = jnp.concatenate([router_group, jnp.moveaxis(router_expert, 0, 1).reshape(d, N_EXPERTS)], axis=1)
    h, ids, gates = moe_router(x2d, norm_g, _pad_cols(w_router, LANES))
    ids_t = jnp.pad(ids[:, :TOP_K_EXPERTS].T, ((0, 8 - TOP_K_EXPERTS), (0, 0)))
    gates_t = jnp.pad(gates[:, :TOP_K_EXPERTS].T, ((0, 8 - TOP_K_EXPERTS), (0, 0)))
    rank_t, dest, meta = moe_plan(ids, ids_t, tt)
    counts = meta[:, 0, :N_EXPERTS].reshape(-1)
    offs = meta[:, 1, :N_EXPERTS].reshape(-1)
    y_all = moe_ffn(h, rank_t, ids_t, gates_t, counts, offs, w1.astype(BF16), w3.astype(BF16), w2.astype(BF16), tt)
    return moe_combine(x2d, dest, y_all, meta[:, 2, 0], tt)


def even_layer(x, v_first, p):
    b, t, d = x.shape
    n = b * t
    w_in = p['w_in']
    aw = p['k_k'].shape[0]
    bw = p['w_kv_up'].shape[1] // 2
    rank = p['w_kv_up'].shape[0]
    n_idx = w_in.shape[1] - (3 * aw + bw + rank + bw + HEAD_DIM)
    o = 0
    segs = []
    for width in (3 * aw, bw, rank, bw, HEAD_DIM, n_idx):
        segs.append(w_in[:, o:o + width])
        o += width
    segs[4] = jnp.concatenate([segs[4], segs[4]], axis=1)
    segs[5] = _pad_cols(segs[5], LANES)
    rkv, q_b, c_kv, q_idx, k_dup, w_idx = norm_proj(
        x.reshape(n, d), p['norm'], [s.astype(BF16) for s in segs], [F32] * 6)

    r, lw, k_mod, v, kk, iclr, gate, bonus = rwkv_prep(x, rkv.reshape(b, t, 3 * aw), v_first, p)
    y = rwkv_scan(r, lw, k_mod, v, kk, iclr)

    cn, kn = kv_prep(c_kv, k_dup, p['kv_norm'], jnp.concatenate([p['idx_k_norm']] * 2))
    wk_t = p['w_kv_up'][:, :bw].astype(BF16)
    wv = p['w_kv_up'][:, bw:].astype(BF16)
    y_b = dsa_attention(q_idx.reshape(b, t, bw), w_idx.reshape(b, t, LANES), q_b.reshape(b, t, bw),
                        kn.reshape(b, t, LANES), cn.reshape(b, t, rank), wk_t, wv)

    x_new = even_out(x.reshape(n, d), y.reshape(n, aw), bonus.reshape(n, aw), gate.reshape(n, aw),
                     y_b.reshape(n, bw), p['ln_g'], p['ln_b'], p['w_out'])
    return x_new.reshape(b, t, d), v


def odd_layer(x, norm_g, w_qkv, w_out):
    b, t, d = x.shape
    n = b * t
    cw = w_qkv.shape[1] // 3
    (qkv,) = norm_proj(x.reshape(n, d), norm_g, [w_qkv.astype(BF16)], [BF16])
    y_c = stickbreak_attention(qkv.reshape(b, t, 3 * cw), cw // HEAD_DIM)
    return proj_residual(x.reshape(n, d), y_c.reshape(n, cw), w_out.astype(BF16)).reshape(b, t, d)


@jax.jit
def _forward(x, ev_norm, ev_w_in, ev_w_out, a_mu_rkv, a_mu_lora, a_w0, a_w1, a_w2, a_a0, a_a1, a_a2, a_g1,
             a_g2, a_k_k, a_k_a, a_r_k, a_ln_g, a_ln_b, a_mu_vres, a_v0, a_v1, a_v2, b_kv_norm, b_w_kv_up,
             b_idx_k_norm, od_norm, od_w_qkv, od_w_out, moe_norm, moe_router_group, moe_router_expert,
             moe_w1, moe_w3, moe_w2, final_norm_g):
    b, t, d = x.shape
    depth = moe_norm.shape[0]
    v_first = None
    for i in range(depth):
        if i % 2 == 0:
            e = i // 2
            p = dict(norm=ev_norm[e], w_in=ev_w_in[e], w_out=ev_w_out[e], mu_rkv=a_mu_rkv[e], mu_lora=a_mu_lora[e],
                     w0=a_w0[e], w1=a_w1[e], w2=a_w2[e], a0=a_a0[e], a1=a_a1[e], a2=a_a2[e], g1=a_g1[e], g2=a_g2[e],
                     k_k=a_k_k[e], k_a=a_k_a[e], r_k=a_r_k[e].reshape(-1), ln_g=a_ln_g[e], ln_b=a_ln_b[e],
                     kv_norm=b_kv_norm[e], w_kv_up=b_w_kv_up[e], idx_k_norm=b_idx_k_norm[e])
            if e > 0:
                p.update(mu_v=a_mu_vres[e - 1], v0=a_v0[e - 1], v1=a_v1[e - 1], v2=a_v2[e - 1])
            x, v_used = even_layer(x, v_first if e > 0 else None, p)
            if e == 0:
                v_first = v_used
        else:
            o = i // 2
            x = odd_layer(x, od_norm[o], od_w_qkv[o], od_w_out[o])
        x = moe_layer(x.reshape(b * t, d), moe_norm[i], moe_router_group[i], moe_router_expert[i],
                      moe_w1[i], moe_w3[i], moe_w2[i]).reshape(b, t, d)
    return final_norm(x.reshape(b * t, d), final_norm_g).reshape(b, t, d)


def kernel(x, ev_norm, ev_w_in, ev_w_out, a_mu_rkv, a_mu_lora, a_w0, a_w1, a_w2, a_a0, a_a1, a_a2, a_g1, a_g2, a_k_k, a_k_a, a_r_k, a_ln_g, a_ln_b, a_mu_vres, a_v0, a_v1, a_v2, b_kv_norm, b_w_kv_up, b_idx_k_norm, od_norm, od_w_qkv, od_w_out, moe_norm, moe_router_group, moe_router_expert, moe_w1, moe_w3, moe_w2, final_norm):
    return _forward(x, ev_norm, ev_w_in, ev_w_out, a_mu_rkv, a_mu_lora, a_w0, a_w1, a_w2, a_a0, a_a1, a_a2, a_g1,
                    a_g2, a_k_k, a_k_a, a_r_k, a_ln_g, a_ln_b, a_mu_vres, a_v0, a_v1, a_v2, b_kv_norm, b_w_kv_up,
                    b_idx_k_norm, od_norm, od_w_qkv, od_w_out, moe_norm, moe_router_group, moe_router_expert,
                    moe_w1, moe_w3, moe_w2, final_norm)
```

```python
import functools
import math

import jax
import jax.numpy as jnp
from jax import lax
from jax.experimental import pallas as pl
from jax.experimental.pallas import tpu as pltpu

F32 = jnp.float32
BF16 = jnp.bfloat16
I32 = jnp.int32
HI = lax.Precision.HIGHEST

LANES = 128
HEAD_DIM = 64
PAIR = 2 * HEAD_DIM
RMS_EPS = 1e-6
GN_EPS = 64e-5
N_GROUPS = 4
EXPERTS_PER_GROUP = 8
N_EXPERTS = N_GROUPS * EXPERTS_PER_GROUP
TOP_K_EXPERTS = 2
IDX_TOPK_MAX = 256
INT_MIN = -2147483648
NEG_BIG = -1e30
VMEM_LIMIT = 56 * 1024 * 1024

_NT = (((1,), (1,)), ((), ()))
_TN = (((0,), (0,)), ((), ()))


def _cparams(sem):
    return pltpu.CompilerParams(dimension_semantics=sem, vmem_limit_bytes=VMEM_LIMIT)


def _rms(x, g):
    return x * lax.rsqrt(jnp.mean(x * x, axis=-1, keepdims=True) + RMS_EPS) * g


def _dot(a, b):
    return jnp.dot(a, b, preferred_element_type=F32)


def _dot_hi(a, b):
    return jnp.dot(a, b, preferred_element_type=F32, precision=HI)


def _dot_nt(a, b):
    return lax.dot_general(a, b, _NT, preferred_element_type=F32)


def _lane_is_first_head(width=PAIR):
    return lax.broadcasted_iota(I32, (1, width), 1) % PAIR < HEAD_DIM


def _norm_proj_kernel(x_ref, g_ref, *refs, n_seg):
    w_refs, o_refs = refs[:n_seg], refs[n_seg:]
    h = _rms(x_ref[...], g_ref[...]).astype(BF16)
    for w_ref, o_ref in zip(w_refs, o_refs):
        o_ref[...] = _dot(h, w_ref[...]).astype(o_ref.dtype)


def norm_proj(x2d, g, ws, out_dtypes, tm=256):
    n, d = x2d.shape
    in_specs = [pl.BlockSpec((tm, d), lambda i: (i, 0)), pl.BlockSpec((1, d), lambda i: (0, 0))]
    in_specs += [pl.BlockSpec(w.shape, lambda i: (0, 0)) for w in ws]
    out_specs = [pl.BlockSpec((tm, w.shape[1]), lambda i: (i, 0)) for w in ws]
    out_shape = [jax.ShapeDtypeStruct((n, w.shape[1]), dt) for w, dt in zip(ws, out_dtypes)]
    return pl.pallas_call(
        functools.partial(_norm_proj_kernel, n_seg=len(ws)),
        grid=(n // tm,), in_specs=in_specs, out_specs=out_specs, out_shape=out_shape,
        compiler_params=_cparams(("parallel",)), name="norm_proj",
    )(x2d, g.reshape(1, d), *ws)


def _shift_rows(z, carry_ref):
    tm = z.shape[0]
    first = lax.broadcasted_iota(I32, (tm, 1), 0) == 0
    prev = jnp.where(first, carry_ref[...], pltpu.roll(z, 1, 0))
    carry_ref[...] = z[tm - 1:tm, :]
    return prev


def _rwkv_prep_kernel(*refs, has_vres):
    it = iter(refs)
    x_ref, rkv_ref = next(it), next(it)
    vfirst_ref = next(it) if has_vres else None
    g_ref, mu_rkv_ref, mu_lora_ref = next(it), next(it), next(it)
    w0_ref, w1_ref, w2_ref = next(it), next(it), next(it)
    a0_ref, a1_ref, a2_ref = next(it), next(it), next(it)
    g1_ref, g2_ref = next(it), next(it)
    if has_vres:
        muv_ref, v0_ref, v1_ref, v2_ref = next(it), next(it), next(it), next(it)
    kk_w_ref, ka_ref, rk_ref, seg_ref = next(it), next(it), next(it), next(it)
    r_o, lw_o, k_o, v_o, kk_o, a_o, gate_o, bonus_o = (next(it) for _ in range(8))
    hcarry, rkvcarry = next(it), next(it)

    @pl.when(pl.program_id(1) == 0)
    def _():
        hcarry[...] = jnp.zeros_like(hcarry)
        rkvcarry[...] = jnp.zeros_like(rkvcarry)

    aw = r_o.shape[-1]
    h = _rms(x_ref[0], g_ref[...])
    dh = _shift_rows(h, hcarry) - h
    rkv = rkv_ref[0]
    rkv = rkv + (_shift_rows(rkv, rkvcarry) - rkv) * mu_rkv_ref[...]
    r, k, v = rkv[:, :aw], rkv[:, aw:2 * aw], rkv[:, 2 * aw:]

    def lora_in(row):
        return (h + dh * mu_lora_ref[row:row + 1, :]).astype(BF16)

    dec = w0_ref[...] + _dot(jnp.tanh(_dot(lora_in(0), w1_ref[...])).astype(BF16), w2_ref[...])
    logw = -math.exp(-0.5) * jax.nn.sigmoid(dec)
    iclr = jax.nn.sigmoid(a0_ref[...] + _dot(_dot(lora_in(1), a1_ref[...]).astype(BF16), a2_ref[...]))
    gate = _dot(jax.nn.sigmoid(_dot(lora_in(2), g1_ref[...])).astype(BF16), g2_ref[...])
    if has_vres:
        xv = (h + dh * muv_ref[...]).astype(BF16)
        mix = jax.nn.sigmoid(v0_ref[...] + _dot(_dot(xv, v1_ref[...]).astype(BF16), v2_ref[...]))
        v = v + (vfirst_ref[0] - v) * mix

    seg = seg_ref[...]
    kk = k * kk_w_ref[...]
    kk = kk * lax.rsqrt(jnp.maximum(_dot_hi(kk * kk, seg), 1e-12))
    k_mod = k * (1.0 + (iclr - 1.0) * ka_ref[...])
    bonus = _dot_hi(r * k_mod * rk_ref[...], seg) * v

    r_o[0], lw_o[0], k_o[0], v_o[0] = r, logw, k_mod, v
    kk_o[0], a_o[0], gate_o[0], bonus_o[0] = kk, iclr, gate, bonus


def _pad_cols(w, n):
    return jnp.pad(w, ((0, 0), (0, n - w.shape[1])))


def _pad_rows(w, n):
    return jnp.pad(w, ((0, n - w.shape[0]), (0, 0)))


def _seg_ones(width):
    i = jnp.arange(width) // HEAD_DIM
    return (i[:, None] == i[None, :]).astype(F32)


def rwkv_prep(x, rkv, v_first, p, tm=256):
    b, t, d = x.shape
    aw = rkv.shape[-1] // 3
    has_vres = v_first is not None
    row = lambda a: a.reshape(1, -1)
    const = lambda a: pl.BlockSpec(a.shape, lambda i, j: (0,) * a.ndim)
    tile = lambda c: pl.BlockSpec((1, tm, c), lambda i, j: (i, j, 0))

    args, specs = [x, rkv], [tile(d), tile(3 * aw)]
    if has_vres:
        args.append(v_first)
        specs.append(tile(aw))
    consts = [row(p['norm']), p['mu_rkv'].reshape(1, 3 * aw), p['mu_lora'],
              row(p['w0']), _pad_cols(p['w1'], LANES).astype(BF16), _pad_rows(p['w2'], LANES).astype(BF16),
              row(p['a0']), _pad_cols(p['a1'], LANES).astype(BF16), _pad_rows(p['a2'], LANES).astype(BF16),
              p['g1'].astype(BF16), p['g2'].astype(BF16)]
    if has_vres:
        consts += [row(p['mu_v']), row(p['v0']), _pad_cols(p['v1'], LANES).astype(BF16),
                   _pad_rows(p['v2'], LANES).astype(BF16)]
    consts += [row(p['k_k']), row(p['k_a']), row(p['r_k']), _seg_ones(aw)]
    args += consts
    specs += [const(a) for a in consts]
    out_shape = [jax.ShapeDtypeStruct((b, t, aw), F32)] * 8
    return pl.pallas_call(
        functools.partial(_rwkv_prep_kernel, has_vres=has_vres),
        grid=(b, t // tm), in_specs=specs, out_specs=[tile(aw)] * 8, out_shape=out_shape,
        scratch_shapes=[pltpu.VMEM((1, d), F32), pltpu.VMEM((1, 3 * aw), F32)],
        compiler_params=_cparams(("arbitrary", "arbitrary")), name="rwkv_prep",
    )(*args)


def _split_bf16(x):
    hi = x.astype(BF16)
    return hi, (x - hi.astype(F32)).astype(BF16)


def _mm3(a, b, dims=None):
    a_hi, a_lo = _split_bf16(a)
    b_hi, b_lo = _split_bf16(b)
    if dims is None:
        f = lambda p, q: jnp.dot(p, q, preferred_element_type=F32)
    else:
        f = lambda p, q: lax.dot_general(p, q, dims, preferred_element_type=F32)
    return f(a_hi, b_hi) + f(a_hi, b_lo) + f(a_lo, b_hi)


def _mmc(a, b, dims=None):
    a, b = a.astype(BF16), b.astype(BF16)
    if dims is None:
        return jnp.dot(a, b, preferred_element_type=F32)
    return lax.dot_general(a, b, dims, preferred_element_type=F32)


def _cumsum_rows(x):
    n = x.shape[0]
    row = lax.broadcasted_iota(I32, (n, 1), 0)
    s = 1
    while s < n:
        x = x + jnp.where(row >= s, pltpu.roll(x, s, 0), 0.0)
        s *= 2
    return x


def _rwkv_chunk_kernel(r_ref, lw_ref, k_ref, v_ref, kk_ref, a_ref, qh_ref, y0_ref, m_ref, c0_ref, *, chunk, cps):
    L = chunk
    n_pair = r_ref.shape[-1] // PAIR
    row = lax.broadcasted_iota(I32, (L, L), 0)
    col = lax.broadcasted_iota(I32, (L, L), 1)
    strict = col < row
    incl = col <= row
    eye = (col == row).astype(F32)
    first = _lane_is_first_head()
    p_row = lax.broadcasted_iota(I32, (PAIR, PAIR), 0)
    p_col = lax.broadcasted_iota(I32, (PAIR, PAIR), 1)
    blockdiag = p_row // HEAD_DIM == p_col // HEAD_DIM
    pair_eye = p_row == p_col

    units = [(ci, p) for ci in range(cps) for p in range(n_pair)]
    heads = (first, ~first)
    pre = {}
    for u in units:
        ci, p = u
        rows = slice(ci * L, (ci + 1) * L)
        sl = slice(p * PAIR, (p + 1) * PAIR)
        lw = lw_ref[0, rows, sl]
        c = _cumsum_rows(lw)
        g = jnp.exp(c)
        g_inv = jnp.exp(-c)
        kk = kk_ref[0, rows, sl]
        pre[u] = dict(rows=rows, sl=sl, g_last=g[L - 1:L, :],
                      at=-kk * jnp.exp(c - lw),
                      bt=kk * a_ref[0, rows, sl] * g_inv,
                      kt=k_ref[0, rows, sl] * g_inv,
                      rt=r_ref[0, rows, sl] * g,
                      vv=v_ref[0, rows, sl])
    uh = [(u, h) for u in units for h in range(2)]

    gm = {}
    for u, h in uh:
        d = pre[u]
        lhs = jnp.concatenate([jnp.where(heads[h], d['at'], 0.0), jnp.where(heads[h], d['rt'], 0.0)], axis=0)
        gm[u, h] = _mm3(lhs, jnp.concatenate([d['bt'], d['kt']], axis=0), _NT)
    a_ak = {x: jnp.where(strict, gm[x][:L, L:], 0.0) for x in uh}
    a_rb = {x: jnp.where(incl, gm[x][L:, :L], 0.0) for x in uh}
    a_rk = {x: jnp.where(incl, gm[x][L:, L:], 0.0) for x in uh}
    apow = {x: jnp.where(strict, gm[x][:L, :L], 0.0) for x in uh}
    inv = {x: eye + apow[x] for x in uh}
    for _ in range(max(1, (L - 1).bit_length()) - 1):
        apow = {x: _mmc(apow[x], apow[x]) for x in uh}
        inv = {x: inv[x] + _mmc(inv[x], apow[x]) for x in uh}
    w_h = {(u, h): _mmc(inv[u, h], pre[u]['at']) for u, h in uh}
    akv = {(u, h): _mmc(a_ak[u, h], pre[u]['vv']) for u, h in uh}
    u0_h = {x: _mmc(inv[x], akv[x]) for x in uh}
    w = {u: jnp.where(first, w_h[u, 0], w_h[u, 1]) for u in units}
    u0 = {u: jnp.where(first, u0_h[u, 0], u0_h[u, 1]) for u in units}
    wu = {u: jnp.concatenate([w[u], u0[u]], axis=1) for u in units}
    arb = {(u, h): _mmc(a_rb[u, h], wu[u]) for u, h in uh}
    ark = {(u, h): _mmc(a_rk[u, h], pre[u]['vv']) for u, h in uh}
    first2 = jnp.concatenate([first, first], axis=1)
    for u in units:
        d = pre[u]
        ci, p = u
        arb_w = jnp.where(first2, arb[u, 0], arb[u, 1])
        qh_ref[0, d['rows'], d['sl']] = d['rt'] + arb_w[:, :PAIR]
        y0_ref[0, d['rows'], d['sl']] = arb_w[:, PAIR:] + jnp.where(first, ark[u, 0], ark[u, 1])
        bg = d['bt'] * d['g_last']
        m_kk = _mmc(bg, w[u], _TN)
        m_ref[0, ci, p] = jnp.where(pair_eye, d['g_last'], 0.0) + jnp.where(blockdiag, m_kk, 0.0)
        c0 = _mmc(jnp.concatenate([u0[u], d['vv']], axis=0),
                  jnp.concatenate([bg, d['kt'] * d['g_last']], axis=0), _TN)
        c0_ref[0, ci, p] = jnp.where(blockdiag, c0, 0.0)


def _rwkv_state_kernel(qh_ref, y0_ref, m_ref, c0_ref, y_ref, s_ref, *, chunk, cps):
    L = chunk

    @pl.when(pl.program_id(0) == 0)
    def _():
        s_ref[...] = jnp.zeros_like(s_ref)

    n_batch = qh_ref.shape[0]
    n_pair = qh_ref.shape[-1] // PAIR
    streams = [(b, p) for b in range(n_batch) for p in range(n_pair)]
    s2 = {x: s_ref[x[0], x[1]] for x in streams}
    for ci in range(cps):
        rows = slice(ci * L, (ci + 1) * L)
        for b, p in streams:
            sl = slice(p * PAIR, (p + 1) * PAIR)
            y_ref[b, rows, sl] = _mm3(qh_ref[b, rows, sl], s2[b, p], _NT) + y0_ref[b, rows, sl]
            s2[b, p] = _mm3(s2[b, p], m_ref[b, ci, p], _NT) + c0_ref[b, ci, p]
    for b, p in streams:
        s_ref[b, p] = s2[b, p]


def rwkv_scan(r, lw, k, v, kk, a, chunk=64, cps_a=2, cps_b=4):
    b, t, aw = r.shape
    n_pair = aw // PAIR
    n_chunk = t // chunk
    cps_a, cps_b = min(cps_a, n_chunk), min(cps_b, n_chunk)
    rows = lambda c: pl.BlockSpec((1, c * chunk, aw), lambda i, j: (i, j, 0))
    mats = lambda c: pl.BlockSpec((1, c, n_pair, PAIR, PAIR), lambda i, j: (i, j, 0, 0, 0))
    seq = jax.ShapeDtypeStruct((b, t, aw), F32)
    mat = jax.ShapeDtypeStruct((b, n_chunk, n_pair, PAIR, PAIR), F32)
    qh, y0, m, c0 = pl.pallas_call(
        functools.partial(_rwkv_chunk_kernel, chunk=chunk, cps=cps_a),
        grid=(b, n_chunk // cps_a), in_specs=[rows(cps_a)] * 6,
        out_specs=[rows(cps_a), rows(cps_a), mats(cps_a), mats(cps_a)], out_shape=[seq, seq, mat, mat],
        compiler_params=_cparams(("parallel", "parallel")), name="rwkv_chunk",
    )(r, lw, k, v, kk, a)
    rows_b = pl.BlockSpec((b, cps_b * chunk, aw), lambda j: (0, j, 0))
    mats_b = pl.BlockSpec((b, cps_b, n_pair, PAIR, PAIR), lambda j: (0, j, 0, 0, 0))
    return pl.pallas_call(
        functools.partial(_rwkv_state_kernel, chunk=chunk, cps=cps_b),
        grid=(n_chunk // cps_b,), in_specs=[rows_b, rows_b, mats_b, mats_b],
        out_specs=rows_b, out_shape=seq,
        scratch_shapes=[pltpu.VMEM((b, n_pair, PAIR, PAIR), F32)],
        compiler_params=_cparams(("arbitrary",)), name="rwkv_state",
    )(qh, y0, m, c0)


def _kv_prep_kernel(c_ref, k_ref, gc_ref, gk_ref, cn_ref, kn_ref):
    cn_ref[...] = _rms(c_ref[...], gc_ref[...]).astype(cn_ref.dtype)
    kn_ref[...] = _rms(k_ref[...], gk_ref[...]).astype(kn_ref.dtype)


def kv_prep(c_kv, k_dup, g_c, g_k_dup, tm=512):
    n, w = c_kv.shape
    tile = pl.BlockSpec((tm, w), lambda i: (i, 0))
    const = pl.BlockSpec((1, w), lambda i: (0, 0))
    return pl.pallas_call(
        _kv_prep_kernel, grid=(n // tm,), in_specs=[tile, tile, const, const], out_specs=[tile, tile],
        out_shape=[jax.ShapeDtypeStruct((n, w), BF16)] * 2,
        compiler_params=_cparams(("parallel",)), name="kv_prep",
    )(c_kv, k_dup, g_c.reshape(1, w), g_k_dup.reshape(1, w))


SEARCH_VALUE_STEPS = 24


def _tree_reduce(combine, xs):
    while len(xs) > 1:
        xs = [combine(xs[i], xs[i + 1]) for i in range(0, len(xs) - 1, 2)] + ([xs[-1]] if len(xs) % 2 else [])
    return xs[0]


def _key_to_f32(key):
    return pltpu.bitcast(jnp.where(key >= 0, key, key ^ 0x7FFFFFFF), F32)


def _f32_to_key(x):
    bits = pltpu.bitcast(x, I32)
    return jnp.where(bits >= 0, bits, bits ^ 0x7FFFFFFF)


def _dsa_kernel(qi_ref, wit_ref, q_ref, kn_ref, cn_ref, cnt_ref, wk_ref, wv_ref, o_ref,
                skey_ref, qpt_ref, acc_ref, m_ref, l_ref, *, qb, kt, ka, topk, n_heads, idx_bits):
    blk = pl.program_id(1)
    nt = ((blk + 1) * qb + kt - 1) // kt
    first = _lane_is_first_head()
    first_sub = lax.broadcasted_iota(I32, (PAIR, 1), 0) < HEAD_DIM
    q_pos = blk * qb + lax.broadcasted_iota(I32, (1, qb), 1)
    row0 = lax.broadcasted_iota(I32, (kt, 1), 0)
    heads = range(n_heads)

    qi = qi_ref[0]
    wit = wit_ref[0]
    qi_t = []
    for p in range(n_heads // 2):
        pair_t = qi[:, p * PAIR:(p + 1) * PAIR].T * HEAD_DIM ** -0.5
        qi_t += [jnp.where(first_sub, pair_t, 0.0).astype(BF16), jnp.where(first_sub, 0.0, pair_t).astype(BF16)]

    def score_tile(j, carry):
        off = pl.multiple_of(j * kt, kt)
        kn = kn_ref[0, pl.ds(off, kt), :]
        sc = jnp.zeros((kt, qb), F32)
        for h in heads:
            sc = sc + wit[h:h + 1, :] * jnp.maximum(_dot(kn, qi_t[h]), 0.0)
        sc = sc * n_heads ** -0.5
        key = jnp.where(sc == 0.0, 0, _f32_to_key(sc))
        skey_ref[j] = jnp.where(off + row0 <= q_pos, key, INT_MIN)
        return carry

    lax.fori_loop(0, nt, score_tile, 0)

    def fold(tile_fn, init, combine):
        def body(j, acc):
            part = tile_fn(skey_ref[j], j * kt + row0).reshape(kt // 8, 8, qb)
            return combine(acc, _tree_reduce(combine, [part[i] for i in range(kt // 8)]))
        return lax.fori_loop(0, nt, body, jnp.full((8, qb), init, F32))

    def count(pred):
        acc = fold(lambda sk, idx: jnp.where(pred(sk, idx), 1.0, 0.0), 0.0, jnp.add)
        return jnp.sum(acc, axis=0, keepdims=True)

    few = q_pos + 1 <= topk
    v_max = jnp.max(fold(lambda sk, _: jnp.where(sk == INT_MIN, -jnp.inf, _key_to_f32(sk)), -jnp.inf, jnp.maximum),
                    axis=0, keepdims=True)
    v_min = jnp.min(fold(lambda sk, _: jnp.where(sk == INT_MIN, jnp.inf, _key_to_f32(sk)), jnp.inf, jnp.minimum),
                    axis=0, keepdims=True)
    lo0 = _f32_to_key(v_min)
    hi0 = _f32_to_key(v_max) + 1
    cnt0 = (q_pos + 1).astype(F32)

    def unresolved(lo, hi, cnt_lo):
        return ~(few | (cnt_lo == topk) | (hi <= lo + 1))

    def search_cond(st):
        return (st[0] < SEARCH_VALUE_STEPS + 34) & (st[4] > 0)

    def search_step(st):
        it, lo, hi, cnt_lo, _ = st
        mid_v = _f32_to_key(0.5 * _key_to_f32(lo) + 0.5 * _key_to_f32(hi))
        mid_k = (lo >> 1) + (hi >> 1) + (lo & hi & 1)
        mid = jnp.where((it < SEARCH_VALUE_STEPS) & (mid_v > lo) & (mid_v < hi), mid_v, mid_k)
        cnt = count(lambda sk, _: sk >= mid)
        open_ = unresolved(lo, hi, cnt_lo)
        up = open_ & (cnt >= topk)
        lo, cnt_lo = jnp.where(up, mid, lo), jnp.where(up, cnt, cnt_lo)
        hi = jnp.where(open_ & ~up, mid, hi)
        return it + 1, lo, hi, cnt_lo, jnp.max(jnp.where(unresolved(lo, hi, cnt_lo), 1, 0))

    st0 = (jnp.int32(0), lo0, hi0, cnt0, jnp.max(jnp.where(unresolved(lo0, hi0, cnt0), 1, 0)))
    _, lo, _, cnt_lo, _ = lax.while_loop(search_cond, search_step, st0)
    thr = jnp.where(few, INT_MIN, lo)
    tie = (cnt_lo > topk) & ~few
    any_tie = jnp.max(jnp.where(tie, 1, 0))
    n_gt = lax.fori_loop(0, any_tie, lambda i, c: count(lambda sk, _: sk > thr), jnp.zeros((1, qb), F32))
    need = topk - n_gt

    def idx_bit(i, prefix):
        cand = prefix | jnp.left_shift(jnp.int32(1), idx_bits - 1 - i)
        cnt = count(lambda sk, kidx: (sk == thr) & (kidx < cand))
        return jnp.where(cnt < need, cand, prefix)

    jmax = lax.fori_loop(0, idx_bits * any_tie, idx_bit, jnp.zeros((1, qb), I32))
    jmax = jnp.where(tie, jmax, 0x7FFFFFFF)

    q = q_ref[0]
    for h in heads:
        pr = slice((h // 2) * PAIR, (h // 2 + 1) * PAIR)
        wk_h = jnp.where(first if h % 2 == 0 else ~first, wk_ref[:, pr], 0.0)
        qpt_ref[h] = (_dot_nt(wk_h, q[:, pr].astype(BF16)) * (HEAD_DIM ** -0.5 * math.log2(math.e))).astype(BF16)
    m_ref[...] = jnp.full_like(m_ref, NEG_BIG)
    l_ref[...] = jnp.zeros_like(l_ref)
    acc_ref[...] = jnp.zeros_like(acc_ref)

    def attn_tile(j, carry):
        sk_all = skey_ref[j]
        for part in range(kt // ka):
            off = pl.multiple_of(j * kt + part * ka, ka)
            sk = sk_all[part * ka:(part + 1) * ka, :]
            kidx = off + row0[:ka]
            sel = ((sk > thr) | ((sk == thr) & (kidx <= jmax))) & (kidx <= q_pos)
            bias = jnp.where(sel, 0.0, NEG_BIG)
            cn = cn_ref[0, pl.ds(off, ka), :]
            cn_t = cnt_ref[0, j * (kt // ka) + part]
            s = [_dot(cn, qpt_ref[h]) + bias for h in heads]
            m_old = [m_ref[h] for h in heads]
            m_new = [jnp.maximum(m_old[h], jnp.max(s[h], axis=0, keepdims=True)) for h in heads]
            pexp = [jnp.exp2(s[h] - m_new[h]) for h in heads]
            alpha = [jnp.exp2(m_old[h] - m_new[h]) for h in heads]
            pv = [_dot(cn_t, pexp[h].astype(BF16)) for h in heads]
            for h in heads:
                l_ref[h] = alpha[h] * l_ref[h] + jnp.sum(pexp[h], axis=0, keepdims=True)
                acc_ref[h] = alpha[h] * acc_ref[h] + pv[h]
                m_ref[h] = m_new[h]
        return carry

    lax.fori_loop(0, nt, attn_tile, 0)

    rank = cn_ref.shape[-1]
    for p in range(n_heads // 2):
        pr = slice(p * PAIR, (p + 1) * PAIR)
        out = jnp.zeros((qb, PAIR), F32)
        for h in (2 * p, 2 * p + 1):
            lat_t = (acc_ref[h] / l_ref[h]).astype(BF16)
            wv_h = jnp.where(first if h % 2 == 0 else ~first, wv_ref[:, pr], 0.0)
            out = out + lax.dot_general(lat_t, wv_h, _TN, preferred_element_type=F32)
        o_ref[0, :, pr] = out.astype(o_ref.dtype)


def dsa_attention(q_idx, w_idx, q_b, kn, cn, wk_t, wv, qb=128, kt=512, ka=256):
    b, t, width = q_b.shape
    n_heads = width // HEAD_DIM
    rank = cn.shape[-1]
    kt, ka = min(kt, t), min(ka, t)
    topk = min(IDX_TOPK_MAX, t // 4)
    w_t = jnp.swapaxes(w_idx[:, :, :n_heads], 1, 2)
    cn_t = jnp.swapaxes(cn.reshape(b, t // ka, ka, rank), 2, 3)
    qtile = lambda c: pl.BlockSpec((1, qb, c), lambda i, j: (i, j, 0))
    full = lambda c: pl.BlockSpec((1, t, c), lambda i, j: (i, 0, 0))
    const = lambda a: pl.BlockSpec(a.shape, lambda i, j: (0, 0))
    kern = functools.partial(_dsa_kernel, qb=qb, kt=kt, ka=ka, topk=topk, n_heads=n_heads,
                             idx_bits=max(1, (t - 1).bit_length()))
    return pl.pallas_call(
        kern, grid=(b, t // qb),
        in_specs=[qtile(width), pl.BlockSpec((1, n_heads, qb), lambda i, j: (i, 0, j)), qtile(width),
                  full(LANES), full(rank), pl.BlockSpec((1, t // ka, rank, ka), lambda i, j: (i, 0, 0, 0)),
                  const(wk_t), const(wv)],
        out_specs=qtile(width), out_shape=jax.ShapeDtypeStruct((b, t, width), BF16),
        scratch_shapes=[pltpu.VMEM((t // kt, kt, qb), I32), pltpu.VMEM((n_heads, rank, qb), BF16),
                        pltpu.VMEM((n_heads, rank, qb), F32), pltpu.VMEM((n_heads, 1, qb), F32),
                        pltpu.VMEM((n_heads, 1, qb), F32)],
        compiler_params=_cparams(("parallel", "arbitrary")), name="dsa_attention",
    )(q_idx, w_t, q_b, kn, cn, cn_t, wk_t, wv)


def _even_out_kernel(x_ref, y_ref, bonus_ref, gate_ref, yb_ref, lng_ref, lnb_ref, seg_ref, wa_ref, wb_ref, o_ref):
    seg = seg_ref[...]
    y = y_ref[...]
    yc = y - _dot_hi(y, seg) * (1.0 / HEAD_DIM)
    yn = yc * lax.rsqrt(_dot_hi(yc * yc, seg) * (1.0 / HEAD_DIM) + GN_EPS)
    ya = (yn * lng_ref[...] + lnb_ref[...] + bonus_ref[...]) * gate_ref[...]
    o_ref[...] = x_ref[...] + _dot(ya.astype(BF16), wa_ref[...]) + _dot(yb_ref[...], wb_ref[...])


def even_out(x2d, y, bonus, gate, y_b, ln_g, ln_b, w_out, tm=256):
    n, d = x2d.shape
    aw = y.shape[1]
    tile = lambda c: pl.BlockSpec((tm, c), lambda i: (i, 0))
    const = lambda a: pl.BlockSpec(a.shape, lambda i: (0, 0))
    consts = [ln_g.reshape(1, aw), ln_b.reshape(1, aw), _seg_ones(aw),
              w_out[:aw].astype(BF16), w_out[aw:].astype(BF16)]
    return pl.pallas_call(
        _even_out_kernel, grid=(n // tm,),
        in_specs=[tile(d), tile(aw), tile(aw), tile(aw), tile(y_b.shape[1])] + [const(a) for a in consts],
        out_specs=tile(d), out_shape=jax.ShapeDtypeStruct((n, d), F32),
        compiler_params=_cparams(("parallel",)), name="even_out",
    )(x2d, y, bonus, gate, y_b, *consts)


def _stickbreak_kernel(q_ref, k_ref, v_ref, o_ref, acc_ref, carry_ref, *, tq):
    qi = pl.program_id(2)
    first = _lane_is_first_head()
    row = lax.broadcasted_iota(I32, (tq, tq), 0)
    col = lax.broadcasted_iota(I32, (tq, tq), 1)
    before = col < row
    later = (row > col).astype(BF16)
    q2 = q_ref[0]
    acc_ref[...] = jnp.zeros_like(acc_ref)

    for m in (first, ~first):
        qh = jnp.where(m, q2, 0.0) * HEAD_DIM ** -0.5

        def tile(j, diag):
            off = pl.multiple_of(j * tq, tq)
            z = _dot_nt(qh, k_ref[0, pl.ds(off, tq), :])
            lk = -(jnp.maximum(z, 0.0) + jnp.log1p(jnp.exp(-jnp.abs(z))))
            if diag:
                lk = jnp.where(before, lk, 0.0)
            lk_hi = lk.astype(BF16)
            lk_lo = (lk - lk_hi.astype(F32)).astype(BF16)
            tail = _dot(lk_hi, later) + _dot(lk_lo, later) + carry_ref[...]
            w = jnp.exp(z + lk + tail)
            if diag:
                w = jnp.where(before, w, 0.0)
            vh = jnp.where(m, v_ref[0, pl.ds(off, tq), :], 0.0)
            acc_ref[...] += _dot(w.astype(BF16), vh)
            carry_ref[...] += jnp.sum(lk, axis=1, keepdims=True)
            return (jnp.max(carry_ref[...]) > -110.0).astype(I32)

        carry_ref[...] = jnp.zeros_like(carry_ref)
        live = tile(qi, True)

        def cond(st):
            return (st[0] >= 0) & (st[1] > 0)

        def body(st):
            return st[0] - 1, tile(st[0], False)

        lax.while_loop(cond, body, (qi - 1, live))

    o_ref[0] = acc_ref[...].astype(o_ref.dtype)


def stickbreak_attention(qkv, n_heads, tq=256):
    b, t, _ = qkv.shape
    n_pair = n_heads // 2
    tq = min(tq, t)
    q_spec = pl.BlockSpec((1, tq, PAIR), lambda i, p, j: (i, j, p))
    k_spec = pl.BlockSpec((1, t, PAIR), lambda i, p, j: (i, 0, n_pair + p))
    v_spec = pl.BlockSpec((1, t, PAIR), lambda i, p, j: (i, 0, 2 * n_pair + p))
    return pl.pallas_call(
        functools.partial(_stickbreak_kernel, tq=tq), grid=(b, n_pair, t // tq),
        in_specs=[q_spec, k_spec, v_spec], out_specs=q_spec,
        out_shape=jax.ShapeDtypeStruct((b, t, n_heads * HEAD_DIM), BF16),
        scratch_shapes=[pltpu.VMEM((tq, PAIR), F32), pltpu.VMEM((tq, 1), F32)],
        compiler_params=_cparams(("parallel", "parallel", "arbitrary")), name="stickbreak",
    )(qkv, qkv, qkv)


def _proj_residual_kernel(x_ref, y_ref, w_ref, o_ref):
    o_ref[...] = x_ref[...] + _dot(y_ref[...], w_ref[...])


def proj_residual(x2d, y, w, tm=256):
    n, d = x2d.shape
    c = y.shape[1]
    return pl.pallas_call(
        _proj_residual_kernel, grid=(n // tm,),
        in_specs=[pl.BlockSpec((tm, d), lambda i: (i, 0)), pl.BlockSpec((tm, c), lambda i: (i, 0)),
                  pl.BlockSpec(w.shape, lambda i: (0, 0))],
        out_specs=pl.BlockSpec((tm, d), lambda i: (i, 0)), out_shape=jax.ShapeDtypeStruct((n, d), F32),
        compiler_params=_cparams(("parallel",)), name="proj_residual",
    )(x2d, y, w)


def _router_kernel(x_ref, g_ref, wr_ref, h_ref, ids_ref, gates_ref):
    h = _rms(x_ref[...], g_ref[...])
    h_ref[...] = h.astype(h_ref.dtype)
    logits = _dot_hi(h, wr_ref[...])
    lane = lax.broadcasted_iota(I32, logits.shape, 1)
    lane_f = lane.astype(F32)

    def top1(vals):
        mx = jnp.max(vals, axis=1, keepdims=True)
        return mx, jnp.min(jnp.where(vals == mx, lane_f, 1e9), axis=1, keepdims=True).astype(I32)

    is_g = lane < N_GROUPS
    g_max, g_sel = top1(jnp.where(is_g, logits, -jnp.inf))
    g_gate = 1.0 / jnp.sum(jnp.where(is_g, jnp.exp(logits - g_max), 0.0), axis=1, keepdims=True)
    e_lane = lane - N_GROUPS
    in_grp = (e_lane >= g_sel * EXPERTS_PER_GROUP) & (e_lane < (g_sel + 1) * EXPERTS_PER_GROUP)
    el = jnp.where(in_grp, logits, -jnp.inf)
    m1, i1 = top1(el)
    m2, i2 = top1(jnp.where(lane == i1, -jnp.inf, el))
    e21 = jnp.exp(m2 - m1)
    p1 = 1.0 / (1.0 + e21)
    ids_ref[...] = jnp.where(lane == 0, i1 - N_GROUPS, jnp.where(lane == 1, i2 - N_GROUPS, 0))
    gates_ref[...] = jnp.where(lane == 0, g_gate * p1, jnp.where(lane == 1, g_gate * (e21 * p1), 0.0))


def moe_router(x2d, g, w_router, tm=256):
    n, d = x2d.shape
    tile = lambda c: pl.BlockSpec((tm, c), lambda i: (i, 0))
    return pl.pallas_call(
        _router_kernel, grid=(n // tm,),
        in_specs=[tile(d), pl.BlockSpec((1, d), lambda i: (0, 0)), pl.BlockSpec(w_router.shape, lambda i: (0, 0))],
        out_specs=[tile(d), tile(LANES), tile(LANES)],
        out_shape=[jax.ShapeDtypeStruct((n, d), BF16), jax.ShapeDtypeStruct((n, LANES), I32),
                   jax.ShapeDtypeStruct((n, LANES), F32)],
        compiler_params=_cparams(("parallel",)), name="moe_router",
    )(x2d, g.reshape(1, d), w_router)


MOE_TT = 2048
MOE_CH = 128
MOE_KC = 512
_PLAN_BLK = 256


def _moe_plan_kernel(ids_ref, idst_ref, rank_ref, dest_ref, meta_ref, *, tt):
    nb = tt // _PLAN_BLK
    row = lax.broadcasted_iota(I32, (_PLAN_BLK, _PLAN_BLK), 0)
    col = lax.broadcasted_iota(I32, (_PLAN_BLK, _PLAN_BLK), 1)
    earlier_t = (row < col).astype(BF16)
    earlier = (col < row).astype(BF16)
    e_sub = lax.broadcasted_iota(I32, (N_EXPERTS, _PLAN_BLK), 0)
    carry_t = jnp.zeros((N_EXPERTS, 1), F32)
    for b in range(nb):
        sl = slice(b * _PLAN_BLK, (b + 1) * _PLAN_BLK)
        member = (e_sub == idst_ref[0:1, sl]) | (e_sub == idst_ref[1:2, sl])
        m_t = jnp.where(member, 1.0, 0.0)
        before = _dot(m_t.astype(BF16), earlier_t) + carry_t
        rank_ref[0, :, sl] = jnp.where(member, before, -1.0)
        carry_t = carry_t + jnp.sum(m_t, axis=1, keepdims=True)
    lane = lax.broadcasted_iota(I32, (_PLAN_BLK, LANES), 1)
    carry = jnp.zeros((1, LANES), F32)
    prefix = []
    for b in range(nb):
        sl = slice(b * _PLAN_BLK, (b + 1) * _PLAN_BLK)
        m = jnp.where((lane == ids_ref[sl, 0:1]) | (lane == ids_ref[sl, 1:2]), 1.0, 0.0)
        prefix.append(_dot(earlier, m.astype(BF16)) + carry)
        carry = carry + jnp.sum(m, axis=0, keepdims=True)
    counts = carry
    padded = jnp.ceil(counts * (1.0 / MOE_CH)) * MOE_CH
    l_row = lax.broadcasted_iota(I32, (LANES, LANES), 0)
    l_col = lax.broadcasted_iota(I32, (LANES, LANES), 1)
    offs = _dot_hi(padded, (l_row < l_col).astype(F32))
    for b in range(nb):
        sl = slice(b * _PLAN_BLK, (b + 1) * _PLAN_BLK)
        where_row = prefix[b] + offs
        d0 = jnp.sum(jnp.where(lane == ids_ref[sl, 0:1], where_row, 0.0), axis=1, keepdims=True)
        d1 = jnp.sum(jnp.where(lane == ids_ref[sl, 1:2], where_row, 0.0), axis=1, keepdims=True)
        dest_ref[sl, :] = jnp.where(lane == 0, d0, jnp.where(lane == 1, d1, 0.0)).astype(I32)
    r_used = jnp.sum(padded, axis=1, keepdims=True)
    sub8 = lax.broadcasted_iota(I32, (8, LANES), 0)
    meta_ref[0] = jnp.where(sub8 == 0, counts, jnp.where(sub8 == 1, offs, jnp.where(sub8 == 2, r_used, 0.0))).astype(I32)


def moe_plan(ids, ids_t, tt):
    n = ids.shape[0]
    n_tiles = n // tt
    return pl.pallas_call(
        functools.partial(_moe_plan_kernel, tt=tt), grid=(n_tiles,),
        in_specs=[pl.BlockSpec((tt, LANES), lambda i: (i, 0)), pl.BlockSpec((8, tt), lambda i: (0, i))],
        out_specs=[pl.BlockSpec((1, N_EXPERTS, tt), lambda i: (i, 0, 0)), pl.BlockSpec((tt, LANES), lambda i: (i, 0)),
                   pl.BlockSpec((1, 8, LANES), lambda i: (i, 0, 0))],
        out_shape=[jax.ShapeDtypeStruct((n_tiles, N_EXPERTS, tt), F32), jax.ShapeDtypeStruct((n, LANES), I32),
                   jax.ShapeDtypeStruct((n_tiles, 8, LANES), I32)],
        compiler_params=_cparams(("parallel",)), name="moe_plan",
    )(ids, ids_t)


def _moe_ffn_kernel(cnt_ref, off_ref, h_ref, rank_ref, idst_ref, gt_ref, w1_ref, w3_ref, w2_ref, y_ref, *, tt):
    t, e = pl.program_id(0), pl.program_id(1)

    @pl.when(e == 0)
    def _():
        y_ref[...] = jnp.zeros_like(y_ref)

    cnt = cnt_ref[t * N_EXPERTS + e]
    off = off_ref[t * N_EXPERTS + e]
    rank_row = rank_ref[0, pl.ds(e, 1), :]
    gate_row = jnp.where(idst_ref[0:1, :] == e, gt_ref[0:1, :], gt_ref[1:2, :])
    sub = lax.broadcasted_iota(I32, (MOE_CH, 1), 0)

    def chunk(c, carry):
        r0 = c * MOE_CH
        pick = rank_row == (r0 + sub).astype(F32)
        x = _dot(jnp.where(pick, 1.0, 0.0).astype(BF16), h_ref[...]).astype(BF16)
        a = _dot(x, w1_ref[0])
        hdn = (a * jax.nn.sigmoid(a)) * _dot(x, w3_ref[0])
        y = _dot(hdn.astype(BF16), w2_ref[0])
        gate = jnp.sum(jnp.where(pick, gate_row, 0.0), axis=1, keepdims=True)
        y_ref[0, pl.ds(pl.multiple_of(off + r0, MOE_CH), MOE_CH), :] = (y * gate).astype(y_ref.dtype)
        return carry

    lax.fori_loop(0, (cnt + MOE_CH - 1) // MOE_CH, chunk, 0)


def moe_ffn(h, rank_t, ids_t, gates_t, counts, offs, w1, w3, w2, tt):
    n, d = h.shape
    ff = w1.shape[-1]
    n_tiles = n // tt
    r_max = TOP_K_EXPERTS * tt + N_EXPERTS * MOE_CH
    grid_spec = pltpu.PrefetchScalarGridSpec(
        num_scalar_prefetch=2, grid=(n_tiles, N_EXPERTS),
        in_specs=[pl.BlockSpec((tt, d), lambda i, e, c, o: (i, 0)),
                  pl.BlockSpec((1, N_EXPERTS, tt), lambda i, e, c, o: (i, 0, 0)),
                  pl.BlockSpec((8, tt), lambda i, e, c, o: (0, i)),
                  pl.BlockSpec((8, tt), lambda i, e, c, o: (0, i)),
                  pl.BlockSpec((1, d, ff), lambda i, e, c, o: (e, 0, 0)),
                  pl.BlockSpec((1, d, ff), lambda i, e, c, o: (e, 0, 0)),
                  pl.BlockSpec((1, ff, d), lambda i, e, c, o: (e, 0, 0))],
        out_specs=pl.BlockSpec((1, r_max, d), lambda i, e, c, o: (i, 0, 0)))
    return pl.pallas_call(
        functools.partial(_moe_ffn_kernel, tt=tt), grid_spec=grid_spec,
        out_shape=jax.ShapeDtypeStruct((n_tiles, r_max, d), BF16),
        compiler_params=_cparams(("arbitrary", "arbitrary")), name="moe_ffn",
    )(counts, offs, h, rank_t, ids_t, gates_t, w1, w3, w2)


def _moe_combine_kernel(used_ref, x_ref, dest_ref, y_ref, o_ref):
    t = pl.program_id(0)
    d0, d1 = dest_ref[:, 0:1], dest_ref[:, 1:2]
    col = lax.broadcasted_iota(I32, (1, MOE_KC), 1)
    o_ref[...] = x_ref[...]

    def step(k, carry):
        r0 = pl.multiple_of(k * MOE_KC, MOE_KC)
        rows = r0 + col
        pick = jnp.where((rows == d0) | (rows == d1), 1.0, 0.0).astype(BF16)
        o_ref[...] += _dot(pick, y_ref[0, pl.ds(r0, MOE_KC), :])
        return carry

    lax.fori_loop(0, (used_ref[t] + MOE_KC - 1) // MOE_KC, step, 0)


def moe_combine(x2d, dest, y_all, r_used, tt, tm=256):
    n, d = x2d.shape
    r_max = y_all.shape[1]
    per = tt // tm
    grid_spec = pltpu.PrefetchScalarGridSpec(
        num_scalar_prefetch=1, grid=(n // tt, per),
        in_specs=[pl.BlockSpec((tm, d), lambda i, j, u: (i * per + j, 0)),
                  pl.BlockSpec((tm, LANES), lambda i, j, u: (i * per + j, 0)),
                  pl.BlockSpec((1, r_max, d), lambda i, j, u: (i, 0, 0))],
        out_specs=pl.BlockSpec((tm, d), lambda i, j, u: (i * per + j, 0)))
    return pl.pallas_call(
        _moe_combine_kernel, grid_spec=grid_spec, out_shape=jax.ShapeDtypeStruct((n, d), F32),
        compiler_params=_cparams(("arbitrary", "arbitrary")), name="moe_combine",
    )(r_used, x2d, dest, y_all)


def _final_norm_kernel(x_ref, g_ref, o_ref):
    o_ref[...] = _rms(x_ref[...], g_ref[...])


def final_norm(x2d, g, tm=512):
    n, d = x2d.shape
    return pl.pallas_call(
        _final_norm_kernel, grid=(n // tm,),
        in_specs=[pl.BlockSpec((tm, d), lambda i: (i, 0)), pl.BlockSpec((1, d), lambda i: (0, 0))],
        out_specs=pl.BlockSpec((tm, d), lambda i: (i, 0)), out_shape=jax.ShapeDtypeStruct((n, d), F32),
        compiler_params=_cparams(("parallel",)), name="final_norm",
    )(x2d, g.reshape(1, d))


def moe_layer(x2d, norm_g, router_group, router_expert, w1, w3, w2):
    n, d = x2d.shape
    tt = min(MOE_TT, n)
    w_router = jnp.concatenate([router_group, jnp.moveaxis(router_expert, 0, 1).reshape(d, N_EXPERTS)], axis=1)
    h, ids, gates = moe_router(x2d, norm_g, _pad_cols(w_router, LANES))
    ids_t = jnp.pad(ids[:, :TOP_K_EXPERTS].T, ((0, 8 - TOP_K_EXPERTS), (0, 0)))
    gates_t = jnp.pad(gates[:, :TOP_K_EXPERTS].T, ((0, 8 - TOP_K_EXPERTS), (0, 0)))
    rank_t, dest, meta = moe_plan(ids, ids_t, tt)
    counts = meta[:, 0, :N_EXPERTS].reshape(-1)
    offs = meta[:, 1, :N_EXPERTS].reshape(-1)
    y_all = moe_ffn(h, rank_t, ids_t, gates_t, counts, offs, w1.astype(BF16), w3.astype(BF16), w2.astype(BF16), tt)
    return moe_combine(x2d, dest, y_all, meta[:, 2, 0], tt)


def even_layer(x, v_first, p):
    b, t, d = x.shape
    n = b * t
    w_in = p['w_in']
    aw = p['k_k'].shape[0]
    bw = p['w_kv_up'].shape[1] // 2
    rank = p['w_kv_up'].shape[0]
    n_idx = w_in.shape[1] - (3 * aw + bw + rank + bw + HEAD_DIM)
    o = 0
    segs = []
    for width in (3 * aw, bw, rank, bw, HEAD_DIM, n_idx):
        segs.append(w_in[:, o:o + width])
        o += width
    segs[4] = jnp.concatenate([segs[4], segs[4]], axis=1)
    segs[5] = _pad_cols(segs[5], LANES)
    rkv, q_b, c_kv, q_idx, k_dup, w_idx = norm_proj(
        x.reshape(n, d), p['norm'], [s.astype(BF16) for s in segs], [F32] * 6)

    r, lw, k_mod, v, kk, iclr, gate, bonus = rwkv_prep(x, rkv.reshape(b, t, 3 * aw), v_first, p)
    y = rwkv_scan(r, lw, k_mod, v, kk, iclr)

    cn, kn = kv_prep(c_kv, k_dup, p['kv_norm'], jnp.concatenate([p['idx_k_norm']] * 2))
    wk_t = p['w_kv_up'][:, :bw].astype(BF16)
    wv = p['w_kv_up'][:, bw:].astype(BF16)
    y_b = dsa_attention(q_idx.reshape(b, t, bw), w_idx.reshape(b, t, LANES), q_b.reshape(b, t, bw),
                        kn.reshape(b, t, LANES), cn.reshape(b, t, rank), wk_t, wv)

    x_new = even_out(x.reshape(n, d), y.reshape(n, aw), bonus.reshape(n, aw), gate.reshape(n, aw),
                     y_b.reshape(n, bw), p['ln_g'], p['ln_b'], p['w_out'])
    return x_new.reshape(b, t, d), v


def odd_layer(x, norm_g, w_qkv, w_out):
    b, t, d = x.shape
    n = b * t
    cw = w_qkv.shape[1] // 3
    (qkv,) = norm_proj(x.reshape(n, d), norm_g, [w_qkv.astype(BF16)], [BF16])
    y_c = stickbreak_attention(qkv.reshape(b, t, 3 * cw), cw // HEAD_DIM)
    return proj_residual(x.reshape(n, d), y_c.reshape(n, cw), w_out.astype(BF16)).reshape(b, t, d)


@jax.jit
def _forward(x, ev_norm, ev_w_in, ev_w_out, a_mu_rkv, a_mu_lora, a_w0, a_w1, a_w2, a_a0, a_a1, a_a2, a_g1,
             a_g2, a_k_k, a_k_a, a_r_k, a_ln_g, a_ln_b, a_mu_vres, a_v0, a_v1, a_v2, b_kv_norm, b_w_kv_up,
             b_idx_k_norm, od_norm, od_w_qkv, od_w_out, moe_norm, moe_router_group, moe_router_expert,
             moe_w1, moe_w3, moe_w2, final_norm_g):
    b, t, d = x.shape
    depth = moe_norm.shape[0]
    v_first = None
    for i in range(depth):
        if i % 2 == 0:
            e = i // 2
            p = dict(norm=ev_norm[e], w_in=ev_w_in[e], w_out=ev_w_out[e], mu_rkv=a_mu_rkv[e], mu_lora=a_mu_lora[e],
                     w0=a_w0[e], w1=a_w1[e], w2=a_w2[e], a0=a_a0[e], a1=a_a1[e], a2=a_a2[e], g1=a_g1[e], g2=a_g2[e],
                     k_k=a_k_k[e], k_a=a_k_a[e], r_k=a_r_k[e].reshape(-1), ln_g=a_ln_g[e], ln_b=a_ln_b[e],
                     kv_norm=b_kv_norm[e], w_kv_up=b_w_kv_up[e], idx_k_norm=b_idx_k_norm[e])
            if e > 0:
                p.update(mu_v=a_mu_vres[e - 1], v0=a_v0[e - 1], v1=a_v1[e - 1], v2=a_v2[e - 1])
            x, v_used = even_layer(x, v_first if e > 0 else None, p)
            if e == 0:
                v_first = v_used
        else:
            o = i // 2
            x = odd_layer(x, od_norm[o], od_w_qkv[o], od_w_out[o])
        x = moe_layer(x.reshape(b * t, d), moe_norm[i], moe_router_group[i], moe_router_expert[i],
                      moe_w1[i], moe_w3[i], moe_w2[i]).reshape(b, t, d)
    return final_norm(x.reshape(b * t, d), final_norm_g).reshape(b, t, d)


def kernel(x, ev_norm, ev_w_in, ev_w_out, a_mu_rkv, a_mu_lora, a_w0, a_w1, a_w2, a_a0, a_a1, a_a2, a_g1, a_g2, a_k_k, a_k_a, a_r_k, a_ln_g, a_ln_b, a_mu_vres, a_v0, a_v1, a_v2, b_kv_norm, b_w_kv_up, b_idx_k_norm, od_norm, od_w_qkv, od_w_out, moe_norm, moe_router_group, moe_router_expert, moe_w1, moe_w3, moe_w2, final_norm):
    return _forward(x, ev_norm, ev_w_in, ev_w_out, a_mu_rkv, a_mu_lora, a_w0, a_w1, a_w2, a_a0, a_a1, a_a2, a_g1,
                    a_g2, a_k_k, a_k_a, a_r_k, a_ln_g, a_ln_b, a_mu_vres, a_v0, a_v1, a_v2, b_kv_norm, b_w_kv_up,
                    b_idx_k_norm, od_norm, od_w_qkv, od_w_out, moe_norm, moe_router_group, moe_router_expert,
                    moe_w1, moe_w3, moe_w2, final_norm)
```

```python
import functools
import math

import jax
import jax.numpy as jnp
from jax import lax
from jax.experimental import pallas as pl
from jax.experimental.pallas import tpu as pltpu

F32 = jnp.float32
BF16 = jnp.bfloat16
I32 = jnp.int32
HI = lax.Precision.HIGHEST

LANES = 128
HEAD_DIM = 64
PAIR = 2 * HEAD_DIM
RMS_EPS = 1e-6
GN_EPS = 64e-5
N_GROUPS = 4
EXPERTS_PER_GROUP = 8
N_EXPERTS = N_GROUPS * EXPERTS_PER_GROUP
TOP_K_EXPERTS = 2
IDX_TOPK_MAX = 256
INT_MIN = -2147483648
NEG_BIG = -1e30
VMEM_LIMIT = 56 * 1024 * 1024

_NT = (((1,), (1,)), ((), ()))
_TN = (((0,), (0,)), ((), ()))


def _cparams(sem):
    return pltpu.CompilerParams(dimension_semantics=sem, vmem_limit_bytes=VMEM_LIMIT)


def _rms(x, g):
    return x * lax.rsqrt(jnp.mean(x * x, axis=-1, keepdims=True) + RMS_EPS) * g


def _dot(a, b):
    return jnp.dot(a, b, preferred_element_type=F32)


def _dot_hi(a, b):
    return jnp.dot(a, b, preferred_element_type=F32, precision=HI)


def _dot_nt(a, b):
    return lax.dot_general(a, b, _NT, preferred_element_type=F32)


def _lane_is_first_head(width=PAIR):
    return lax.broadcasted_iota(I32, (1, width), 1) % PAIR < HEAD_DIM


def _norm_proj_kernel(x_ref, g_ref, *refs, n_seg):
    w_refs, o_refs = refs[:n_seg], refs[n_seg:]
    h = _rms(x_ref[...], g_ref[...]).astype(BF16)
    for w_ref, o_ref in zip(w_refs, o_refs):
        o_ref[...] = _dot(h, w_ref[...]).astype(o_ref.dtype)


def norm_proj(x2d, g, ws, out_dtypes, tm=256):
    n, d = x2d.shape
    in_specs = [pl.BlockSpec((tm, d), lambda i: (i, 0)), pl.BlockSpec((1, d), lambda i: (0, 0))]
    in_specs += [pl.BlockSpec(w.shape, lambda i: (0, 0)) for w in ws]
    out_specs = [pl.BlockSpec((tm, w.shape[1]), lambda i: (i, 0)) for w in ws]
    out_shape = [jax.ShapeDtypeStruct((n, w.shape[1]), dt) for w, dt in zip(ws, out_dtypes)]
    return pl.pallas_call(
        functools.partial(_norm_proj_kernel, n_seg=len(ws)),
        grid=(n // tm,), in_specs=in_specs, out_specs=out_specs, out_shape=out_shape,
        compiler_params=_cparams(("parallel",)), name="norm_proj",
    )(x2d, g.reshape(1, d), *ws)


def _shift_rows(z, carry_ref):
    tm = z.shape[0]
    first = lax.broadcasted_iota(I32, (tm, 1), 0) == 0
    prev = jnp.where(first, carry_ref[...], pltpu.roll(z, 1, 0))
    carry_ref[...] = z[tm - 1:tm, :]
    return prev


def _rwkv_prep_kernel(*refs, has_vres):
    it = iter(refs)
    x_ref, rkv_ref = next(it), next(it)
    vfirst_ref = next(it) if has_vres else None
    g_ref, mu_rkv_ref, mu_lora_ref = next(it), next(it), next(it)
    w0_ref, w1_ref, w2_ref = next(it), next(it), next(it)
    a0_ref, a1_ref, a2_ref = next(it), next(it), next(it)
    g1_ref, g2_ref = next(it), next(it)
    if has_vres:
        muv_ref, v0_ref, v1_ref, v2_ref = next(it), next(it), next(it), next(it)
    kk_w_ref, ka_ref, rk_ref, seg_ref = next(it), next(it), next(it), next(it)
    r_o, lw_o, k_o, v_o, kk_o, a_o, gate_o, bonus_o = (next(it) for _ in range(8))
    hcarry, rkvcarry = next(it), next(it)

    @pl.when(pl.program_id(1) == 0)
    def _():
        hcarry[...] = jnp.zeros_like(hcarry)
        rkvcarry[...] = jnp.zeros_like(rkvcarry)

    aw = r_o.shape[-1]
    h = _rms(x_ref[0], g_ref[...])
    dh = _shift_rows(h, hcarry) - h
    rkv = rkv_ref[0]
    rkv = rkv + (_shift_rows(rkv, rkvcarry) - rkv) * mu_rkv_ref[...]
    r, k, v = rkv[:, :aw], rkv[:, aw:2 * aw], rkv[:, 2 * aw:]

    def lora_in(row):
        return (h + dh * mu_lora_ref[row:row + 1, :]).astype(BF16)

    dec = w0_ref[...] + _dot(jnp.tanh(_dot(lora_in(0), w1_ref[...])).astype(BF16), w2_ref[...])
    logw = -math.exp(-0.5) * jax.nn.sigmoid(dec)
    iclr = jax.nn.sigmoid(a0_ref[...] + _dot(_dot(lora_in(1), a1_ref[...]).astype(BF16), a2_ref[...]))
    gate = _dot(jax.nn.sigmoid(_dot(lora_in(2), g1_ref[...])).astype(BF16), g2_ref[...])
    if has_vres:
        xv = (h + dh * muv_ref[...]).astype(BF16)
        mix = jax.nn.sigmoid(v0_ref[...] + _dot(_dot(xv, v1_ref[...]).astype(BF16), v2_ref[...]))
        v = v + (vfirst_ref[0] - v) * mix

    seg = seg_ref[...]
    kk = k * kk_w_ref[...]
    kk = kk * lax.rsqrt(jnp.maximum(_dot_hi(kk * kk, seg), 1e-12))
    k_mod = k * (1.0 + (iclr - 1.0) * ka_ref[...])
    bonus = _dot_hi(r * k_mod * rk_ref[...], seg) * v

    r_o[0], lw_o[0], k_o[0], v_o[0] = r, logw, k_mod, v
    kk_o[0], a_o[0], gate_o[0], bonus_o[0] = kk, iclr, gate, bonus


def _pad_cols(w, n):
    return jnp.pad(w, ((0, 0), (0, n - w.shape[1])))


def _pad_rows(w, n):
    return jnp.pad(w, ((0, n - w.shape[0]), (0, 0)))


def _seg_ones(width):
    i = jnp.arange(width) // HEAD_DIM
    return (i[:, None] == i[None, :]).astype(F32)


def rwkv_prep(x, rkv, v_first, p, tm=256):
    b, t, d = x.shape
    aw = rkv.shape[-1] // 3
    has_vres = v_first is not None
    row = lambda a: a.reshape(1, -1)
    const = lambda a: pl.BlockSpec(a.shape, lambda i, j: (0,) * a.ndim)
    tile = lambda c: pl.BlockSpec((1, tm, c), lambda i, j: (i, j, 0))

    args, specs = [x, rkv], [tile(d), tile(3 * aw)]
    if has_vres:
        args.append(v_first)
        specs.append(tile(aw))
    consts = [row(p['norm']), p['mu_rkv'].reshape(1, 3 * aw), p['mu_lora'],
              row(p['w0']), _pad_cols(p['w1'], LANES).astype(BF16), _pad_rows(p['w2'], LANES).astype(BF16),
              row(p['a0']), _pad_cols(p['a1'], LANES).astype(BF16), _pad_rows(p['a2'], LANES).astype(BF16),
              p['g1'].astype(BF16), p['g2'].astype(BF16)]
    if has_vres:
        consts += [row(p['mu_v']), row(p['v0']), _pad_cols(p['v1'], LANES).astype(BF16),
                   _pad_rows(p['v2'], LANES).astype(BF16)]
    consts += [row(p['k_k']), row(p['k_a']), row(p['r_k']), _seg_ones(aw)]
    args += consts
    specs += [const(a) for a in consts]
    out_shape = [jax.ShapeDtypeStruct((b, t, aw), F32)] * 8
    return pl.pallas_call(
        functools.partial(_rwkv_prep_kernel, has_vres=has_vres),
        grid=(b, t // tm), in_specs=specs, out_specs=[tile(aw)] * 8, out_shape=out_shape,
        scratch_shapes=[pltpu.VMEM((1, d), F32), pltpu.VMEM((1, 3 * aw), F32)],
        compiler_params=_cparams(("arbitrary", "arbitrary")), name="rwkv_prep",
    )(*args)


def _split_bf16(x):
    hi = x.astype(BF16)
    return hi, (x - hi.astype(F32)).astype(BF16)


def _mm3(a, b, dims=None):
    a_hi, a_lo = _split_bf16(a)
    b_hi, b_lo = _split_bf16(b)
    if dims is None:
        f = lambda p, q: jnp.dot(p, q, preferred_element_type=F32)
    else:
        f = lambda p, q: lax.dot_general(p, q, dims, preferred_element_type=F32)
    return f(a_hi, b_hi) + f(a_hi, b_lo) + f(a_lo, b_hi)


def _mmc(a, b, dims=None):
    a, b = a.astype(BF16), b.astype(BF16)
    if dims is None:
        return jnp.dot(a, b, preferred_element_type=F32)
    return lax.dot_general(a, b, dims, preferred_element_type=F32)


def _cumsum_rows(x):
    n = x.shape[0]
    row = lax.broadcasted_iota(I32, (n, 1), 0)
    s = 1
    while s < n:
        x = x + jnp.where(row >= s, pltpu.roll(x, s, 0), 0.0)
        s *= 2
    return x


def _rwkv_chunk_kernel(r_ref, lw_ref, k_ref, v_ref, kk_ref, a_ref, qh_ref, y0_ref, m_ref, c0_ref, *, chunk, cps):
    L = chunk
    n_pair = r_ref.shape[-1] // PAIR
    row = lax.broadcasted_iota(I32, (L, L), 0)
    col = lax.broadcasted_iota(I32, (L, L), 1)
    strict = col < row
    incl = col <= row
    eye = (col == row).astype(F32)
    first = _lane_is_first_head()
    p_row = lax.broadcasted_iota(I32, (PAIR, PAIR), 0)
    p_col = lax.broadcasted_iota(I32, (PAIR, PAIR), 1)
    blockdiag = p_row // HEAD_DIM == p_col // HEAD_DIM
    pair_eye = p_row == p_col

    units = [(ci, p) for ci in range(cps) for p in range(n_pair)]
    heads = (first, ~first)
    pre = {}
    for u in units:
        ci, p = u
        rows = slice(ci * L, (ci + 1) * L)
        sl = slice(p * PAIR, (p + 1) * PAIR)
        lw = lw_ref[0, rows, sl]
        c = _cumsum_rows(lw)
        g = jnp.exp(c)
        g_inv = jnp.exp(-c)
        kk = kk_ref[0, rows, sl]
        pre[u] = dict(rows=rows, sl=sl, g_last=g[L - 1:L, :],
                      at=-kk * jnp.exp(c - lw),
                      bt=kk * a_ref[0, rows, sl] * g_inv,
                      kt=k_ref[0, rows, sl] * g_inv,
                      rt=r_ref[0, rows, sl] * g,
                      vv=v_ref[0, rows, sl])
    uh = [(u, h) for u in units for h in range(2)]

    gm = {}
    for u, h in uh:
        d = pre[u]
        lhs = jnp.concatenate([jnp.where(heads[h], d['at'], 0.0), jnp.where(heads[h], d['rt'], 0.0)], axis=0)
        gm[u, h] = _mm3(lhs, jnp.concatenate([d['bt'], d['kt']], axis=0), _NT)
    a_ak = {x: jnp.where(strict, gm[x][:L, L:], 0.0) for x in uh}
    a_rb = {x: jnp.where(incl, gm[x][L:, :L], 0.0) for x in uh}
    a_rk = {x: jnp.where(incl, gm[x][L:, L:], 0.0) for x in uh}
    apow = {x: jnp.where(strict, gm[x][:L, :L], 0.0) for x in uh}
    inv = {x: eye + apow[x] for x in uh}
    for _ in range(max(1, (L - 1).bit_length()) - 1):
        apow = {x: _mmc(apow[x], apow[x]) for x in uh}
        inv = {x: inv[x] + _mmc(inv[x], apow[x]) for x in uh}
    w_h = {(u, h): _mmc(inv[u, h], pre[u]['at']) for u, h in uh}
    akv = {(u, h): _mmc(a_ak[u, h], pre[u]['vv']) for u, h in uh}
    u0_h = {x: _mmc(inv[x], akv[x]) for x in uh}
    w = {u: jnp.where(first, w_h[u, 0], w_h[u, 1]) for u in units}
    u0 = {u: jnp.where(first, u0_h[u, 0], u0_h[u, 1]) for u in units}
    wu = {u: jnp.concatenate([w[u], u0[u]], axis=1) for u in units}
    arb = {(u, h): _mmc(a_rb[u, h], wu[u]) for u, h in uh}
    ark = {(u, h): _mmc(a_rk[u, h], pre[u]['vv']) for u, h in uh}
    first2 = jnp.concatenate([first, first], axis=1)
    for u in units:
        d = pre[u]
        ci, p = u
        arb_w = jnp.where(first2, arb[u, 0], arb[u, 1])
        qh_ref[0, d['rows'], d['sl']] = d['rt'] + arb_w[:, :PAIR]
        y0_ref[0, d['rows'], d['sl']] = arb_w[:, PAIR:] + jnp.where(first, ark[u, 0], ark[u, 1])
        bg = d['bt'] * d['g_last']
        m_kk = _mmc(bg, w[u], _TN)
        m_ref[0, ci, p] = jnp.where(pair_eye, d['g_last'], 0.0) + jnp.where(blockdiag, m_kk, 0.0)
        c0 = _mmc(jnp.concatenate([u0[u], d['vv']], axis=0),
                  jnp.concatenate([bg, d['kt'] * d['g_last']], axis=0), _TN)
        c0_ref[0, ci, p] = jnp.where(blockdiag, c0, 0.0)


def _rwkv_state_kernel(qh_ref, y0_ref, m_ref, c0_ref, y_ref, s_ref, *, chunk, cps):
    L = chunk

    @pl.when(pl.program_id(0) == 0)
    def _():
        s_ref[...] = jnp.zeros_like(s_ref)

    n_batch = qh_ref.shape[0]
    n_pair = qh_ref.shape[-1] // PAIR
    streams = [(b, p) for b in range(n_batch) for p in range(n_pair)]
    s2 = {x: s_ref[x[0], x[1]] for x in streams}
    for ci in range(cps):
        rows = slice(ci * L, (ci + 1) * L)
        for b, p in streams:
            sl = slice(p * PAIR, (p + 1) * PAIR)
            y_ref[b, rows, sl] = _mm3(qh_ref[b, rows, sl], s2[b, p], _NT) + y0_ref[b, rows, sl]
            s2[b, p] = _mm3(s2[b, p], m_ref[b, ci, p], _NT) + c0_ref[b, ci, p]
    for b, p in streams:
        s_ref[b, p] = s2[b, p]


def rwkv_scan(r, lw, k, v, kk, a, chunk=64, cps_a=2, cps_b=4):
    b, t, aw = r.shape
    n_pair = aw // PAIR
    n_chunk = t // chunk
    cps_a, cps_b = min(cps_a, n_chunk), min(cps_b, n_chunk)
    rows = lambda c: pl.BlockSpec((1, c * chunk, aw), lambda i, j: (i, j, 0))
    mats = lambda c: pl.BlockSpec((1, c, n_pair, PAIR, PAIR), lambda i, j: (i, j, 0, 0, 0))
    seq = jax.ShapeDtypeStruct((b, t, aw), F32)
    mat = jax.ShapeDtypeStruct((b, n_chunk, n_pair, PAIR, PAIR), F32)
    qh, y0, m, c0 = pl.pallas_call(
        functools.partial(_rwkv_chunk_kernel, chunk=chunk, cps=cps_a),
        grid=(b, n_chunk // cps_a), in_specs=[rows(cps_a)] * 6,
        out_specs=[rows(cps_a), rows(cps_a), mats(cps_a), mats(cps_a)], out_shape=[seq, seq, mat, mat],
        compiler_params=_cparams(("parallel", "parallel")), name="rwkv_chunk",
    )(r, lw, k, v, kk, a)
    rows_b = pl.BlockSpec((b, cps_b * chunk, aw), lambda j: (0, j, 0))
    mats_b = pl.BlockSpec((b, cps_b, n_pair, PAIR, PAIR), lambda j: (0, j, 0, 0, 0))
    return pl.pallas_call(
        functools.partial(_rwkv_state_kernel, chunk=chunk, cps=cps_b),
        grid=(n_chunk // cps_b,), in_specs=[rows_b, rows_b, mats_b, mats_b],
        out_specs=rows_b, out_shape=seq,
        scratch_shapes=[pltpu.VMEM((b, n_pair, PAIR, PAIR), F32)],
        compiler_params=_cparams(("arbitrary",)), name="rwkv_state",
    )(qh, y0, m, c0)


def _kv_prep_kernel(c_ref, k_ref, gc_ref, gk_ref, cn_ref, kn_ref):
    cn_ref[...] = _rms(c_ref[...], gc_ref[...]).astype(cn_ref.dtype)
    kn_ref[...] = _rms(k_ref[...], gk_ref[...]).astype(kn_ref.dtype)


def kv_prep(c_kv, k_dup, g_c, g_k_dup, tm=512):
    n, w = c_kv.shape
    tile = pl.BlockSpec((tm, w), lambda i: (i, 0))
    const = pl.BlockSpec((1, w), lambda i: (0, 0))
    return pl.pallas_call(
        _kv_prep_kernel, grid=(n // tm,), in_specs=[tile, tile, const, const], out_specs=[tile, tile],
        out_shape=[jax.ShapeDtypeStruct((n, w), BF16)] * 2,
        compiler_params=_cparams(("parallel",)), name="kv_prep",
    )(c_kv, k_dup, g_c.reshape(1, w), g_k_dup.reshape(1, w))


SEARCH_VALUE_STEPS = 24
SEARCH_STEPS_PER_CHECK = 2


def _tree_reduce(combine, xs):
    while len(xs) > 1:
        xs = [combine(xs[i], xs[i + 1]) for i in range(0, len(xs) - 1, 2)] + ([xs[-1]] if len(xs) % 2 else [])
    return xs[0]


def _key_to_f32(key):
    return pltpu.bitcast(jnp.where(key >= 0, key, key ^ 0x7FFFFFFF), F32)


def _f32_to_key(x):
    bits = pltpu.bitcast(x, I32)
    return jnp.where(bits >= 0, bits, bits ^ 0x7FFFFFFF)


def _dsa_kernel(qi_ref, wit_ref, q_ref, kn_ref, cn_ref, cnt_ref, wk_ref, wv_ref, o_ref,
                skey_ref, qpt_ref, acc_ref, m_ref, l_ref, *, qb, kt, ka, topk, n_heads, idx_bits):
    blk = pl.program_id(1)
    nt = ((blk + 1) * qb + kt - 1) // kt
    first = _lane_is_first_head()
    first_sub = lax.broadcasted_iota(I32, (PAIR, 1), 0) < HEAD_DIM
    q_pos = blk * qb + lax.broadcasted_iota(I32, (1, qb), 1)
    row0 = lax.broadcasted_iota(I32, (kt, 1), 0)
    heads = range(n_heads)

    qi = qi_ref[0]
    wit = wit_ref[0]
    qi_t = []
    for p in range(n_heads // 2):
        pair_t = qi[:, p * PAIR:(p + 1) * PAIR].T * HEAD_DIM ** -0.5
        qi_t += [jnp.where(first_sub, pair_t, 0.0).astype(BF16), jnp.where(first_sub, 0.0, pair_t).astype(BF16)]

    def score_tile(j, carry):
        off = pl.multiple_of(j * kt, kt)
        kn = kn_ref[0, pl.ds(off, kt), :]
        sc = jnp.zeros((kt, qb), F32)
        for h in heads:
            sc = sc + wit[h:h + 1, :] * jnp.maximum(_dot(kn, qi_t[h]), 0.0)
        sc = sc * n_heads ** -0.5
        key = jnp.where(sc == 0.0, 0, _f32_to_key(sc))
        skey_ref[j] = jnp.where(off + row0 <= q_pos, key, INT_MIN)
        return carry

    lax.fori_loop(0, nt, score_tile, 0)

    def fold(tile_fn, init, combine):
        def body(j, acc):
            part = tile_fn(skey_ref[j], j * kt + row0).reshape(kt // 8, 8, qb)
            return combine(acc, _tree_reduce(combine, [part[i] for i in range(kt // 8)]))
        return lax.fori_loop(0, nt, body, jnp.full((8, qb), init, F32))

    def count(pred):
        acc = fold(lambda sk, idx: jnp.where(pred(sk, idx), 1.0, 0.0), 0.0, jnp.add)
        return jnp.sum(acc, axis=0, keepdims=True)

    few = q_pos + 1 <= topk
    v_max = jnp.max(fold(lambda sk, _: jnp.where(sk == INT_MIN, -jnp.inf, _key_to_f32(sk)), -jnp.inf, jnp.maximum),
                    axis=0, keepdims=True)
    v_min = jnp.min(fold(lambda sk, _: jnp.where(sk == INT_MIN, jnp.inf, _key_to_f32(sk)), jnp.inf, jnp.minimum),
                    axis=0, keepdims=True)
    lo0 = _f32_to_key(v_min)
    hi0 = _f32_to_key(v_max) + 1
    cnt0 = (q_pos + 1).astype(F32)

    def unresolved(lo, hi, cnt_lo):
        return ~(few | (cnt_lo == topk) | (hi <= lo + 1))

    def any_open(lo, hi, cnt_lo):
        return jnp.max(jnp.where(unresolved(lo, hi, cnt_lo), 1.0, 0.0))

    def search_cond(st):
        return (st[0] < SEARCH_VALUE_STEPS + 34) & (st[4] > 0.0)

    def search_step(st):
        it, lo, hi, cnt_lo, _ = st
        for _ in range(SEARCH_STEPS_PER_CHECK):
            mid_v = _f32_to_key(0.5 * _key_to_f32(lo) + 0.5 * _key_to_f32(hi))
            mid_k = (lo >> 1) + (hi >> 1) + (lo & hi & 1)
            mid = jnp.where((it < SEARCH_VALUE_STEPS) & (mid_v > lo) & (mid_v < hi), mid_v, mid_k)
            cnt = count(lambda sk, _: sk >= mid)
            open_ = unresolved(lo, hi, cnt_lo)
            up = open_ & (cnt >= topk)
            lo, cnt_lo = jnp.where(up, mid, lo), jnp.where(up, cnt, cnt_lo)
            hi = jnp.where(open_ & ~up, mid, hi)
            it = it + 1
        return it, lo, hi, cnt_lo, any_open(lo, hi, cnt_lo)

    st0 = (jnp.int32(0), lo0, hi0, cnt0, any_open(lo0, hi0, cnt0))
    _, lo, _, cnt_lo, _ = lax.while_loop(search_cond, search_step, st0)
    thr = jnp.where(few, INT_MIN, lo)
    tie = (cnt_lo > topk) & ~few
    any_tie = jnp.max(jnp.where(tie, 1, 0))
    n_gt = lax.fori_loop(0, any_tie, lambda i, c: count(lambda sk, _: sk > thr), jnp.zeros((1, qb), F32))
    need = topk - n_gt

    def idx_bit(i, prefix):
        cand = prefix | jnp.left_shift(jnp.int32(1), idx_bits - 1 - i)
        cnt = count(lambda sk, kidx: (sk == thr) & (kidx < cand))
        return jnp.where(cnt < need, cand, prefix)

    jmax = lax.fori_loop(0, idx_bits * any_tie, idx_bit, jnp.zeros((1, qb), I32))
    jmax = jnp.where(tie, jmax, 0x7FFFFFFF)

    q = q_ref[0]
    for h in heads:
        pr = slice((h // 2) * PAIR, (h // 2 + 1) * PAIR)
        wk_h = jnp.where(first if h % 2 == 0 else ~first, wk_ref[:, pr], 0.0)
        qpt_ref[h] = (_dot_nt(wk_h, q[:, pr].astype(BF16)) * (HEAD_DIM ** -0.5 * math.log2(math.e))).astype(BF16)
    m_ref[...] = jnp.full_like(m_ref, NEG_BIG)
    l_ref[...] = jnp.zeros_like(l_ref)
    acc_ref[...] = jnp.zeros_like(acc_ref)

    def attn_tile(j, carry):
        sk_all = skey_ref[j]
        for part in range(kt // ka):
            off = pl.multiple_of(j * kt + part * ka, ka)
            sk = sk_all[part * ka:(part + 1) * ka, :]
            kidx = off + row0[:ka]
            sel = ((sk > thr) | ((sk == thr) & (kidx <= jmax))) & (kidx <= q_pos)
            bias = jnp.where(sel, 0.0, NEG_BIG)
            cn = cn_ref[0, pl.ds(off, ka), :]
            cn_t = cnt_ref[0, j * (kt // ka) + part]
            s = [_dot(cn, qpt_ref[h]) + bias for h in heads]
            m_old = [m_ref[h] for h in heads]
            m_new = [jnp.maximum(m_old[h], jnp.max(s[h], axis=0, keepdims=True)) for h in heads]
            pexp = [jnp.exp2(s[h] - m_new[h]) for h in heads]
            alpha = [jnp.exp2(m_old[h] - m_new[h]) for h in heads]
            pv = [_dot(cn_t, pexp[h].astype(BF16)) for h in heads]
            for h in heads:
                l_ref[h] = alpha[h] * l_ref[h] + jnp.sum(pexp[h], axis=0, keepdims=True)
                acc_ref[h] = alpha[h] * acc_ref[h] + pv[h]
                m_ref[h] = m_new[h]
        return carry

    lax.fori_loop(0, nt, attn_tile, 0)

    rank = cn_ref.shape[-1]
    for p in range(n_heads // 2):
        pr = slice(p * PAIR, (p + 1) * PAIR)
        out = jnp.zeros((qb, PAIR), F32)
        for h in (2 * p, 2 * p + 1):
            lat_t = (acc_ref[h] / l_ref[h]).astype(BF16)
            wv_h = jnp.where(first if h % 2 == 0 else ~first, wv_ref[:, pr], 0.0)
            out = out + lax.dot_general(lat_t, wv_h, _TN, preferred_element_type=F32)
        o_ref[0, :, pr] = out.astype(o_ref.dtype)


def dsa_attention(q_idx, w_idx, q_b, kn, cn, wk_t, wv, qb=128, kt=512, ka=256):
    b, t, width = q_b.shape
    n_heads = width // HEAD_DIM
    rank = cn.shape[-1]
    kt, ka = min(kt, t), min(ka, t)
    topk = min(IDX_TOPK_MAX, t // 4)
    w_t = jnp.swapaxes(w_idx[:, :, :n_heads], 1, 2)
    cn_t = jnp.swapaxes(cn.reshape(b, t // ka, ka, rank), 2, 3)
    qtile = lambda c: pl.BlockSpec((1, qb, c), lambda i, j: (i, j, 0))
    full = lambda c: pl.BlockSpec((1, t, c), lambda i, j: (i, 0, 0))
    const = lambda a: pl.BlockSpec(a.shape, lambda i, j: (0, 0))
    kern = functools.partial(_dsa_kernel, qb=qb, kt=kt, ka=ka, topk=topk, n_heads=n_heads,
                             idx_bits=max(1, (t - 1).bit_length()))
    return pl.pallas_call(
        kern, grid=(b, t // qb),
        in_specs=[qtile(width), pl.BlockSpec((1, n_heads, qb), lambda i, j: (i, 0, j)), qtile(width),
                  full(LANES), full(rank), pl.BlockSpec((1, t // ka, rank, ka), lambda i, j: (i, 0, 0, 0)),
                  const(wk_t), const(wv)],
        out_specs=qtile(width), out_shape=jax.ShapeDtypeStruct((b, t, width), BF16),
        scratch_shapes=[pltpu.VMEM((t // kt, kt, qb), I32), pltpu.VMEM((n_heads, rank, qb), BF16),
                        pltpu.VMEM((n_heads, rank, qb), F32), pltpu.VMEM((n_heads, 1, qb), F32),
                        pltpu.VMEM((n_heads, 1, qb), F32)],
        compiler_params=_cparams(("parallel", "arbitrary")), name="dsa_attention",
    )(q_idx, w_t, q_b, kn, cn, cn_t, wk_t, wv)


def _even_out_kernel(x_ref, y_ref, bonus_ref, gate_ref, yb_ref, lng_ref, lnb_ref, seg_ref, wa_ref, wb_ref, o_ref):
    seg = seg_ref[...]
    y = y_ref[...]
    yc = y - _dot_hi(y, seg) * (1.0 / HEAD_DIM)
    yn = yc * lax.rsqrt(_dot_hi(yc * yc, seg) * (1.0 / HEAD_DIM) + GN_EPS)
    ya = (yn * lng_ref[...] + lnb_ref[...] + bonus_ref[...]) * gate_ref[...]
    o_ref[...] = x_ref[...] + _dot(ya.astype(BF16), wa_ref[...]) + _dot(yb_ref[...], wb_ref[...])


def even_out(x2d, y, bonus, gate, y_b, ln_g, ln_b, w_out, tm=256):
    n, d = x2d.shape
    aw = y.shape[1]
    tile = lambda c: pl.BlockSpec((tm, c), lambda i: (i, 0))
    const = lambda a: pl.BlockSpec(a.shape, lambda i: (0, 0))
    consts = [ln_g.reshape(1, aw), ln_b.reshape(1, aw), _seg_ones(aw),
              w_out[:aw].astype(BF16), w_out[aw:].astype(BF16)]
    return pl.pallas_call(
        _even_out_kernel, grid=(n // tm,),
        in_specs=[tile(d), tile(aw), tile(aw), tile(aw), tile(y_b.shape[1])] + [const(a) for a in consts],
        out_specs=tile(d), out_shape=jax.ShapeDtypeStruct((n, d), F32),
        compiler_params=_cparams(("parallel",)), name="even_out",
    )(x2d, y, bonus, gate, y_b, *consts)


SB_PAIRS = 2


def _stickbreak_kernel(q_ref, k_ref, vt_ref, o_ref, acc_ref, carry_ref, *, tq, n_pairs):
    qi = pl.program_id(2)
    first_sub = lax.broadcasted_iota(I32, (PAIR, 1), 0) < HEAD_DIM
    key_i = lax.broadcasted_iota(I32, (tq, tq), 0)
    qry_i = lax.broadcasted_iota(I32, (tq, tq), 1)
    before = key_i < qry_i
    later = (qry_i > key_i).astype(BF16)
    heads = [(p, x) for p in range(n_pairs) for x in range(2)]
    q_h = {}
    for p in range(n_pairs):
        q_t = q_ref[0, :, p * PAIR:(p + 1) * PAIR].astype(F32).T * HEAD_DIM ** -0.5
        q_h[p, 0] = jnp.where(first_sub, q_t, 0.0).astype(BF16)
        q_h[p, 1] = jnp.where(first_sub, 0.0, q_t).astype(BF16)
    acc_ref[...] = jnp.zeros_like(acc_ref)
    carry_ref[...] = jnp.zeros_like(carry_ref)

    def tile(j, diag):
        off = pl.multiple_of(j * tq, tq)
        k2 = [k_ref[0, pl.ds(off, tq), p * PAIR:(p + 1) * PAIR] for p in range(n_pairs)]
        z = {h: _dot(k2[h[0]], q_h[h]) * math.log2(math.e) for h in heads}
        lk = {h: -(jnp.maximum(z[h], 0.0) + jnp.log2(1.0 + jnp.exp2(-jnp.abs(z[h])))) for h in heads}
        if diag:
            lk = {h: jnp.where(before, lk[h], 0.0) for h in heads}
        hi = {h: lk[h].astype(BF16) for h in heads}
        lo = {h: (lk[h] - hi[h].astype(F32)).astype(BF16) for h in heads}
        tail = {h: _dot(later, hi[h]) + _dot(later, lo[h]) + carry_ref[h[0], h[1]] for h in heads}
        w = {h: jnp.exp2(z[h] + lk[h] + tail[h]) for h in heads}
        if diag:
            w = {h: jnp.where(before, w[h], 0.0) for h in heads}
        for p in range(n_pairs):
            v_t = vt_ref[0, p, j]
            acc_ref[p] += (_dot(jnp.where(first_sub, v_t, 0.0), w[p, 0].astype(BF16))
                           + _dot(jnp.where(first_sub, 0.0, v_t), w[p, 1].astype(BF16)))
        for h in heads:
            carry_ref[h[0], h[1]] += jnp.sum(lk[h], axis=0, keepdims=True)
        return (jnp.max(carry_ref[...]) > -110.0 * math.log2(math.e)).astype(I32)

    live = tile(qi, True)

    def cond(st):
        return (st[0] >= 0) & (st[1] > 0)

    def body(st):
        return st[0] - 1, tile(st[0], False)

    lax.while_loop(cond, body, (qi - 1, live))
    for p in range(n_pairs):
        o_ref[0, :, p * PAIR:(p + 1) * PAIR] = acc_ref[p].T.astype(o_ref.dtype)


def stickbreak_attention(qkv, n_heads, tq=256):
    b, t, _ = qkv.shape
    n_pair = n_heads // 2
    tq = min(tq, t)
    width = n_heads * HEAD_DIM
    gw = SB_PAIRS * PAIR
    n_grp = n_pair // SB_PAIRS
    v_t = jnp.transpose(qkv[:, :, 2 * width:].reshape(b, t // tq, tq, n_pair, PAIR), (0, 3, 1, 4, 2))
    q_spec = pl.BlockSpec((1, tq, gw), lambda i, p, j: (i, j, p))
    k_spec = pl.BlockSpec((1, t, gw), lambda i, p, j: (i, 0, n_grp + p))
    v_spec = pl.BlockSpec((1, SB_PAIRS, t // tq, PAIR, tq), lambda i, p, j: (i, p, 0, 0, 0))
    return pl.pallas_call(
        functools.partial(_stickbreak_kernel, tq=tq, n_pairs=SB_PAIRS), grid=(b, n_grp, t // tq),
        in_specs=[q_spec, k_spec, v_spec], out_specs=q_spec,
        out_shape=jax.ShapeDtypeStruct((b, t, width), BF16),
        scratch_shapes=[pltpu.VMEM((SB_PAIRS, PAIR, tq), F32), pltpu.VMEM((SB_PAIRS, 2, 1, tq), F32)],
        compiler_params=_cparams(("parallel", "parallel", "arbitrary")), name="stickbreak",
    )(qkv, qkv, v_t)


def _proj_residual_kernel(x_ref, y_ref, w_ref, o_ref):
    o_ref[...] = x_ref[...] + _dot(y_ref[...], w_ref[...])


def proj_residual(x2d, y, w, tm=256):
    n, d = x2d.shape
    c = y.shape[1]
    return pl.pallas_call(
        _proj_residual_kernel, grid=(n // tm,),
        in_specs=[pl.BlockSpec((tm, d), lambda i: (i, 0)), pl.BlockSpec((tm, c), lambda i: (i, 0)),
                  pl.BlockSpec(w.shape, lambda i: (0, 0))],
        out_specs=pl.BlockSpec((tm, d), lambda i: (i, 0)), out_shape=jax.ShapeDtypeStruct((n, d), F32),
        compiler_params=_cparams(("parallel",)), name="proj_residual",
    )(x2d, y, w)


def _router_kernel(x_ref, g_ref, wr_ref, h_ref, ids_ref, gates_ref):
    h = _rms(x_ref[...], g_ref[...])
    h_ref[...] = h.astype(h_ref.dtype)
    logits = _dot_hi(h, wr_ref[...])
    lane = lax.broadcasted_iota(I32, logits.shape, 1)
    lane_f = lane.astype(F32)

    def top1(vals):
        mx = jnp.max(vals, axis=1, keepdims=True)
        return mx, jnp.min(jnp.where(vals == mx, lane_f, 1e9), axis=1, keepdims=True).astype(I32)

    is_g = lane < N_GROUPS
    g_max, g_sel = top1(jnp.where(is_g, logits, -jnp.inf))
    g_gate = 1.0 / jnp.sum(jnp.where(is_g, jnp.exp(logits - g_max), 0.0), axis=1, keepdims=True)
    e_lane = lane - N_GROUPS
    in_grp = (e_lane >= g_sel * EXPERTS_PER_GROUP) & (e_lane < (g_sel + 1) * EXPERTS_PER_GROUP)
    el = jnp.where(in_grp, logits, -jnp.inf)
    m1, i1 = top1(el)
    m2, i2 = top1(jnp.where(lane == i1, -jnp.inf, el))
    e21 = jnp.exp(m2 - m1)
    p1 = 1.0 / (1.0 + e21)
    ids_ref[...] = jnp.where(lane == 0, i1 - N_GROUPS, jnp.where(lane == 1, i2 - N_GROUPS, 0))
    gates_ref[...] = jnp.where(lane == 0, g_gate * p1, jnp.where(lane == 1, g_gate * (e21 * p1), 0.0))


def moe_router(x2d, g, w_router, tm=256):
    n, d = x2d.shape
    tile = lambda c: pl.BlockSpec((tm, c), lambda i: (i, 0))
    return pl.pallas_call(
        _router_kernel, grid=(n // tm,),
        in_specs=[tile(d), pl.BlockSpec((1, d), lambda i: (0, 0)), pl.BlockSpec(w_router.shape, lambda i: (0, 0))],
        out_specs=[tile(d), tile(LANES), tile(LANES)],
        out_shape=[jax.ShapeDtypeStruct((n, d), BF16), jax.ShapeDtypeStruct((n, LANES), I32),
                   jax.ShapeDtypeStruct((n, LANES), F32)],
        compiler_params=_cparams(("parallel",)), name="moe_router",
    )(x2d, g.reshape(1, d), w_router)


MOE_TT = 2048
MOE_CH = 128
MOE_KC = 512
_PLAN_BLK = 256


def _moe_plan_kernel(ids_ref, idst_ref, rank_ref, dest_ref, meta_ref, *, tt):
    nb = tt // _PLAN_BLK
    row = lax.broadcasted_iota(I32, (_PLAN_BLK, _PLAN_BLK), 0)
    col = lax.broadcasted_iota(I32, (_PLAN_BLK, _PLAN_BLK), 1)
    earlier_t = (row < col).astype(BF16)
    earlier = (col < row).astype(BF16)
    e_sub = lax.broadcasted_iota(I32, (N_EXPERTS, _PLAN_BLK), 0)
    carry_t = jnp.zeros((N_EXPERTS, 1), F32)
    for b in range(nb):
        sl = slice(b * _PLAN_BLK, (b + 1) * _PLAN_BLK)
        member = (e_sub == idst_ref[0:1, sl]) | (e_sub == idst_ref[1:2, sl])
        m_t = jnp.where(member, 1.0, 0.0)
        before = _dot(m_t.astype(BF16), earlier_t) + carry_t
        rank_ref[0, :, sl] = jnp.where(member, before, -1.0)
        carry_t = carry_t + jnp.sum(m_t, axis=1, keepdims=True)
    lane = lax.broadcasted_iota(I32, (_PLAN_BLK, LANES), 1)
    carry = jnp.zeros((1, LANES), F32)
    prefix = []
    for b in range(nb):
        sl = slice(b * _PLAN_BLK, (b + 1) * _PLAN_BLK)
        m = jnp.where((lane == ids_ref[sl, 0:1]) | (lane == ids_ref[sl, 1:2]), 1.0, 0.0)
        prefix.append(_dot(earlier, m.astype(BF16)) + carry)
        carry = carry + jnp.sum(m, axis=0, keepdims=True)
    counts = carry
    padded = jnp.ceil(counts * (1.0 / MOE_CH)) * MOE_CH
    l_row = lax.broadcasted_iota(I32, (LANES, LANES), 0)
    l_col = lax.broadcasted_iota(I32, (LANES, LANES), 1)
    offs = _dot_hi(padded, (l_row < l_col).astype(F32))
    for b in range(nb):
        sl = slice(b * _PLAN_BLK, (b + 1) * _PLAN_BLK)
        where_row = prefix[b] + offs
        d0 = jnp.sum(jnp.where(lane == ids_ref[sl, 0:1], where_row, 0.0), axis=1, keepdims=True)
        d1 = jnp.sum(jnp.where(lane == ids_ref[sl, 1:2], where_row, 0.0), axis=1, keepdims=True)
        dest_ref[sl, :] = jnp.where(lane == 0, d0, jnp.where(lane == 1, d1, 0.0)).astype(I32)
    r_used = jnp.sum(padded, axis=1, keepdims=True)
    sub8 = lax.broadcasted_iota(I32, (8, LANES), 0)
    meta_ref[0] = jnp.where(sub8 == 0, counts, jnp.where(sub8 == 1, offs, jnp.where(sub8 == 2, r_used, 0.0))).astype(I32)


def moe_plan(ids, ids_t, tt):
    n = ids.shape[0]
    n_tiles = n // tt
    return pl.pallas_call(
        functools.partial(_moe_plan_kernel, tt=tt), grid=(n_tiles,),
        in_specs=[pl.BlockSpec((tt, LANES), lambda i: (i, 0)), pl.BlockSpec((8, tt), lambda i: (0, i))],
        out_specs=[pl.BlockSpec((1, N_EXPERTS, tt), lambda i: (i, 0, 0)), pl.BlockSpec((tt, LANES), lambda i: (i, 0)),
                   pl.BlockSpec((1, 8, LANES), lambda i: (i, 0, 0))],
        out_shape=[jax.ShapeDtypeStruct((n_tiles, N_EXPERTS, tt), F32), jax.ShapeDtypeStruct((n, LANES), I32),
                   jax.ShapeDtypeStruct((n_tiles, 8, LANES), I32)],
        compiler_params=_cparams(("parallel",)), name="moe_plan",
    )(ids, ids_t)


def _moe_ffn_kernel(cnt_ref, off_ref, h_ref, rank_ref, idst_ref, gt_ref, w1_ref, w3_ref, w2_ref, y_ref, *, tt):
    t, e = pl.program_id(0), pl.program_id(1)

    @pl.when(e == 0)
    def _():
        y_ref[...] = jnp.zeros_like(y_ref)

    cnt = cnt_ref[t * N_EXPERTS + e]
    off = off_ref[t * N_EXPERTS + e]
    rank_row = rank_ref[0, pl.ds(e, 1), :]
    gate_row = jnp.where(idst_ref[0:1, :] == e, gt_ref[0:1, :], gt_ref[1:2, :])
    sub = lax.broadcasted_iota(I32, (MOE_CH, 1), 0)

    def chunk(c, carry):
        r0 = c * MOE_CH
        pick = rank_row == (r0 + sub).astype(F32)
        x = _dot(jnp.where(pick, 1.0, 0.0).astype(BF16), h_ref[...]).astype(BF16)
        a = _dot(x, w1_ref[0])
        hdn = (a * jax.nn.sigmoid(a)) * _dot(x, w3_ref[0])
        y = _dot(hdn.astype(BF16), w2_ref[0])
        gate = jnp.sum(jnp.where(pick, gate_row, 0.0), axis=1, keepdims=True)
        y_ref[0, pl.ds(pl.multiple_of(off + r0, MOE_CH), MOE_CH), :] = (y * gate).astype(y_ref.dtype)
        return carry

    lax.fori_loop(0, (cnt + MOE_CH - 1) // MOE_CH, chunk, 0)


def moe_ffn(h, rank_t, ids_t, gates_t, counts, offs, w1, w3, w2, tt):
    n, d = h.shape
    ff = w1.shape[-1]
    n_tiles = n // tt
    r_max = TOP_K_EXPERTS * tt + N_EXPERTS * MOE_CH
    grid_spec = pltpu.PrefetchScalarGridSpec(
        num_scalar_prefetch=2, grid=(n_tiles, N_EXPERTS),
        in_specs=[pl.BlockSpec((tt, d), lambda i, e, c, o: (i, 0)),
                  pl.BlockSpec((1, N_EXPERTS, tt), lambda i, e, c, o: (i, 0, 0)),
                  pl.BlockSpec((8, tt), lambda i, e, c, o: (0, i)),
                  pl.BlockSpec((8, tt), lambda i, e, c, o: (0, i)),
                  pl.BlockSpec((1, d, ff), lambda i, e, c, o: (e, 0, 0)),
                  pl.BlockSpec((1, d, ff), lambda i, e, c, o: (e, 0, 0)),
                  pl.BlockSpec((1, ff, d), lambda i, e, c, o: (e, 0, 0))],
        out_specs=pl.BlockSpec((1, r_max, d), lambda i, e, c, o: (i, 0, 0)))
    return pl.pallas_call(
        functools.partial(_moe_ffn_kernel, tt=tt), grid_spec=grid_spec,
        out_shape=jax.ShapeDtypeStruct((n_tiles, r_max, d), BF16),
        compiler_params=_cparams(("arbitrary", "arbitrary")), name="moe_ffn",
    )(counts, offs, h, rank_t, ids_t, gates_t, w1, w3, w2)


def _moe_combine_kernel(used_ref, x_ref, dest_ref, y_ref, o_ref):
    t = pl.program_id(0)
    d0, d1 = dest_ref[:, 0:1], dest_ref[:, 1:2]
    col = lax.broadcasted_iota(I32, (1, MOE_KC), 1)
    o_ref[...] = x_ref[...]

    def step(k, carry):
        r0 = pl.multiple_of(k * MOE_KC, MOE_KC)
        rows = r0 + col
        pick = jnp.where((rows == d0) | (rows == d1), 1.0, 0.0).astype(BF16)
        o_ref[...] += _dot(pick, y_ref[0, pl.ds(r0, MOE_KC), :])
        return carry

    lax.fori_loop(0, (used_ref[t] + MOE_KC - 1) // MOE_KC, step, 0)


def moe_combine(x2d, dest, y_all, r_used, tt, tm=256):
    n, d = x2d.shape
    r_max = y_all.shape[1]
    per = tt // tm
    grid_spec = pltpu.PrefetchScalarGridSpec(
        num_scalar_prefetch=1, grid=(n // tt, per),
        in_specs=[pl.BlockSpec((tm, d), lambda i, j, u: (i * per + j, 0)),
                  pl.BlockSpec((tm, LANES), lambda i, j, u: (i * per + j, 0)),
                  pl.BlockSpec((1, r_max, d), lambda i, j, u: (i, 0, 0))],
        out_specs=pl.BlockSpec((tm, d), lambda i, j, u: (i * per + j, 0)))
    return pl.pallas_call(
        _moe_combine_kernel, grid_spec=grid_spec, out_shape=jax.ShapeDtypeStruct((n, d), F32),
        compiler_params=_cparams(("arbitrary", "arbitrary")), name="moe_combine",
    )(r_used, x2d, dest, y_all)


def _final_norm_kernel(x_ref, g_ref, o_ref):
    o_ref[...] = _rms(x_ref[...], g_ref[...])


def final_norm(x2d, g, tm=512):
    n, d = x2d.shape
    return pl.pallas_call(
        _final_norm_kernel, grid=(n // tm,),
        in_specs=[pl.BlockSpec((tm, d), lambda i: (i, 0)), pl.BlockSpec((1, d), lambda i: (0, 0))],
        out_specs=pl.BlockSpec((tm, d), lambda i: (i, 0)), out_shape=jax.ShapeDtypeStruct((n, d), F32),
        compiler_params=_cparams(("parallel",)), name="final_norm",
    )(x2d, g.reshape(1, d))


def moe_layer(x2d, norm_g, router_group, router_expert, w1, w3, w2):
    n, d = x2d.shape
    tt = min(MOE_TT, n)
    w_router = jnp.concatenate([router_group, jnp.moveaxis(router_expert, 0, 1).reshape(d, N_EXPERTS)], axis=1)
    h, ids, gates = moe_router(x2d, norm_g, _pad_cols(w_router, LANES))
    ids_t = jnp.pad(ids[:, :TOP_K_EXPERTS].T, ((0, 8 - TOP_K_EXPERTS), (0, 0)))
    gates_t = jnp.pad(gates[:, :TOP_K_EXPERTS].T, ((0, 8 - TOP_K_EXPERTS), (0, 0)))
    rank_t, dest, meta = moe_plan(ids, ids_t, tt)
    counts = meta[:, 0, :N_EXPERTS].reshape(-1)
    offs = meta[:, 1, :N_EXPERTS].reshape(-1)
    y_all = moe_ffn(h, rank_t, ids_t, gates_t, counts, offs, w1.astype(BF16), w3.astype(BF16), w2.astype(BF16), tt)
    return moe_combine(x2d, dest, y_all, meta[:, 2, 0], tt)


def even_layer(x, v_first, p):
    b, t, d = x.shape
    n = b * t
    w_in = p['w_in']
    aw = p['k_k'].shape[0]
    bw = p['w_kv_up'].shape[1] // 2
    rank = p['w_kv_up'].shape[0]
    n_idx = w_in.shape[1] - (3 * aw + bw + rank + bw + HEAD_DIM)
    o = 0
    segs = []
    for width in (3 * aw, bw, rank, bw, HEAD_DIM, n_idx):
        segs.append(w_in[:, o:o + width])
        o += width
    segs[4] = jnp.concatenate([segs[4], segs[4]], axis=1)
    segs[5] = _pad_cols(segs[5], LANES)
    rkv, q_b, c_kv, q_idx, k_dup, w_idx = norm_proj(
        x.reshape(n, d), p['norm'], [s.astype(BF16) for s in segs], [F32] * 6)

    r, lw, k_mod, v, kk, iclr, gate, bonus = rwkv_prep(x, rkv.reshape(b, t, 3 * aw), v_first, p)
    y = rwkv_scan(r, lw, k_mod, v, kk, iclr)

    cn, kn = kv_prep(c_kv, k_dup, p['kv_norm'], jnp.concatenate([p['idx_k_norm']] * 2))
    wk_t = p['w_kv_up'][:, :bw].astype(BF16)
    wv = p['w_kv_up'][:, bw:].astype(BF16)
    y_b = dsa_attention(q_idx.reshape(b, t, bw), w_idx.reshape(b, t, LANES), q_b.reshape(b, t, bw),
                        kn.reshape(b, t, LANES), cn.reshape(b, t, rank), wk_t, wv)

    x_new = even_out(x.reshape(n, d), y.reshape(n, aw), bonus.reshape(n, aw), gate.reshape(n, aw),
                     y_b.reshape(n, bw), p['ln_g'], p['ln_b'], p['w_out'])
    return x_new.reshape(b, t, d), v


def odd_layer(x, norm_g, w_qkv, w_out):
    b, t, d = x.shape
    n = b * t
    cw = w_qkv.shape[1] // 3
    (qkv,) = norm_proj(x.reshape(n, d), norm_g, [w_qkv.astype(BF16)], [BF16])
    y_c = stickbreak_attention(qkv.reshape(b, t, 3 * cw), cw // HEAD_DIM)
    return proj_residual(x.reshape(n, d), y_c.reshape(n, cw), w_out.astype(BF16)).reshape(b, t, d)


@jax.jit
def _forward(x, ev_norm, ev_w_in, ev_w_out, a_mu_rkv, a_mu_lora, a_w0, a_w1, a_w2, a_a0, a_a1, a_a2, a_g1,
             a_g2, a_k_k, a_k_a, a_r_k, a_ln_g, a_ln_b, a_mu_vres, a_v0, a_v1, a_v2, b_kv_norm, b_w_kv_up,
             b_idx_k_norm, od_norm, od_w_qkv, od_w_out, moe_norm, moe_router_group, moe_router_expert,
             moe_w1, moe_w3, moe_w2, final_norm_g):
    b, t, d = x.shape
    depth = moe_norm.shape[0]
    v_first = None
    for i in range(depth):
        if i % 2 == 0:
            e = i // 2
            p = dict(norm=ev_norm[e], w_in=ev_w_in[e], w_out=ev_w_out[e], mu_rkv=a_mu_rkv[e], mu_lora=a_mu_lora[e],
                     w0=a_w0[e], w1=a_w1[e], w2=a_w2[e], a0=a_a0[e], a1=a_a1[e], a2=a_a2[e], g1=a_g1[e], g2=a_g2[e],
                     k_k=a_k_k[e], k_a=a_k_a[e], r_k=a_r_k[e].reshape(-1), ln_g=a_ln_g[e], ln_b=a_ln_b[e],
                     kv_norm=b_kv_norm[e], w_kv_up=b_w_kv_up[e], idx_k_norm=b_idx_k_norm[e])
            if e > 0:
                p.update(mu_v=a_mu_vres[e - 1], v0=a_v0[e - 1], v1=a_v1[e - 1], v2=a_v2[e - 1])
            x, v_used = even_layer(x, v_first if e > 0 else None, p)
            if e == 0:
                v_first = v_used
        else:
            o = i // 2
            x = odd_layer(x, od_norm[o], od_w_qkv[o], od_w_out[o])
        x = moe_layer(x.reshape(b * t, d), moe_norm[i], moe_router_group[i], moe_router_expert[i],
                      moe_w1[i], moe_w3[i], moe_w2[i]).reshape(b, t, d)
    return final_norm(x.reshape(b * t, d), final_norm_g).reshape(b, t, d)


def kernel(x, ev_norm, ev_w_in, ev_w_out, a_mu_rkv, a_mu_lora, a_w0, a_w1, a_w2, a_a0, a_a1, a_a2, a_g1, a_g2, a_k_k, a_k_a, a_r_k, a_ln_g, a_ln_b, a_mu_vres, a_v0, a_v1, a_v2, b_kv_norm, b_w_kv_up, b_idx_k_norm, od_norm, od_w_qkv, od_w_out, moe_norm, moe_router_group, moe_router_expert, moe_w1, moe_w3, moe_w2, final_norm):
    return _forward(x, ev_norm, ev_w_in, ev_w_out, a_mu_rkv, a_mu_lora, a_w0, a_w1, a_w2, a_a0, a_a1, a_a2, a_g1,
                    a_g2, a_k_k, a_k_a, a_r_k, a_ln_g, a_ln_b, a_mu_vres, a_v0, a_v1, a_v2, b_kv_norm, b_w_kv_up,
                    b_idx_k_norm, od_norm, od_w_qkv, od_w_out, moe_norm, moe_router_group, moe_router_expert,
                    moe_w1, moe_w3, moe_w2, final_norm)
```

```python
import functools
import math

import jax
import jax.numpy as jnp
from jax import lax
from jax.experimental import pallas as pl
from jax.experimental.pallas import tpu as pltpu

F32 = jnp.float32
BF16 = jnp.bfloat16
I32 = jnp.int32
HI = lax.Precision.HIGHEST

LANES = 128
HEAD_DIM = 64
PAIR = 2 * HEAD_DIM
RMS_EPS = 1e-6
GN_EPS = 64e-5
N_GROUPS = 4
EXPERTS_PER_GROUP = 8
N_EXPERTS = N_GROUPS * EXPERTS_PER_GROUP
TOP_K_EXPERTS = 2
IDX_TOPK_MAX = 256
INT_MIN = -2147483648
NEG_BIG = -1e30
VMEM_LIMIT = 56 * 1024 * 1024

_NT = (((1,), (1,)), ((), ()))
_TN = (((0,), (0,)), ((), ()))


def _cparams(sem):
    return pltpu.CompilerParams(dimension_semantics=sem, vmem_limit_bytes=VMEM_LIMIT)


def _rms(x, g):
    return x * lax.rsqrt(jnp.mean(x * x, axis=-1, keepdims=True) + RMS_EPS) * g


def _dot(a, b):
    return jnp.dot(a, b, preferred_element_type=F32)


def _dot_hi(a, b):
    return jnp.dot(a, b, preferred_element_type=F32, precision=HI)


def _dot_nt(a, b):
    return lax.dot_general(a, b, _NT, preferred_element_type=F32)


def _lane_is_first_head(width=PAIR):
    return lax.broadcasted_iota(I32, (1, width), 1) % PAIR < HEAD_DIM


def _norm_proj_kernel(x_ref, g_ref, *refs, n_seg):
    w_refs, o_refs = refs[:n_seg], refs[n_seg:]
    h = _rms(x_ref[...], g_ref[...]).astype(BF16)
    for w_ref, o_ref in zip(w_refs, o_refs):
        o_ref[...] = _dot(h, w_ref[...]).astype(o_ref.dtype)


def norm_proj(x2d, g, ws, out_dtypes, tm=256):
    n, d = x2d.shape
    in_specs = [pl.BlockSpec((tm, d), lambda i: (i, 0)), pl.BlockSpec((1, d), lambda i: (0, 0))]
    in_specs += [pl.BlockSpec(w.shape, lambda i: (0, 0)) for w in ws]
    out_specs = [pl.BlockSpec((tm, w.shape[1]), lambda i: (i, 0)) for w in ws]
    out_shape = [jax.ShapeDtypeStruct((n, w.shape[1]), dt) for w, dt in zip(ws, out_dtypes)]
    return pl.pallas_call(
        functools.partial(_norm_proj_kernel, n_seg=len(ws)),
        grid=(n // tm,), in_specs=in_specs, out_specs=out_specs, out_shape=out_shape,
        compiler_params=_cparams(("parallel",)), name="norm_proj",
    )(x2d, g.reshape(1, d), *ws)


def _shift_rows(z, carry_ref):
    tm = z.shape[0]
    first = lax.broadcasted_iota(I32, (tm, 1), 0) == 0
    prev = jnp.where(first, carry_ref[...], pltpu.roll(z, 1, 0))
    carry_ref[...] = z[tm - 1:tm, :]
    return prev


def _rwkv_prep_kernel(*refs, has_vres):
    it = iter(refs)
    x_ref, rkv_ref = next(it), next(it)
    vfirst_ref = next(it) if has_vres else None
    g_ref, mu_rkv_ref, mu_lora_ref = next(it), next(it), next(it)
    w0_ref, w1_ref, w2_ref = next(it), next(it), next(it)
    a0_ref, a1_ref, a2_ref = next(it), next(it), next(it)
    g1_ref, g2_ref = next(it), next(it)
    if has_vres:
        muv_ref, v0_ref, v1_ref, v2_ref = next(it), next(it), next(it), next(it)
    kk_w_ref, ka_ref, rk_ref, seg_ref = next(it), next(it), next(it), next(it)
    r_o, lw_o, k_o, v_o, kk_o, a_o, gate_o, bonus_o = (next(it) for _ in range(8))
    hcarry, rkvcarry = next(it), next(it)

    @pl.when(pl.program_id(1) == 0)
    def _():
        hcarry[...] = jnp.zeros_like(hcarry)
        rkvcarry[...] = jnp.zeros_like(rkvcarry)

    aw = r_o.shape[-1]
    h = _rms(x_ref[0], g_ref[...])
    dh = _shift_rows(h, hcarry) - h
    rkv = rkv_ref[0]
    rkv = rkv + (_shift_rows(rkv, rkvcarry) - rkv) * mu_rkv_ref[...]
    r, k, v = rkv[:, :aw], rkv[:, aw:2 * aw], rkv[:, 2 * aw:]

    def lora_in(row):
        return (h + dh * mu_lora_ref[row:row + 1, :]).astype(BF16)

    dec = w0_ref[...] + _dot(jnp.tanh(_dot(lora_in(0), w1_ref[...])).astype(BF16), w2_ref[...])
    logw = -math.exp(-0.5) * jax.nn.sigmoid(dec)
    iclr = jax.nn.sigmoid(a0_ref[...] + _dot(_dot(lora_in(1), a1_ref[...]).astype(BF16), a2_ref[...]))
    gate = _dot(jax.nn.sigmoid(_dot(lora_in(2), g1_ref[...])).astype(BF16), g2_ref[...])
    if has_vres:
        xv = (h + dh * muv_ref[...]).astype(BF16)
        mix = jax.nn.sigmoid(v0_ref[...] + _dot(_dot(xv, v1_ref[...]).astype(BF16), v2_ref[...]))
        v = v + (vfirst_ref[0] - v) * mix

    seg = seg_ref[...]
    kk = k * kk_w_ref[...]
    kk = kk * lax.rsqrt(jnp.maximum(_dot_hi(kk * kk, seg), 1e-12))
    k_mod = k * (1.0 + (iclr - 1.0) * ka_ref[...])
    bonus = _dot_hi(r * k_mod * rk_ref[...], seg) * v

    r_o[0], lw_o[0], k_o[0], v_o[0] = r, logw, k_mod, v
    kk_o[0], a_o[0], gate_o[0], bonus_o[0] = kk, iclr, gate, bonus


def _pad_cols(w, n):
    return jnp.pad(w, ((0, 0), (0, n - w.shape[1])))


def _pad_rows(w, n):
    return jnp.pad(w, ((0, n - w.shape[0]), (0, 0)))


def _seg_ones(width):
    i = jnp.arange(width) // HEAD_DIM
    return (i[:, None] == i[None, :]).astype(F32)


def rwkv_prep(x, rkv, v_first, p, tm=256):
    b, t, d = x.shape
    aw = rkv.shape[-1] // 3
    has_vres = v_first is not None
    row = lambda a: a.reshape(1, -1)
    const = lambda a: pl.BlockSpec(a.shape, lambda i, j: (0,) * a.ndim)
    tile = lambda c: pl.BlockSpec((1, tm, c), lambda i, j: (i, j, 0))

    args, specs = [x, rkv], [tile(d), tile(3 * aw)]
    if has_vres:
        args.append(v_first)
        specs.append(tile(aw))
    consts = [row(p['norm']), p['mu_rkv'].reshape(1, 3 * aw), p['mu_lora'],
              row(p['w0']), _pad_cols(p['w1'], LANES).astype(BF16), _pad_rows(p['w2'], LANES).astype(BF16),
              row(p['a0']), _pad_cols(p['a1'], LANES).astype(BF16), _pad_rows(p['a2'], LANES).astype(BF16),
              p['g1'].astype(BF16), p['g2'].astype(BF16)]
    if has_vres:
        consts += [row(p['mu_v']), row(p['v0']), _pad_cols(p['v1'], LANES).astype(BF16),
                   _pad_rows(p['v2'], LANES).astype(BF16)]
    consts += [row(p['k_k']), row(p['k_a']), row(p['r_k']), _seg_ones(aw)]
    args += consts
    specs += [const(a) for a in consts]
    out_shape = [jax.ShapeDtypeStruct((b, t, aw), F32)] * 8
    return pl.pallas_call(
        functools.partial(_rwkv_prep_kernel, has_vres=has_vres),
        grid=(b, t // tm), in_specs=specs, out_specs=[tile(aw)] * 8, out_shape=out_shape,
        scratch_shapes=[pltpu.VMEM((1, d), F32), pltpu.VMEM((1, 3 * aw), F32)],
        compiler_params=_cparams(("arbitrary", "arbitrary")), name="rwkv_prep",
    )(*args)


def _split_bf16(x):
    hi = x.astype(BF16)
    return hi, (x - hi.astype(F32)).astype(BF16)


def _mm3(a, b, dims=None):
    a_hi, a_lo = _split_bf16(a)
    b_hi, b_lo = _split_bf16(b)
    if dims is None:
        f = lambda p, q: jnp.dot(p, q, preferred_element_type=F32)
    else:
        f = lambda p, q: lax.dot_general(p, q, dims, preferred_element_type=F32)
    return f(a_hi, b_hi) + f(a_hi, b_lo) + f(a_lo, b_hi)


def _mmc(a, b, dims=None):
    a, b = a.astype(BF16), b.astype(BF16)
    if dims is None:
        return jnp.dot(a, b, preferred_element_type=F32)
    return lax.dot_general(a, b, dims, preferred_element_type=F32)


def _cumsum_rows(x):
    n = x.shape[0]
    row = lax.broadcasted_iota(I32, (n, 1), 0)
    s = 1
    while s < n:
        x = x + jnp.where(row >= s, pltpu.roll(x, s, 0), 0.0)
        s *= 2
    return x


def _rwkv_chunk_kernel(r_ref, lw_ref, k_ref, v_ref, kk_ref, a_ref, qh_ref, y0_ref, m_ref, c0_ref, *, chunk, cps):
    L = chunk
    n_pair = r_ref.shape[-1] // PAIR
    row = lax.broadcasted_iota(I32, (L, L), 0)
    col = lax.broadcasted_iota(I32, (L, L), 1)
    strict = col < row
    incl = col <= row
    eye = (col == row).astype(F32)
    first = _lane_is_first_head()
    p_row = lax.broadcasted_iota(I32, (PAIR, PAIR), 0)
    p_col = lax.broadcasted_iota(I32, (PAIR, PAIR), 1)
    blockdiag = p_row // HEAD_DIM == p_col // HEAD_DIM
    pair_eye = p_row == p_col

    units = [(ci, p) for ci in range(cps) for p in range(n_pair)]
    heads = (first, ~first)
    pre = {}
    for u in units:
        ci, p = u
        rows = slice(ci * L, (ci + 1) * L)
        sl = slice(p * PAIR, (p + 1) * PAIR)
        lw = lw_ref[0, rows, sl]
        c = _cumsum_rows(lw)
        g = jnp.exp(c)
        g_inv = jnp.exp(-c)
        kk = kk_ref[0, rows, sl]
        pre[u] = dict(rows=rows, sl=sl, g_last=g[L - 1:L, :],
                      at=-kk * jnp.exp(c - lw),
                      bt=kk * a_ref[0, rows, sl] * g_inv,
                      kt=k_ref[0, rows, sl] * g_inv,
                      rt=r_ref[0, rows, sl] * g,
                      vv=v_ref[0, rows, sl])
    uh = [(u, h) for u in units for h in range(2)]

    gm = {}
    for u, h in uh:
        d = pre[u]
        lhs = jnp.concatenate([jnp.where(heads[h], d['at'], 0.0), jnp.where(heads[h], d['rt'], 0.0)], axis=0)
        gm[u, h] = _mm3(lhs, jnp.concatenate([d['bt'], d['kt']], axis=0), _NT)
    a_ak = {x: jnp.where(strict, gm[x][:L, L:], 0.0) for x in uh}
    a_rb = {x: jnp.where(incl, gm[x][L:, :L], 0.0) for x in uh}
    a_rk = {x: jnp.where(incl, gm[x][L:, L:], 0.0) for x in uh}
    apow = {x: jnp.where(strict, gm[x][:L, :L], 0.0) for x in uh}
    inv = {x: eye + apow[x] for x in uh}
    for _ in range(max(1, (L - 1).bit_length()) - 1):
        apow = {x: _mmc(apow[x], apow[x]) for x in uh}
        inv = {x: inv[x] + _mmc(inv[x], apow[x]) for x in uh}
    w_h = {(u, h): _mmc(inv[u, h], pre[u]['at']) for u, h in uh}
    akv = {(u, h): _mmc(a_ak[u, h], pre[u]['vv']) for u, h in uh}
    u0_h = {x: _mmc(inv[x], akv[x]) for x in uh}
    w = {u: jnp.where(first, w_h[u, 0], w_h[u, 1]) for u in units}
    u0 = {u: jnp.where(first, u0_h[u, 0], u0_h[u, 1]) for u in units}
    wu = {u: jnp.concatenate([w[u], u0[u]], axis=1) for u in units}
    arb = {(u, h): _mmc(a_rb[u, h], wu[u]) for u, h in uh}
    ark = {(u, h): _mmc(a_rk[u, h], pre[u]['vv']) for u, h in uh}
    first2 = jnp.concatenate([first, first], axis=1)
    for u in units:
        d = pre[u]
        ci, p = u
        arb_w = jnp.where(first2, arb[u, 0], arb[u, 1])
        qh_ref[0, d['rows'], d['sl']] = d['rt'] + arb_w[:, :PAIR]
        y0_ref[0, d['rows'], d['sl']] = arb_w[:, PAIR:] + jnp.where(first, ark[u, 0], ark[u, 1])
        bg = d['bt'] * d['g_last']
        m_kk = _mmc(bg, w[u], _TN)
        m_ref[0, ci, p] = jnp.where(pair_eye, d['g_last'], 0.0) + jnp.where(blockdiag, m_kk, 0.0)
        c0 = _mmc(jnp.concatenate([u0[u], d['vv']], axis=0),
                  jnp.concatenate([bg, d['kt'] * d['g_last']], axis=0), _TN)
        c0_ref[0, ci, p] = jnp.where(blockdiag, c0, 0.0)


def _rwkv_state_kernel(qh_ref, y0_ref, m_ref, c0_ref, y_ref, s_ref, *, chunk, cps):
    L = chunk

    @pl.when(pl.program_id(0) == 0)
    def _():
        s_ref[...] = jnp.zeros_like(s_ref)

    n_batch = qh_ref.shape[0]
    n_pair = qh_ref.shape[-1] // PAIR
    streams = [(b, p) for b in range(n_batch) for p in range(n_pair)]
    s2 = {x: s_ref[x[0], x[1]] for x in streams}
    for ci in range(cps):
        rows = slice(ci * L, (ci + 1) * L)
        for b, p in streams:
            sl = slice(p * PAIR, (p + 1) * PAIR)
            y_ref[b, rows, sl] = _mm3(qh_ref[b, rows, sl], s2[b, p], _NT) + y0_ref[b, rows, sl]
            s2[b, p] = _mm3(s2[b, p], m_ref[b, ci, p], _NT) + c0_ref[b, ci, p]
    for b, p in streams:
        s_ref[b, p] = s2[b, p]


def rwkv_scan(r, lw, k, v, kk, a, chunk=64, cps_a=2, cps_b=4):
    b, t, aw = r.shape
    n_pair = aw // PAIR
    n_chunk = t // chunk
    cps_a, cps_b = min(cps_a, n_chunk), min(cps_b, n_chunk)
    rows = lambda c: pl.BlockSpec((1, c * chunk, aw), lambda i, j: (i, j, 0))
    mats = lambda c: pl.BlockSpec((1, c, n_pair, PAIR, PAIR), lambda i, j: (i, j, 0, 0, 0))
    seq = jax.ShapeDtypeStruct((b, t, aw), F32)
    mat = jax.ShapeDtypeStruct((b, n_chunk, n_pair, PAIR, PAIR), F32)
    qh, y0, m, c0 = pl.pallas_call(
        functools.partial(_rwkv_chunk_kernel, chunk=chunk, cps=cps_a),
        grid=(b, n_chunk // cps_a), in_specs=[rows(cps_a)] * 6,
        out_specs=[rows(cps_a), rows(cps_a), mats(cps_a), mats(cps_a)], out_shape=[seq, seq, mat, mat],
        compiler_params=_cparams(("parallel", "parallel")), name="rwkv_chunk",
    )(r, lw, k, v, kk, a)
    rows_b = pl.BlockSpec((b, cps_b * chunk, aw), lambda j: (0, j, 0))
    mats_b = pl.BlockSpec((b, cps_b, n_pair, PAIR, PAIR), lambda j: (0, j, 0, 0, 0))
    return pl.pallas_call(
        functools.partial(_rwkv_state_kernel, chunk=chunk, cps=cps_b),
        grid=(n_chunk // cps_b,), in_specs=[rows_b, rows_b, mats_b, mats_b],
        out_specs=rows_b, out_shape=seq,
        scratch_shapes=[pltpu.VMEM((b, n_pair, PAIR, PAIR), F32)],
        compiler_params=_cparams(("arbitrary",)), name="rwkv_state",
    )(qh, y0, m, c0)


def _kv_prep_kernel(c_ref, k_ref, gc_ref, gk_ref, cn_ref, kn_ref):
    cn_ref[...] = _rms(c_ref[...], gc_ref[...]).astype(cn_ref.dtype)
    kn_ref[...] = _rms(k_ref[...], gk_ref[...]).astype(kn_ref.dtype)


def kv_prep(c_kv, k_dup, g_c, g_k_dup, tm=512):
    n, w = c_kv.shape
    tile = pl.BlockSpec((tm, w), lambda i: (i, 0))
    const = pl.BlockSpec((1, w), lambda i: (0, 0))
    return pl.pallas_call(
        _kv_prep_kernel, grid=(n // tm,), in_specs=[tile, tile, const, const], out_specs=[tile, tile],
        out_shape=[jax.ShapeDtypeStruct((n, w), BF16)] * 2,
        compiler_params=_cparams(("parallel",)), name="kv_prep",
    )(c_kv, k_dup, g_c.reshape(1, w), g_k_dup.reshape(1, w))


SEARCH_VALUE_STEPS = 24
SEARCH_STEPS_PER_CHECK = 2


def _tree_reduce(combine, xs):
    while len(xs) > 1:
        xs = [combine(xs[i], xs[i + 1]) for i in range(0, len(xs) - 1, 2)] + ([xs[-1]] if len(xs) % 2 else [])
    return xs[0]


def _key_to_f32(key):
    return pltpu.bitcast(jnp.where(key >= 0, key, key ^ 0x7FFFFFFF), F32)


def _f32_to_key(x):
    bits = pltpu.bitcast(x, I32)
    return jnp.where(bits >= 0, bits, bits ^ 0x7FFFFFFF)


def _dsa_kernel(qi_ref, wit_ref, q_ref, kn_ref, cn_ref, cnt_ref, wk_ref, wv_ref, o_ref,
                skey_ref, qpt_ref, acc_ref, m_ref, l_ref, *, qb, kt, ka, topk, n_heads, idx_bits):
    blk = pl.program_id(1)
    nt = ((blk + 1) * qb + kt - 1) // kt
    first = _lane_is_first_head()
    first_sub = lax.broadcasted_iota(I32, (PAIR, 1), 0) < HEAD_DIM
    q_pos = blk * qb + lax.broadcasted_iota(I32, (1, qb), 1)
    row0 = lax.broadcasted_iota(I32, (kt, 1), 0)
    heads = range(n_heads)

    qi = qi_ref[0]
    wit = wit_ref[0]
    qi_t = []
    for p in range(n_heads // 2):
        pair_t = qi[:, p * PAIR:(p + 1) * PAIR].T * HEAD_DIM ** -0.5
        qi_t += [jnp.where(first_sub, pair_t, 0.0).astype(BF16), jnp.where(first_sub, 0.0, pair_t).astype(BF16)]

    def score_tile(j, carry):
        off = pl.multiple_of(j * kt, kt)
        kn = kn_ref[0, pl.ds(off, kt), :]
        sc = jnp.zeros((kt, qb), F32)
        for h in heads:
            sc = sc + wit[h:h + 1, :] * jnp.maximum(_dot(kn, qi_t[h]), 0.0)
        sc = sc * n_heads ** -0.5
        key = jnp.where(sc == 0.0, 0, _f32_to_key(sc))
        skey_ref[j] = jnp.where(off + row0 <= q_pos, key, INT_MIN)
        return carry

    lax.fori_loop(0, nt, score_tile, 0)

    def fold(tile_fn, init, combine):
        def body(j, acc):
            part = tile_fn(skey_ref[j], j * kt + row0).reshape(kt // 8, 8, qb)
            return combine(acc, _tree_reduce(combine, [part[i] for i in range(kt // 8)]))
        return lax.fori_loop(0, nt, body, jnp.full((8, qb), init, F32))

    def count(pred):
        acc = fold(lambda sk, idx: jnp.where(pred(sk, idx), 1.0, 0.0), 0.0, jnp.add)
        return jnp.sum(acc, axis=0, keepdims=True)

    few = q_pos + 1 <= topk
    v_max = jnp.max(fold(lambda sk, _: jnp.where(sk == INT_MIN, -jnp.inf, _key_to_f32(sk)), -jnp.inf, jnp.maximum),
                    axis=0, keepdims=True)
    v_min = jnp.min(fold(lambda sk, _: jnp.where(sk == INT_MIN, jnp.inf, _key_to_f32(sk)), jnp.inf, jnp.minimum),
                    axis=0, keepdims=True)
    lo0 = _f32_to_key(v_min)
    hi0 = _f32_to_key(v_max) + 1
    cnt0 = (q_pos + 1).astype(F32)

    def unresolved(lo, hi, cnt_lo):
        return ~(few | (cnt_lo == topk) | (hi <= lo + 1))

    def any_open(lo, hi, cnt_lo):
        return jnp.max(jnp.where(unresolved(lo, hi, cnt_lo), 1.0, 0.0))

    def search_cond(st):
        return (st[0] < SEARCH_VALUE_STEPS + 34) & (st[4] > 0.0)

    def search_step(st):
        it, lo, hi, cnt_lo, _ = st
        for _ in range(SEARCH_STEPS_PER_CHECK):
            mid_v = _f32_to_key(0.5 * _key_to_f32(lo) + 0.5 * _key_to_f32(hi))
            mid_k = (lo >> 1) + (hi >> 1) + (lo & hi & 1)
            mid = jnp.where((it < SEARCH_VALUE_STEPS) & (mid_v > lo) & (mid_v < hi), mid_v, mid_k)
            cnt = count(lambda sk, _: sk >= mid)
            open_ = unresolved(lo, hi, cnt_lo)
            up = open_ & (cnt >= topk)
            lo, cnt_lo = jnp.where(up, mid, lo), jnp.where(up, cnt, cnt_lo)
            hi = jnp.where(open_ & ~up, mid, hi)
            it = it + 1
        return it, lo, hi, cnt_lo, any_open(lo, hi, cnt_lo)

    st0 = (jnp.int32(0), lo0, hi0, cnt0, any_open(lo0, hi0, cnt0))
    _, lo, _, cnt_lo, _ = lax.while_loop(search_cond, search_step, st0)
    thr = jnp.where(few, INT_MIN, lo)
    tie = (cnt_lo > topk) & ~few
    any_tie = jnp.max(jnp.where(tie, 1, 0))
    n_gt = lax.fori_loop(0, any_tie, lambda i, c: count(lambda sk, _: sk > thr), jnp.zeros((1, qb), F32))
    need = topk - n_gt

    def idx_bit(i, prefix):
        cand = prefix | jnp.left_shift(jnp.int32(1), idx_bits - 1 - i)
        cnt = count(lambda sk, kidx: (sk == thr) & (kidx < cand))
        return jnp.where(cnt < need, cand, prefix)

    jmax = lax.fori_loop(0, idx_bits * any_tie, idx_bit, jnp.zeros((1, qb), I32))
    jmax = jnp.where(tie, jmax, 0x7FFFFFFF)

    q = q_ref[0]
    for h in heads:
        pr = slice((h // 2) * PAIR, (h // 2 + 1) * PAIR)
        wk_h = jnp.where(first if h % 2 == 0 else ~first, wk_ref[:, pr], 0.0)
        qpt_ref[h] = (_dot_nt(wk_h, q[:, pr].astype(BF16)) * (HEAD_DIM ** -0.5 * math.log2(math.e))).astype(BF16)
    m_ref[...] = jnp.full_like(m_ref, NEG_BIG)
    l_ref[...] = jnp.zeros_like(l_ref)
    acc_ref[...] = jnp.zeros_like(acc_ref)

    def attn_tile(j, carry):
        sk_all = skey_ref[j]
        for part in range(kt // ka):
            off = pl.multiple_of(j * kt + part * ka, ka)
            sk = sk_all[part * ka:(part + 1) * ka, :]
            kidx = off + row0[:ka]
            sel = ((sk > thr) | ((sk == thr) & (kidx <= jmax))) & (kidx <= q_pos)
            bias = jnp.where(sel, 0.0, NEG_BIG)
            cn = cn_ref[0, pl.ds(off, ka), :]
            cn_t = cnt_ref[0, j * (kt // ka) + part]
            s = [_dot(cn, qpt_ref[h]) + bias for h in heads]
            m_old = [m_ref[h] for h in heads]
            m_new = [jnp.maximum(m_old[h], jnp.max(s[h], axis=0, keepdims=True)) for h in heads]
            pexp = [jnp.exp2(s[h] - m_new[h]) for h in heads]
            alpha = [jnp.exp2(m_old[h] - m_new[h]) for h in heads]
            pv = [_dot(cn_t, pexp[h].astype(BF16)) for h in heads]
            for h in heads:
                l_ref[h] = alpha[h] * l_ref[h] + jnp.sum(pexp[h], axis=0, keepdims=True)
                acc_ref[h] = alpha[h] * acc_ref[h] + pv[h]
                m_ref[h] = m_new[h]
        return carry

    lax.fori_loop(0, nt, attn_tile, 0)

    rank = cn_ref.shape[-1]
    for p in range(n_heads // 2):
        pr = slice(p * PAIR, (p + 1) * PAIR)
        out = jnp.zeros((qb, PAIR), F32)
        for h in (2 * p, 2 * p + 1):
            lat_t = (acc_ref[h] / l_ref[h]).astype(BF16)
            wv_h = jnp.where(first if h % 2 == 0 else ~first, wv_ref[:, pr], 0.0)
            out = out + lax.dot_general(lat_t, wv_h, _TN, preferred_element_type=F32)
        o_ref[0, :, pr] = out.astype(o_ref.dtype)


def dsa_attention(q_idx, w_idx, q_b, kn, cn, wk_t, wv, qb=128, kt=512, ka=256):
    b, t, width = q_b.shape
    n_heads = width // HEAD_DIM
    rank = cn.shape[-1]
    kt, ka = min(kt, t), min(ka, t)
    topk = min(IDX_TOPK_MAX, t // 4)
    w_t = jnp.swapaxes(w_idx[:, :, :n_heads], 1, 2)
    cn_t = jnp.swapaxes(cn.reshape(b, t // ka, ka, rank), 2, 3)
    qtile = lambda c: pl.BlockSpec((1, qb, c), lambda i, j: (i, j, 0))
    full = lambda c: pl.BlockSpec((1, t, c), lambda i, j: (i, 0, 0))
    const = lambda a: pl.BlockSpec(a.shape, lambda i, j: (0, 0))
    kern = functools.partial(_dsa_kernel, qb=qb, kt=kt, ka=ka, topk=topk, n_heads=n_heads,
                             idx_bits=max(1, (t - 1).bit_length()))
    return pl.pallas_call(
        kern, grid=(b, t // qb),
        in_specs=[qtile(width), pl.BlockSpec((1, n_heads, qb), lambda i, j: (i, 0, j)), qtile(width),
                  full(LANES), full(rank), pl.BlockSpec((1, t // ka, rank, ka), lambda i, j: (i, 0, 0, 0)),
                  const(wk_t), const(wv)],
        out_specs=qtile(width), out_shape=jax.ShapeDtypeStruct((b, t, width), BF16),
        scratch_shapes=[pltpu.VMEM((t // kt, kt, qb), I32), pltpu.VMEM((n_heads, rank, qb), BF16),
                        pltpu.VMEM((n_heads, rank, qb), F32), pltpu.VMEM((n_heads, 1, qb), F32),
                        pltpu.VMEM((n_heads, 1, qb), F32)],
        compiler_params=_cparams(("parallel", "arbitrary")), name="dsa_attention",
    )(q_idx, w_t, q_b, kn, cn, cn_t, wk_t, wv)


def _even_out_kernel(x_ref, y_ref, bonus_ref, gate_ref, yb_ref, lng_ref, lnb_ref, seg_ref, wa_ref, wb_ref, o_ref):
    seg = seg_ref[...]
    y = y_ref[...]
    yc = y - _dot_hi(y, seg) * (1.0 / HEAD_DIM)
    yn = yc * lax.rsqrt(_dot_hi(yc * yc, seg) * (1.0 / HEAD_DIM) + GN_EPS)
    ya = (yn * lng_ref[...] + lnb_ref[...] + bonus_ref[...]) * gate_ref[...]
    o_ref[...] = x_ref[...] + _dot(ya.astype(BF16), wa_ref[...]) + _dot(yb_ref[...], wb_ref[...])


def even_out(x2d, y, bonus, gate, y_b, ln_g, ln_b, w_out, tm=256):
    n, d = x2d.shape
    aw = y.shape[1]
    tile = lambda c: pl.BlockSpec((tm, c), lambda i: (i, 0))
    const = lambda a: pl.BlockSpec(a.shape, lambda i: (0, 0))
    consts = [ln_g.reshape(1, aw), ln_b.reshape(1, aw), _seg_ones(aw),
              w_out[:aw].astype(BF16), w_out[aw:].astype(BF16)]
    return pl.pallas_call(
        _even_out_kernel, grid=(n // tm,),
        in_specs=[tile(d), tile(aw), tile(aw), tile(aw), tile(y_b.shape[1])] + [const(a) for a in consts],
        out_specs=tile(d), out_shape=jax.ShapeDtypeStruct((n, d), F32),
        compiler_params=_cparams(("parallel",)), name="even_out",
    )(x2d, y, bonus, gate, y_b, *consts)


SB_PAIRS = 2


def _stickbreak_kernel(q_ref, k_ref, vt_ref, o_ref, acc_ref, carry_ref, *, tq, n_pairs):
    qi = pl.program_id(2)
    first_sub = lax.broadcasted_iota(I32, (PAIR, 1), 0) < HEAD_DIM
    key_i = lax.broadcasted_iota(I32, (tq, tq), 0)
    qry_i = lax.broadcasted_iota(I32, (tq, tq), 1)
    before = key_i < qry_i
    later = (qry_i > key_i).astype(BF16)
    heads = [(p, x) for p in range(n_pairs) for x in range(2)]
    q_h = {}
    for p in range(n_pairs):
        q_t = q_ref[0, :, p * PAIR:(p + 1) * PAIR].astype(F32).T * HEAD_DIM ** -0.5
        q_h[p, 0] = jnp.where(first_sub, q_t, 0.0).astype(BF16)
        q_h[p, 1] = jnp.where(first_sub, 0.0, q_t).astype(BF16)
    acc_ref[...] = jnp.zeros_like(acc_ref)
    carry_ref[...] = jnp.zeros_like(carry_ref)

    def tile(j, diag):
        off = pl.multiple_of(j * tq, tq)
        k2 = [k_ref[0, pl.ds(off, tq), p * PAIR:(p + 1) * PAIR] for p in range(n_pairs)]
        z = {h: _dot(k2[h[0]], q_h[h]) * math.log2(math.e) for h in heads}
        lk = {h: -(jnp.maximum(z[h], 0.0) + jnp.log2(1.0 + jnp.exp2(-jnp.abs(z[h])))) for h in heads}
        if diag:
            lk = {h: jnp.where(before, lk[h], 0.0) for h in heads}
        hi = {h: lk[h].astype(BF16) for h in heads}
        lo = {h: (lk[h] - hi[h].astype(F32)).astype(BF16) for h in heads}
        tail = {h: _dot(later, hi[h]) + _dot(later, lo[h]) + carry_ref[h[0], h[1]] for h in heads}
        w = {h: jnp.exp2(z[h] + lk[h] + tail[h]) for h in heads}
        if diag:
            w = {h: jnp.where(before, w[h], 0.0) for h in heads}
        for p in range(n_pairs):
            v_t = vt_ref[0, p, j]
            acc_ref[p] += (_dot(jnp.where(first_sub, v_t, 0.0), w[p, 0].astype(BF16))
                           + _dot(jnp.where(first_sub, 0.0, v_t), w[p, 1].astype(BF16)))
        for h in heads:
            carry_ref[h[0], h[1]] += jnp.sum(lk[h], axis=0, keepdims=True)
        return (jnp.max(carry_ref[...]) > -110.0 * math.log2(math.e)).astype(I32)

    live = tile(qi, True)

    def cond(st):
        return (st[0] >= 0) & (st[1] > 0)

    def body(st):
        return st[0] - 1, tile(st[0], False)

    lax.while_loop(cond, body, (qi - 1, live))
    for p in range(n_pairs):
        o_ref[0, :, p * PAIR:(p + 1) * PAIR] = acc_ref[p].T.astype(o_ref.dtype)


def stickbreak_attention(qkv, n_heads, tq=256):
    b, t, _ = qkv.shape
    n_pair = n_heads // 2
    tq = min(tq, t)
    width = n_heads * HEAD_DIM
    gw = SB_PAIRS * PAIR
    n_grp = n_pair // SB_PAIRS
    v_t = jnp.transpose(qkv[:, :, 2 * width:].reshape(b, t // tq, tq, n_pair, PAIR), (0, 3, 1, 4, 2))
    q_spec = pl.BlockSpec((1, tq, gw), lambda i, p, j: (i, j, p))
    k_spec = pl.BlockSpec((1, t, gw), lambda i, p, j: (i, 0, n_grp + p))
    v_spec = pl.BlockSpec((1, SB_PAIRS, t // tq, PAIR, tq), lambda i, p, j: (i, p, 0, 0, 0))
    return pl.pallas_call(
        functools.partial(_stickbreak_kernel, tq=tq, n_pairs=SB_PAIRS), grid=(b, n_grp, t // tq),
        in_specs=[q_spec, k_spec, v_spec], out_specs=q_spec,
        out_shape=jax.ShapeDtypeStruct((b, t, width), BF16),
        scratch_shapes=[pltpu.VMEM((SB_PAIRS, PAIR, tq), F32), pltpu.VMEM((SB_PAIRS, 2, 1, tq), F32)],
        compiler_params=_cparams(("parallel", "parallel", "arbitrary")), name="stickbreak",
    )(qkv, qkv, v_t)


def _proj_residual_kernel(x_ref, y_ref, w_ref, o_ref):
    o_ref[...] = x_ref[...] + _dot(y_ref[...], w_ref[...])


def proj_residual(x2d, y, w, tm=256):
    n, d = x2d.shape
    c = y.shape[1]
    return pl.pallas_call(
        _proj_residual_kernel, grid=(n // tm,),
        in_specs=[pl.BlockSpec((tm, d), lambda i: (i, 0)), pl.BlockSpec((tm, c), lambda i: (i, 0)),
                  pl.BlockSpec(w.shape, lambda i: (0, 0))],
        out_specs=pl.BlockSpec((tm, d), lambda i: (i, 0)), out_shape=jax.ShapeDtypeStruct((n, d), F32),
        compiler_params=_cparams(("parallel",)), name="proj_residual",
    )(x2d, y, w)


def _router_kernel(x_ref, g_ref, wr_ref, h_ref, ids_ref, gates_ref):
    h = _rms(x_ref[...], g_ref[...])
    h_ref[...] = h.astype(h_ref.dtype)
    logits = _dot_hi(h, wr_ref[...])
    lane = lax.broadcasted_iota(I32, logits.shape, 1)
    lane_f = lane.astype(F32)

    def top1(vals):
        mx = jnp.max(vals, axis=1, keepdims=True)
        return mx, jnp.min(jnp.where(vals == mx, lane_f, 1e9), axis=1, keepdims=True).astype(I32)

    is_g = lane < N_GROUPS
    g_max, g_sel = top1(jnp.where(is_g, logits, -jnp.inf))
    g_gate = 1.0 / jnp.sum(jnp.where(is_g, jnp.exp(logits - g_max), 0.0), axis=1, keepdims=True)
    e_lane = lane - N_GROUPS
    in_grp = (e_lane >= g_sel * EXPERTS_PER_GROUP) & (e_lane < (g_sel + 1) * EXPERTS_PER_GROUP)
    el = jnp.where(in_grp, logits, -jnp.inf)
    m1, i1 = top1(el)
    m2, i2 = top1(jnp.where(lane == i1, -jnp.inf, el))
    e21 = jnp.exp(m2 - m1)
    p1 = 1.0 / (1.0 + e21)
    ids_ref[...] = jnp.where(lane == 0, i1 - N_GROUPS, jnp.where(lane == 1, i2 - N_GROUPS, 0))
    gates_ref[...] = jnp.where(lane == 0, g_gate * p1, jnp.where(lane == 1, g_gate * (e21 * p1), 0.0))


def moe_router(x2d, g, w_router, tm=256):
    n, d = x2d.shape
    tile = lambda c: pl.BlockSpec((tm, c), lambda i: (i, 0))
    return pl.pallas_call(
        _router_kernel, grid=(n // tm,),
        in_specs=[tile(d), pl.BlockSpec((1, d), lambda i: (0, 0)), pl.BlockSpec(w_router.shape, lambda i: (0, 0))],
        out_specs=[tile(d), tile(LANES), tile(LANES)],
        out_shape=[jax.ShapeDtypeStruct((n, d), BF16), jax.ShapeDtypeStruct((n, LANES), I32),
                   jax.ShapeDtypeStruct((n, LANES), F32)],
        compiler_params=_cparams(("parallel",)), name="moe_router",
    )(x2d, g.reshape(1, d), w_router)


MOE_TT = 2048
MOE_CH = 128
MOE_KC = 512
_PLAN_BLK = 256


def _moe_plan_kernel(ids_ref, idst_ref, rank_ref, dest_ref, meta_ref, *, tt):
    nb = tt // _PLAN_BLK
    row = lax.broadcasted_iota(I32, (_PLAN_BLK, _PLAN_BLK), 0)
    col = lax.broadcasted_iota(I32, (_PLAN_BLK, _PLAN_BLK), 1)
    earlier_t = (row < col).astype(BF16)
    earlier = (col < row).astype(BF16)
    e_sub = lax.broadcasted_iota(I32, (N_EXPERTS, _PLAN_BLK), 0)
    carry_t = jnp.zeros((N_EXPERTS, 1), F32)
    for b in range(nb):
        sl = slice(b * _PLAN_BLK, (b + 1) * _PLAN_BLK)
        member = (e_sub == idst_ref[0:1, sl]) | (e_sub == idst_ref[1:2, sl])
        m_t = jnp.where(member, 1.0, 0.0)
        before = _dot(m_t.astype(BF16), earlier_t) + carry_t
        rank_ref[0, :, sl] = jnp.where(member, before, -1.0)
        carry_t = carry_t + jnp.sum(m_t, axis=1, keepdims=True)
    lane = lax.broadcasted_iota(I32, (_PLAN_BLK, LANES), 1)
    carry = jnp.zeros((1, LANES), F32)
    prefix = []
    for b in range(nb):
        sl = slice(b * _PLAN_BLK, (b + 1) * _PLAN_BLK)
        m = jnp.where((lane == ids_ref[sl, 0:1]) | (lane == ids_ref[sl, 1:2]), 1.0, 0.0)
        prefix.append(_dot(earlier, m.astype(BF16)) + carry)
        carry = carry + jnp.sum(m, axis=0, keepdims=True)
    counts = carry
    padded = jnp.ceil(counts * (1.0 / MOE_CH)) * MOE_CH
    l_row = lax.broadcasted_iota(I32, (LANES, LANES), 0)
    l_col = lax.broadcasted_iota(I32, (LANES, LANES), 1)
    offs = _dot_hi(padded, (l_row < l_col).astype(F32))
    for b in range(nb):
        sl = slice(b * _PLAN_BLK, (b + 1) * _PLAN_BLK)
        where_row = prefix[b] + offs
        d0 = jnp.sum(jnp.where(lane == ids_ref[sl, 0:1], where_row, 0.0), axis=1, keepdims=True)
        d1 = jnp.sum(jnp.where(lane == ids_ref[sl, 1:2], where_row, 0.0), axis=1, keepdims=True)
        dest_ref[sl, :] = jnp.where(lane == 0, d0, jnp.where(lane == 1, d1, 0.0)).astype(I32)
    r_used = jnp.sum(padded, axis=1, keepdims=True)
    sub8 = lax.broadcasted_iota(I32, (8, LANES), 0)
    meta_ref[0] = jnp.where(sub8 == 0, counts, jnp.where(sub8 == 1, offs, jnp.where(sub8 == 2, r_used, 0.0))).astype(I32)


def moe_plan(ids, ids_t, tt):
    n = ids.shape[0]
    n_tiles = n // tt
    return pl.pallas_call(
        functools.partial(_moe_plan_kernel, tt=tt), grid=(n_tiles,),
        in_specs=[pl.BlockSpec((tt, LANES), lambda i: (i, 0)), pl.BlockSpec((8, tt), lambda i: (0, i))],
        out_specs=[pl.BlockSpec((1, N_EXPERTS, tt), lambda i: (i, 0, 0)), pl.BlockSpec((tt, LANES), lambda i: (i, 0)),
                   pl.BlockSpec((1, 8, LANES), lambda i: (i, 0, 0))],
        out_shape=[jax.ShapeDtypeStruct((n_tiles, N_EXPERTS, tt), F32), jax.ShapeDtypeStruct((n, LANES), I32),
                   jax.ShapeDtypeStruct((n_tiles, 8, LANES), I32)],
        compiler_params=_cparams(("parallel",)), name="moe_plan",
    )(ids, ids_t)


def _pack_bf16_halves(y):
    half = y.shape[1] // 2
    lo = pltpu.bitcast(y[:, :half].astype(BF16).astype(F32), I32)
    hi = pltpu.bitcast(y[:, half:].astype(BF16).astype(F32), I32)
    return lax.shift_right_logical(lo, 16) | hi


def _unpack_bf16_halves(w):
    return pltpu.bitcast(w << 16, F32), pltpu.bitcast(w & jnp.int32(-65536), F32)


def _moe_ffn_kernel(cnt_ref, off_ref, h_ref, rank_ref, idst_ref, gt_ref, w1_ref, w3_ref, w2_ref, y_ref, *, tt):
    t, e = pl.program_id(0), pl.program_id(1)

    @pl.when(e == 0)
    def _():
        y_ref[...] = jnp.zeros_like(y_ref)

    cnt = cnt_ref[t * N_EXPERTS + e]
    off = off_ref[t * N_EXPERTS + e]
    rank_row = rank_ref[0, pl.ds(e, 1), :]
    gate_row = jnp.where(idst_ref[0:1, :] == e, gt_ref[0:1, :], gt_ref[1:2, :])
    sub = lax.broadcasted_iota(I32, (MOE_CH, 1), 0)

    def chunk(c, carry):
        r0 = c * MOE_CH
        pick = rank_row == (r0 + sub).astype(F32)
        x = _dot(jnp.where(pick, 1.0, 0.0).astype(BF16), h_ref[...]).astype(BF16)
        a = _dot(x, w1_ref[0])
        hdn = (a * jax.nn.sigmoid(a)) * _dot(x, w3_ref[0])
        y = _dot(hdn.astype(BF16), w2_ref[0])
        gate = jnp.sum(jnp.where(pick, gate_row, 0.0), axis=1, keepdims=True)
        y_ref[0, pl.ds(pl.multiple_of(off + r0, MOE_CH), MOE_CH), :] = _pack_bf16_halves(y * gate)
        return carry

    lax.fori_loop(0, (cnt + MOE_CH - 1) // MOE_CH, chunk, 0)


def moe_ffn(h, rank_t, ids_t, gates_t, counts, offs, w1, w3, w2, tt):
    n, d = h.shape
    ff = w1.shape[-1]
    n_tiles = n // tt
    r_max = TOP_K_EXPERTS * tt + N_EXPERTS * MOE_CH
    grid_spec = pltpu.PrefetchScalarGridSpec(
        num_scalar_prefetch=2, grid=(n_tiles, N_EXPERTS),
        in_specs=[pl.BlockSpec((tt, d), lambda i, e, c, o: (i, 0)),
                  pl.BlockSpec((1, N_EXPERTS, tt), lambda i, e, c, o: (i, 0, 0)),
                  pl.BlockSpec((8, tt), lambda i, e, c, o: (0, i)),
                  pl.BlockSpec((8, tt), lambda i, e, c, o: (0, i)),
                  pl.BlockSpec((1, d, ff), lambda i, e, c, o: (e, 0, 0)),
                  pl.BlockSpec((1, d, ff), lambda i, e, c, o: (e, 0, 0)),
                  pl.BlockSpec((1, ff, d), lambda i, e, c, o: (e, 0, 0))],
        out_specs=pl.BlockSpec((1, r_max, d // 2), lambda i, e, c, o: (i, 0, 0)))
    return pl.pallas_call(
        functools.partial(_moe_ffn_kernel, tt=tt), grid_spec=grid_spec,
        out_shape=jax.ShapeDtypeStruct((n_tiles, r_max, d // 2), I32),
        compiler_params=_cparams(("arbitrary", "arbitrary")), name="moe_ffn",
    )(counts, offs, h, rank_t, ids_t, gates_t, w1, w3, w2)


def _moe_combine_kernel(dest_ref, x_ref, y_ref, o_ref, ga_ref, gb_ref, *, tm, per):
    base = (pl.program_id(0) * per + pl.program_id(1)) * tm

    def fetch(i, carry):
        tok = (base + i) * TOP_K_EXPERTS
        ga_ref[pl.ds(i, 1), :] = y_ref[0, pl.ds(dest_ref[tok], 1), :]
        gb_ref[pl.ds(i, 1), :] = y_ref[0, pl.ds(dest_ref[tok + 1], 1), :]
        return carry

    lax.fori_loop(0, tm, fetch, 0, unroll=8)
    half = x_ref.shape[1] // 2
    lo_a, hi_a = _unpack_bf16_halves(ga_ref[...])
    lo_b, hi_b = _unpack_bf16_halves(gb_ref[...])
    o_ref[:, :half] = x_ref[:, :half] + (lo_a + lo_b)
    o_ref[:, half:] = x_ref[:, half:] + (hi_a + hi_b)


def moe_combine(x2d, dest_flat, y_all, tt, tm=256):
    n, d = x2d.shape
    r_max = y_all.shape[1]
    per = tt // tm
    grid_spec = pltpu.PrefetchScalarGridSpec(
        num_scalar_prefetch=1, grid=(n // tt, per),
        in_specs=[pl.BlockSpec((tm, d), lambda i, j, u: (i * per + j, 0)),
                  pl.BlockSpec((1, r_max, d // 2), lambda i, j, u: (i, 0, 0))],
        out_specs=pl.BlockSpec((tm, d), lambda i, j, u: (i * per + j, 0)),
        scratch_shapes=[pltpu.VMEM((tm, d // 2), I32), pltpu.VMEM((tm, d // 2), I32)])
    return pl.pallas_call(
        functools.partial(_moe_combine_kernel, tm=tm, per=per), grid_spec=grid_spec,
        out_shape=jax.ShapeDtypeStruct((n, d), F32),
        compiler_params=_cparams(("arbitrary", "arbitrary")), name="moe_combine",
    )(dest_flat, x2d, y_all)


def _final_norm_kernel(x_ref, g_ref, o_ref):
    o_ref[...] = _rms(x_ref[...], g_ref[...])


def final_norm(x2d, g, tm=512):
    n, d = x2d.shape
    return pl.pallas_call(
        _final_norm_kernel, grid=(n // tm,),
        in_specs=[pl.BlockSpec((tm, d), lambda i: (i, 0)), pl.BlockSpec((1, d), lambda i: (0, 0))],
        out_specs=pl.BlockSpec((tm, d), lambda i: (i, 0)), out_shape=jax.ShapeDtypeStruct((n, d), F32),
        compiler_params=_cparams(("parallel",)), name="final_norm",
    )(x2d, g.reshape(1, d))


def moe_layer(x2d, norm_g, router_group, router_expert, w1, w3, w2):
    n, d = x2d.shape
    tt = min(MOE_TT, n)
    w_router = jnp.concatenate([router_group, jnp.moveaxis(router_expert, 0, 1).reshape(d, N_EXPERTS)], axis=1)
    h, ids, gates = moe_router(x2d, norm_g, _pad_cols(w_router, LANES))
    ids_t = jnp.pad(ids[:, :TOP_K_EXPERTS].T, ((0, 8 - TOP_K_EXPERTS), (0, 0)))
    gates_t = jnp.pad(gates[:, :TOP_K_EXPERTS].T, ((0, 8 - TOP_K_EXPERTS), (0, 0)))
    rank_t, dest, meta = moe_plan(ids, ids_t, tt)
    counts = meta[:, 0, :N_EXPERTS].reshape(-1)
    offs = meta[:, 1, :N_EXPERTS].reshape(-1)
    y_all = moe_ffn(h, rank_t, ids_t, gates_t, counts, offs, w1.astype(BF16), w3.astype(BF16), w2.astype(BF16), tt)
    return moe_combine(x2d, dest[:, :TOP_K_EXPERTS].reshape(-1), y_all, tt)


def even_layer(x, v_first, p):
    b, t, d = x.shape
    n = b * t
    w_in = p['w_in']
    aw = p['k_k'].shape[0]
    bw = p['w_kv_up'].shape[1] // 2
    rank = p['w_kv_up'].shape[0]
    n_idx = w_in.shape[1] - (3 * aw + bw + rank + bw + HEAD_DIM)
    o = 0
    segs = []
    for width in (3 * aw, bw, rank, bw, HEAD_DIM, n_idx):
        segs.append(w_in[:, o:o + width])
        o += width
    segs[4] = jnp.concatenate([segs[4], segs[4]], axis=1)
    segs[5] = _pad_cols(segs[5], LANES)
    rkv, q_b, c_kv, q_idx, k_dup, w_idx = norm_proj(
        x.reshape(n, d), p['norm'], [s.astype(BF16) for s in segs], [F32] * 6)

    r, lw, k_mod, v, kk, iclr, gate, bonus = rwkv_prep(x, rkv.reshape(b, t, 3 * aw), v_first, p)
    y = rwkv_scan(r, lw, k_mod, v, kk, iclr)

    cn, kn = kv_prep(c_kv, k_dup, p['kv_norm'], jnp.concatenate([p['idx_k_norm']] * 2))
    wk_t = p['w_kv_up'][:, :bw].astype(BF16)
    wv = p['w_kv_up'][:, bw:].astype(BF16)
    y_b = dsa_attention(q_idx.reshape(b, t, bw), w_idx.reshape(b, t, LANES), q_b.reshape(b, t, bw),
                        kn.reshape(b, t, LANES), cn.reshape(b, t, rank), wk_t, wv)

    x_new = even_out(x.reshape(n, d), y.reshape(n, aw), bonus.reshape(n, aw), gate.reshape(n, aw),
                     y_b.reshape(n, bw), p['ln_g'], p['ln_b'], p['w_out'])
    return x_new.reshape(b, t, d), v


def odd_layer(x, norm_g, w_qkv, w_out):
    b, t, d = x.shape
    n = b * t
    cw = w_qkv.shape[1] // 3
    (qkv,) = norm_proj(x.reshape(n, d), norm_g, [w_qkv.astype(BF16)], [BF16])
    y_c = stickbreak_attention(qkv.reshape(b, t, 3 * cw), cw // HEAD_DIM)
    return proj_residual(x.reshape(n, d), y_c.reshape(n, cw), w_out.astype(BF16)).reshape(b, t, d)


@jax.jit
def _forward(x, ev_norm, ev_w_in, ev_w_out, a_mu_rkv, a_mu_lora, a_w0, a_w1, a_w2, a_a0, a_a1, a_a2, a_g1,
             a_g2, a_k_k, a_k_a, a_r_k, a_ln_g, a_ln_b, a_mu_vres, a_v0, a_v1, a_v2, b_kv_norm, b_w_kv_up,
             b_idx_k_norm, od_norm, od_w_qkv, od_w_out, moe_norm, moe_router_group, moe_router_expert,
             moe_w1, moe_w3, moe_w2, final_norm_g):
    b, t, d = x.shape
    depth = moe_norm.shape[0]
    v_first = None
    for i in range(depth):
        if i % 2 == 0:
            e = i // 2
            p = dict(norm=ev_norm[e], w_in=ev_w_in[e], w_out=ev_w_out[e], mu_rkv=a_mu_rkv[e], mu_lora=a_mu_lora[e],
                     w0=a_w0[e], w1=a_w1[e], w2=a_w2[e], a0=a_a0[e], a1=a_a1[e], a2=a_a2[e], g1=a_g1[e], g2=a_g2[e],
                     k_k=a_k_k[e], k_a=a_k_a[e], r_k=a_r_k[e].reshape(-1), ln_g=a_ln_g[e], ln_b=a_ln_b[e],
                     kv_norm=b_kv_norm[e], w_kv_up=b_w_kv_up[e], idx_k_norm=b_idx_k_norm[e])
            if e > 0:
                p.update(mu_v=a_mu_vres[e - 1], v0=a_v0[e - 1], v1=a_v1[e - 1], v2=a_v2[e - 1])
            x, v_used = even_layer(x, v_first if e > 0 else None, p)
            if e == 0:
                v_first = v_used
        else:
            o = i // 2
            x = odd_layer(x, od_norm[o], od_w_qkv[o], od_w_out[o])
        x = moe_layer(x.reshape(b * t, d), moe_norm[i], moe_router_group[i], moe_router_expert[i],
                      moe_w1[i], moe_w3[i], moe_w2[i]).reshape(b, t, d)
    return final_norm(x.reshape(b * t, d), final_norm_g).reshape(b, t, d)


def kernel(x, ev_norm, ev_w_in, ev_w_out, a_mu_rkv, a_mu_lora, a_w0, a_w1, a_w2, a_a0, a_a1, a_a2, a_g1, a_g2, a_k_k, a_k_a, a_r_k, a_ln_g, a_ln_b, a_mu_vres, a_v0, a_v1, a_v2, b_kv_norm, b_w_kv_up, b_idx_k_norm, od_norm, od_w_qkv, od_w_out, moe_norm, moe_router_group, moe_router_expert, moe_w1, moe_w3, moe_w2, final_norm):
    return _forward(x, ev_norm, ev_w_in, ev_w_out, a_mu_rkv, a_mu_lora, a_w0, a_w1, a_w2, a_a0, a_a1, a_a2, a_g1,
                    a_g2, a_k_k, a_k_a, a_r_k, a_ln_g, a_ln_b, a_mu_vres, a_v0, a_v1, a_v2, b_kv_norm, b_w_kv_up,
                    b_idx_k_norm, od_norm, od_w_qkv, od_w_out, moe_norm, moe_router_group, moe_router_expert,
                    moe_w1, moe_w3, moe_w2, final_norm)
```

```python
import functools
import math

import jax
import jax.numpy as jnp
from jax import lax
from jax.experimental import pallas as pl
from jax.experimental.pallas import tpu as pltpu

F32 = jnp.float32
BF16 = jnp.bfloat16
I32 = jnp.int32
HI = lax.Precision.HIGHEST

LANES = 128
HEAD_DIM = 64
PAIR = 2 * HEAD_DIM
RMS_EPS = 1e-6
GN_EPS = 64e-5
N_GROUPS = 4
EXPERTS_PER_GROUP = 8
N_EXPERTS = N_GROUPS * EXPERTS_PER_GROUP
TOP_K_EXPERTS = 2
IDX_TOPK_MAX = 256
INT_MIN = -2147483648
NEG_BIG = -1e30
VMEM_LIMIT = 56 * 1024 * 1024

_NT = (((1,), (1,)), ((), ()))
_TN = (((0,), (0,)), ((), ()))


def _cparams(sem):
    return pltpu.CompilerParams(dimension_semantics=sem, vmem_limit_bytes=VMEM_LIMIT)


def _rms(x, g):
    return x * lax.rsqrt(jnp.mean(x * x, axis=-1, keepdims=True) + RMS_EPS) * g


def _dot(a, b):
    return jnp.dot(a, b, preferred_element_type=F32)


def _dot_hi(a, b):
    return jnp.dot(a, b, preferred_element_type=F32, precision=HI)


def _seg_dot(x, seg):
    hi = x.astype(BF16)
    return _dot(hi, seg) + _dot((x - hi.astype(F32)).astype(BF16), seg)


def _dot_nt(a, b):
    return lax.dot_general(a, b, _NT, preferred_element_type=F32)


def _lane_is_first_head(width=PAIR):
    return lax.broadcasted_iota(I32, (1, width), 1) % PAIR < HEAD_DIM


def _norm_proj_kernel(x_ref, g_ref, *refs, n_seg):
    w_refs, o_refs = refs[:n_seg], refs[n_seg:]
    h = _rms(x_ref[...], g_ref[...]).astype(BF16)
    for w_ref, o_ref in zip(w_refs, o_refs):
        o_ref[...] = _dot(h, w_ref[...]).astype(o_ref.dtype)


def norm_proj(x2d, g, ws, out_dtypes, tm=256):
    n, d = x2d.shape
    in_specs = [pl.BlockSpec((tm, d), lambda i: (i, 0)), pl.BlockSpec((1, d), lambda i: (0, 0))]
    in_specs += [pl.BlockSpec(w.shape, lambda i: (0, 0)) for w in ws]
    out_specs = [pl.BlockSpec((tm, w.shape[1]), lambda i: (i, 0)) for w in ws]
    out_shape = [jax.ShapeDtypeStruct((n, w.shape[1]), dt) for w, dt in zip(ws, out_dtypes)]
    return pl.pallas_call(
        functools.partial(_norm_proj_kernel, n_seg=len(ws)),
        grid=(n // tm,), in_specs=in_specs, out_specs=out_specs, out_shape=out_shape,
        compiler_params=_cparams(("parallel",)), name="norm_proj",
    )(x2d, g.reshape(1, d), *ws)


def _shift_rows(z, carry_ref):
    tm = z.shape[0]
    first = lax.broadcasted_iota(I32, (tm, 1), 0) == 0
    prev = jnp.where(first, carry_ref[...], pltpu.roll(z, 1, 0))
    carry_ref[...] = z[tm - 1:tm, :]
    return prev


def _rwkv_prep_kernel(*refs, has_vres):
    it = iter(refs)
    x_ref, rkv_ref = next(it), next(it)
    vfirst_ref = next(it) if has_vres else None
    g_ref, mu_rkv_ref, mu_lora_ref = next(it), next(it), next(it)
    w0_ref, w1_ref, w2_ref = next(it), next(it), next(it)
    a0_ref, a1_ref, a2_ref = next(it), next(it), next(it)
    g1_ref, g2_ref = next(it), next(it)
    if has_vres:
        muv_ref, v0_ref, v1_ref, v2_ref = next(it), next(it), next(it), next(it)
    kk_w_ref, ka_ref, rk_ref, seg_ref = next(it), next(it), next(it), next(it)
    r_o, lw_o, k_o, v_o, kk_o, a_o, gate_o, bonus_o = (next(it) for _ in range(8))
    hcarry, rkvcarry = next(it), next(it)

    @pl.when(pl.program_id(1) == 0)
    def _():
        hcarry[...] = jnp.zeros_like(hcarry)
        rkvcarry[...] = jnp.zeros_like(rkvcarry)

    aw = r_o.shape[-1]
    h = _rms(x_ref[0], g_ref[...])
    dh = _shift_rows(h, hcarry) - h
    rkv = rkv_ref[0]
    rkv = rkv + (_shift_rows(rkv, rkvcarry) - rkv) * mu_rkv_ref[...]
    r, k, v = rkv[:, :aw], rkv[:, aw:2 * aw], rkv[:, 2 * aw:]

    def lora_in(row):
        return (h + dh * mu_lora_ref[row:row + 1, :]).astype(BF16)

    dec = w0_ref[...] + _dot(jnp.tanh(_dot(lora_in(0), w1_ref[...])).astype(BF16), w2_ref[...])
    logw = -math.exp(-0.5) * jax.nn.sigmoid(dec)
    iclr = jax.nn.sigmoid(a0_ref[...] + _dot(_dot(lora_in(1), a1_ref[...]).astype(BF16), a2_ref[...]))
    gate = _dot(jax.nn.sigmoid(_dot(lora_in(2), g1_ref[...])).astype(BF16), g2_ref[...])
    if has_vres:
        xv = (h + dh * muv_ref[...]).astype(BF16)
        mix = jax.nn.sigmoid(v0_ref[...] + _dot(_dot(xv, v1_ref[...]).astype(BF16), v2_ref[...]))
        v = v + (vfirst_ref[0] - v) * mix

    seg = seg_ref[...]
    kk = k * kk_w_ref[...]
    kk = kk * lax.rsqrt(jnp.maximum(_seg_dot(kk * kk, seg), 1e-12))
    k_mod = k * (1.0 + (iclr - 1.0) * ka_ref[...])
    bonus = _seg_dot(r * k_mod * rk_ref[...], seg) * v

    r_o[0], lw_o[0], k_o[0], v_o[0] = r, logw, k_mod, v
    kk_o[0], a_o[0], gate_o[0], bonus_o[0] = kk, iclr, gate, bonus


def _pad_cols(w, n):
    return jnp.pad(w, ((0, 0), (0, n - w.shape[1])))


def _pad_rows(w, n):
    return jnp.pad(w, ((0, n - w.shape[0]), (0, 0)))


def _seg_ones(width):
    i = jnp.arange(width) // HEAD_DIM
    return (i[:, None] == i[None, :]).astype(BF16)


def rwkv_prep(x, rkv, v_first, p, tm=256):
    b, t, d = x.shape
    aw = rkv.shape[-1] // 3
    has_vres = v_first is not None
    row = lambda a: a.reshape(1, -1)
    const = lambda a: pl.BlockSpec(a.shape, lambda i, j: (0,) * a.ndim)
    tile = lambda c: pl.BlockSpec((1, tm, c), lambda i, j: (i, j, 0))

    args, specs = [x, rkv], [tile(d), tile(3 * aw)]
    if has_vres:
        args.append(v_first)
        specs.append(tile(aw))
    consts = [row(p['norm']), p['mu_rkv'].reshape(1, 3 * aw), p['mu_lora'],
              row(p['w0']), _pad_cols(p['w1'], LANES).astype(BF16), _pad_rows(p['w2'], LANES).astype(BF16),
              row(p['a0']), _pad_cols(p['a1'], LANES).astype(BF16), _pad_rows(p['a2'], LANES).astype(BF16),
              p['g1'].astype(BF16), p['g2'].astype(BF16)]
    if has_vres:
        consts += [row(p['mu_v']), row(p['v0']), _pad_cols(p['v1'], LANES).astype(BF16),
                   _pad_rows(p['v2'], LANES).astype(BF16)]
    consts += [row(p['k_k']), row(p['k_a']), row(p['r_k']), _seg_ones(aw)]
    args += consts
    specs += [const(a) for a in consts]
    out_shape = [jax.ShapeDtypeStruct((b, t, aw), F32)] * 8
    return pl.pallas_call(
        functools.partial(_rwkv_prep_kernel, has_vres=has_vres),
        grid=(b, t // tm), in_specs=specs, out_specs=[tile(aw)] * 8, out_shape=out_shape,
        scratch_shapes=[pltpu.VMEM((1, d), F32), pltpu.VMEM((1, 3 * aw), F32)],
        compiler_params=_cparams(("arbitrary", "arbitrary")), name="rwkv_prep",
    )(*args)


def _split_bf16(x):
    hi = x.astype(BF16)
    return hi, (x - hi.astype(F32)).astype(BF16)


def _mm3(a, b, dims=None):
    a_hi, a_lo = _split_bf16(a)
    b_hi, b_lo = _split_bf16(b)
    if dims is None:
        f = lambda p, q: jnp.dot(p, q, preferred_element_type=F32)
    else:
        f = lambda p, q: lax.dot_general(p, q, dims, preferred_element_type=F32)
    return f(a_hi, b_hi) + f(a_hi, b_lo) + f(a_lo, b_hi)


def _mmc(a, b, dims=None):
    a, b = a.astype(BF16), b.astype(BF16)
    if dims is None:
        return jnp.dot(a, b, preferred_element_type=F32)
    return lax.dot_general(a, b, dims, preferred_element_type=F32)


def _cumsum_rows(x):
    n = x.shape[0]
    row = lax.broadcasted_iota(I32, (n, 1), 0)
    s = 1
    while s < n:
        x = x + jnp.where(row >= s, pltpu.roll(x, s, 0), 0.0)
        s *= 2
    return x


def _rwkv_chunk_kernel(r_ref, lw_ref, k_ref, v_ref, kk_ref, a_ref, qh_ref, y0_ref, m_ref, c0_ref, *, chunk, cps):
    L = chunk
    n_pair = r_ref.shape[-1] // PAIR
    row = lax.broadcasted_iota(I32, (L, L), 0)
    col = lax.broadcasted_iota(I32, (L, L), 1)
    strict = col < row
    incl = col <= row
    eye = (col == row).astype(F32)
    first = _lane_is_first_head()
    p_row = lax.broadcasted_iota(I32, (PAIR, PAIR), 0)
    p_col = lax.broadcasted_iota(I32, (PAIR, PAIR), 1)
    blockdiag = p_row // HEAD_DIM == p_col // HEAD_DIM
    pair_eye = p_row == p_col

    units = [(ci, p) for ci in range(cps) for p in range(n_pair)]
    heads = (first, ~first)
    pre = {}
    for u in units:
        ci, p = u
        rows = slice(ci * L, (ci + 1) * L)
        sl = slice(p * PAIR, (p + 1) * PAIR)
        lw = lw_ref[0, rows, sl]
        c = _cumsum_rows(lw)
        g = jnp.exp(c)
        g_inv = jnp.exp(-c)
        kk = kk_ref[0, rows, sl]
        pre[u] = dict(rows=rows, sl=sl, g_last=g[L - 1:L, :],
                      at=-kk * jnp.exp(c - lw),
                      bt=kk * a_ref[0, rows, sl] * g_inv,
                      kt=k_ref[0, rows, sl] * g_inv,
                      rt=r_ref[0, rows, sl] * g,
                      vv=v_ref[0, rows, sl])
    uh = [(u, h) for u in units for h in range(2)]

    gm = {}
    for u, h in uh:
        d = pre[u]
        lhs = jnp.concatenate([jnp.where(heads[h], d['at'], 0.0), jnp.where(heads[h], d['rt'], 0.0)], axis=0)
        gm[u, h] = _mm3(lhs, jnp.concatenate([d['bt'], d['kt']], axis=0), _NT)
    a_ak = {x: jnp.where(strict, gm[x][:L, L:], 0.0) for x in uh}
    a_rb = {x: jnp.where(incl, gm[x][L:, :L], 0.0) for x in uh}
    a_rk = {x: jnp.where(incl, gm[x][L:, L:], 0.0) for x in uh}
    apow = {x: jnp.where(strict, gm[x][:L, :L], 0.0) for x in uh}
    inv = {x: eye + apow[x] for x in uh}
    for _ in range(max(1, (L - 1).bit_length()) - 1):
        apow = {x: _mmc(apow[x], apow[x]) for x in uh}
        inv = {x: inv[x] + _mmc(inv[x], apow[x]) for x in uh}
    w_h = {(u, h): _mmc(inv[u, h], pre[u]['at']) for u, h in uh}
    akv = {(u, h): _mmc(a_ak[u, h], pre[u]['vv']) for u, h in uh}
    u0_h = {x: _mmc(inv[x], akv[x]) for x in uh}
    w = {u: jnp.where(first, w_h[u, 0], w_h[u, 1]) for u in units}
    u0 = {u: jnp.where(first, u0_h[u, 0], u0_h[u, 1]) for u in units}
    wu = {u: jnp.concatenate([w[u], u0[u]], axis=1) for u in units}
    arb = {(u, h): _mmc(a_rb[u, h], wu[u]) for u, h in uh}
    ark = {(u, h): _mmc(a_rk[u, h], pre[u]['vv']) for u, h in uh}
    first2 = jnp.concatenate([first, first], axis=1)
    for u in units:
        d = pre[u]
        ci, p = u
        arb_w = jnp.where(first2, arb[u, 0], arb[u, 1])
        qh_ref[0, d['rows'], d['sl']] = d['rt'] + arb_w[:, :PAIR]
        y0_ref[0, d['rows'], d['sl']] = arb_w[:, PAIR:] + jnp.where(first, ark[u, 0], ark[u, 1])
        bg = d['bt'] * d['g_last']
        m_kk = _mmc(bg, w[u], _TN)
        m_ref[0, ci, p] = jnp.where(pair_eye, d['g_last'], 0.0) + jnp.where(blockdiag, m_kk, 0.0)
        c0 = _mmc(jnp.concatenate([u0[u], d['vv']], axis=0),
                  jnp.concatenate([bg, d['kt'] * d['g_last']], axis=0), _TN)
        c0_ref[0, ci, p] = jnp.where(blockdiag, c0, 0.0)


def _rwkv_state_kernel(qh_ref, y0_ref, m_ref, c0_ref, y_ref, s_ref, *, chunk, cps):
    L = chunk

    @pl.when(pl.program_id(0) == 0)
    def _():
        s_ref[...] = jnp.zeros_like(s_ref)

    n_batch = qh_ref.shape[0]
    n_pair = qh_ref.shape[-1] // PAIR
    streams = [(b, p) for b in range(n_batch) for p in range(n_pair)]
    s2 = {x: s_ref[x[0], x[1]] for x in streams}
    for ci in range(cps):
        rows = slice(ci * L, (ci + 1) * L)
        for b, p in streams:
            sl = slice(p * PAIR, (p + 1) * PAIR)
            y_ref[b, rows, sl] = _mm3(qh_ref[b, rows, sl], s2[b, p], _NT) + y0_ref[b, rows, sl]
            s2[b, p] = _mm3(s2[b, p], m_ref[b, ci, p], _NT) + c0_ref[b, ci, p]
    for b, p in streams:
        s_ref[b, p] = s2[b, p]


def rwkv_scan(r, lw, k, v, kk, a, chunk=64, cps_a=2, cps_b=4):
    b, t, aw = r.shape
    n_pair = aw // PAIR
    n_chunk = t // chunk
    cps_a, cps_b = min(cps_a, n_chunk), min(cps_b, n_chunk)
    rows = lambda c: pl.BlockSpec((1, c * chunk, aw), lambda i, j: (i, j, 0))
    mats = lambda c: pl.BlockSpec((1, c, n_pair, PAIR, PAIR), lambda i, j: (i, j, 0, 0, 0))
    seq = jax.ShapeDtypeStruct((b, t, aw), F32)
    mat = jax.ShapeDtypeStruct((b, n_chunk, n_pair, PAIR, PAIR), F32)
    qh, y0, m, c0 = pl.pallas_call(
        functools.partial(_rwkv_chunk_kernel, chunk=chunk, cps=cps_a),
        grid=(b, n_chunk // cps_a), in_specs=[rows(cps_a)] * 6,
        out_specs=[rows(cps_a), rows(cps_a), mats(cps_a), mats(cps_a)], out_shape=[seq, seq, mat, mat],
        compiler_params=_cparams(("parallel", "parallel")), name="rwkv_chunk",
    )(r, lw, k, v, kk, a)
    rows_b = pl.BlockSpec((b, cps_b * chunk, aw), lambda j: (0, j, 0))
    mats_b = pl.BlockSpec((b, cps_b, n_pair, PAIR, PAIR), lambda j: (0, j, 0, 0, 0))
    return pl.pallas_call(
        functools.partial(_rwkv_state_kernel, chunk=chunk, cps=cps_b),
        grid=(n_chunk // cps_b,), in_specs=[rows_b, rows_b, mats_b, mats_b],
        out_specs=rows_b, out_shape=seq,
        scratch_shapes=[pltpu.VMEM((b, n_pair, PAIR, PAIR), F32)],
        compiler_params=_cparams(("arbitrary",)), name="rwkv_state",
    )(qh, y0, m, c0)


def _kv_prep_kernel(c_ref, k_ref, gc_ref, gk_ref, cn_ref, kn_ref):
    cn_ref[...] = _rms(c_ref[...], gc_ref[...]).astype(cn_ref.dtype)
    kn_ref[...] = _rms(k_ref[...], gk_ref[...]).astype(kn_ref.dtype)


def kv_prep(c_kv, k_dup, g_c, g_k_dup, tm=512):
    n, w = c_kv.shape
    tile = pl.BlockSpec((tm, w), lambda i: (i, 0))
    const = pl.BlockSpec((1, w), lambda i: (0, 0))
    return pl.pallas_call(
        _kv_prep_kernel, grid=(n // tm,), in_specs=[tile, tile, const, const], out_specs=[tile, tile],
        out_shape=[jax.ShapeDtypeStruct((n, w), BF16)] * 2,
        compiler_params=_cparams(("parallel",)), name="kv_prep",
    )(c_kv, k_dup, g_c.reshape(1, w), g_k_dup.reshape(1, w))


SEARCH_VALUE_STEPS = 24
SEARCH_STEPS_PER_CHECK = 2


def _tree_reduce(combine, xs):
    while len(xs) > 1:
        xs = [combine(xs[i], xs[i + 1]) for i in range(0, len(xs) - 1, 2)] + ([xs[-1]] if len(xs) % 2 else [])
    return xs[0]


def _key_to_f32(key):
    return pltpu.bitcast(jnp.where(key >= 0, key, key ^ 0x7FFFFFFF), F32)


def _f32_to_key(x):
    bits = pltpu.bitcast(x, I32)
    return jnp.where(bits >= 0, bits, bits ^ 0x7FFFFFFF)


def _dsa_kernel(qi_ref, wit_ref, q_ref, kn_ref, cn_ref, cnt_ref, wk_ref, wv_ref, o_ref,
                skey_ref, qpt_ref, acc_ref, m_ref, l_ref, *, qb, kt, ka, topk, n_heads, idx_bits):
    blk = pl.program_id(1)
    nt = ((blk + 1) * qb + kt - 1) // kt
    first = _lane_is_first_head()
    first_sub = lax.broadcasted_iota(I32, (PAIR, 1), 0) < HEAD_DIM
    q_pos = blk * qb + lax.broadcasted_iota(I32, (1, qb), 1)
    row0 = lax.broadcasted_iota(I32, (kt, 1), 0)
    heads = range(n_heads)

    qi = qi_ref[0]
    wit = wit_ref[0]
    qi_t = []
    for p in range(n_heads // 2):
        pair_t = qi[:, p * PAIR:(p + 1) * PAIR].T * HEAD_DIM ** -0.5
        qi_t += [jnp.where(first_sub, pair_t, 0.0).astype(BF16), jnp.where(first_sub, 0.0, pair_t).astype(BF16)]

    def score_tile(j, carry):
        off = pl.multiple_of(j * kt, kt)
        kn = kn_ref[0, pl.ds(off, kt), :]
        sc = jnp.zeros((kt, qb), F32)
        for h in heads:
            sc = sc + wit[h:h + 1, :] * jnp.maximum(_dot(kn, qi_t[h]), 0.0)
        sc = sc * n_heads ** -0.5
        key = jnp.where(sc == 0.0, 0, _f32_to_key(sc))
        causal = off + row0 <= q_pos
        skey_ref[j] = jnp.where(causal, key, INT_MIN)
        hi8 = _tree_reduce(jnp.maximum, list(jnp.where(causal, sc, -jnp.inf).reshape(kt // 8, 8, qb)))
        lo8 = _tree_reduce(jnp.minimum, list(jnp.where(causal, sc, jnp.inf).reshape(kt // 8, 8, qb)))
        return jnp.maximum(carry[0], hi8), jnp.minimum(carry[1], lo8)

    hi8, lo8 = lax.fori_loop(0, nt, score_tile,
                             (jnp.full((8, qb), -jnp.inf, F32), jnp.full((8, qb), jnp.inf, F32)))

    def fold(tile_fn, init, combine):
        def body(j, acc):
            part = tile_fn(skey_ref[j], j * kt + row0).reshape(kt // 8, 8, qb)
            return combine(acc, _tree_reduce(combine, [part[i] for i in range(kt // 8)]))
        return lax.fori_loop(0, nt, body, jnp.full((8, qb), init, F32))

    def count(pred):
        acc = fold(lambda sk, idx: jnp.where(pred(sk, idx), 1.0, 0.0), 0.0, jnp.add)
        return jnp.sum(acc, axis=0, keepdims=True)

    few = q_pos + 1 <= topk
    v_max = jnp.max(hi8, axis=0, keepdims=True)
    v_min = jnp.min(lo8, axis=0, keepdims=True)
    lo0 = _f32_to_key(jnp.where(v_min == 0.0, 0.0, v_min))
    hi0 = _f32_to_key(jnp.where(v_max == 0.0, 0.0, v_max)) + 1
    cnt0 = (q_pos + 1).astype(F32)

    def unresolved(lo, hi, cnt_lo):
        return ~(few | (cnt_lo == topk) | (hi <= lo + 1))

    def any_open(lo, hi, cnt_lo):
        return jnp.max(jnp.where(unresolved(lo, hi, cnt_lo), 1.0, 0.0))

    def search_cond(st):
        return (st[0] < SEARCH_VALUE_STEPS + 34) & (st[4] > 0.0)

    def search_step(st):
        it, lo, hi, cnt_lo, _ = st
        for _ in range(SEARCH_STEPS_PER_CHECK):
            mid_v = _f32_to_key(0.5 * _key_to_f32(lo) + 0.5 * _key_to_f32(hi))
            mid_k = (lo >> 1) + (hi >> 1) + (lo & hi & 1)
            mid = jnp.where((it < SEARCH_VALUE_STEPS) & (mid_v > lo) & (mid_v < hi), mid_v, mid_k)
            cnt = count(lambda sk, _: sk >= mid)
            open_ = unresolved(lo, hi, cnt_lo)
            up = open_ & (cnt >= topk)
            lo, cnt_lo = jnp.where(up, mid, lo), jnp.where(up, cnt, cnt_lo)
            hi = jnp.where(open_ & ~up, mid, hi)
            it = it + 1
        return it, lo, hi, cnt_lo, any_open(lo, hi, cnt_lo)

    st0 = (jnp.int32(0), lo0, hi0, cnt0, any_open(lo0, hi0, cnt0))
    _, lo, _, cnt_lo, _ = lax.while_loop(search_cond, search_step, st0)
    thr = jnp.where(few, INT_MIN, lo)
    tie = (cnt_lo > topk) & ~few
    any_tie = jnp.max(jnp.where(tie, 1, 0))
    n_gt = lax.fori_loop(0, any_tie, lambda i, c: count(lambda sk, _: sk > thr), jnp.zeros((1, qb), F32))
    need = topk - n_gt

    def idx_bit(i, prefix):
        cand = prefix | jnp.left_shift(jnp.int32(1), idx_bits - 1 - i)
        cnt = count(lambda sk, kidx: (sk == thr) & (kidx < cand))
        return jnp.where(cnt < need, cand, prefix)

    jmax = lax.fori_loop(0, idx_bits * any_tie, idx_bit, jnp.zeros((1, qb), I32))
    jmax = jnp.where(tie, jmax, 0x7FFFFFFF)

    q = q_ref[0]
    for h in heads:
        pr = slice((h // 2) * PAIR, (h // 2 + 1) * PAIR)
        wk_h = jnp.where(first if h % 2 == 0 else ~first, wk_ref[:, pr], 0.0)
        qpt_ref[h] = (_dot_nt(wk_h, q[:, pr].astype(BF16)) * (HEAD_DIM ** -0.5 * math.log2(math.e))).astype(BF16)
    m_ref[...] = jnp.full_like(m_ref, NEG_BIG)
    l_ref[...] = jnp.zeros_like(l_ref)
    acc_ref[...] = jnp.zeros_like(acc_ref)

    def attn_tile(j, carry):
        sk_all = skey_ref[j]
        for part in range(kt // ka):
            off = pl.multiple_of(j * kt + part * ka, ka)
            sk = sk_all[part * ka:(part + 1) * ka, :]
            kidx = off + row0[:ka]
            sel = ((sk > thr) | ((sk == thr) & (kidx <= jmax))) & (kidx <= q_pos)
            bias = jnp.where(sel, 0.0, NEG_BIG)
            cn = cn_ref[0, pl.ds(off, ka), :]
            cn_t = cnt_ref[0, j * (kt // ka) + part]
            s = [_dot(cn, qpt_ref[h]) + bias for h in heads]
            m_old = [m_ref[h] for h in heads]
            m_new = [jnp.maximum(m_old[h], jnp.max(s[h], axis=0, keepdims=True)) for h in heads]
            pexp = [jnp.exp2(s[h] - m_new[h]) for h in heads]
            alpha = [jnp.exp2(m_old[h] - m_new[h]) for h in heads]
            pv = [_dot(cn_t, pexp[h].astype(BF16)) for h in heads]
            for h in heads:
                l_ref[h] = alpha[h] * l_ref[h] + jnp.sum(pexp[h], axis=0, keepdims=True)
                acc_ref[h] = alpha[h] * acc_ref[h] + pv[h]
                m_ref[h] = m_new[h]
        return carry

    lax.fori_loop(0, nt, attn_tile, 0)

    rank = cn_ref.shape[-1]
    for p in range(n_heads // 2):
        pr = slice(p * PAIR, (p + 1) * PAIR)
        out = jnp.zeros((qb, PAIR), F32)
        for h in (2 * p, 2 * p + 1):
            lat_t = (acc_ref[h] / l_ref[h]).astype(BF16)
            wv_h = jnp.where(first if h % 2 == 0 else ~first, wv_ref[:, pr], 0.0)
            out = out + lax.dot_general(lat_t, wv_h, _TN, preferred_element_type=F32)
        o_ref[0, :, pr] = out.astype(o_ref.dtype)


def dsa_attention(q_idx, w_idx, q_b, kn, cn, wk_t, wv, qb=128, kt=512, ka=256):
    b, t, width = q_b.shape
    n_heads = width // HEAD_DIM
    rank = cn.shape[-1]
    kt, ka = min(kt, t), min(ka, t)
    topk = min(IDX_TOPK_MAX, t // 4)
    w_t = jnp.swapaxes(w_idx[:, :, :n_heads], 1, 2)
    cn_t = jnp.swapaxes(cn.reshape(b, t // ka, ka, rank), 2, 3)
    qtile = lambda c: pl.BlockSpec((1, qb, c), lambda i, j: (i, j, 0))
    full = lambda c: pl.BlockSpec((1, t, c), lambda i, j: (i, 0, 0))
    const = lambda a: pl.BlockSpec(a.shape, lambda i, j: (0, 0))
    kern = functools.partial(_dsa_kernel, qb=qb, kt=kt, ka=ka, topk=topk, n_heads=n_heads,
                             idx_bits=max(1, (t - 1).bit_length()))
    return pl.pallas_call(
        kern, grid=(b, t // qb),
        in_specs=[qtile(width), pl.BlockSpec((1, n_heads, qb), lambda i, j: (i, 0, j)), qtile(width),
                  full(LANES), full(rank), pl.BlockSpec((1, t // ka, rank, ka), lambda i, j: (i, 0, 0, 0)),
                  const(wk_t), const(wv)],
        out_specs=qtile(width), out_shape=jax.ShapeDtypeStruct((b, t, width), BF16),
        scratch_shapes=[pltpu.VMEM((t // kt, kt, qb), I32), pltpu.VMEM((n_heads, rank, qb), BF16),
                        pltpu.VMEM((n_heads, rank, qb), F32), pltpu.VMEM((n_heads, 1, qb), F32),
                        pltpu.VMEM((n_heads, 1, qb), F32)],
        compiler_params=_cparams(("parallel", "arbitrary")), name="dsa_attention",
    )(q_idx, w_t, q_b, kn, cn, cn_t, wk_t, wv)


def _even_out_kernel(x_ref, y_ref, bonus_ref, gate_ref, yb_ref, lng_ref, lnb_ref, seg_ref, wa_ref, wb_ref, o_ref):
    seg = seg_ref[...]
    y = y_ref[...]
    yc = y - _seg_dot(y, seg) * (1.0 / HEAD_DIM)
    yn = yc * lax.rsqrt(_seg_dot(yc * yc, seg) * (1.0 / HEAD_DIM) + GN_EPS)
    ya = (yn * lng_ref[...] + lnb_ref[...] + bonus_ref[...]) * gate_ref[...]
    o_ref[...] = x_ref[...] + _dot(ya.astype(BF16), wa_ref[...]) + _dot(yb_ref[...], wb_ref[...])


def even_out(x2d, y, bonus, gate, y_b, ln_g, ln_b, w_out, tm=256):
    n, d = x2d.shape
    aw = y.shape[1]
    tile = lambda c: pl.BlockSpec((tm, c), lambda i: (i, 0))
    const = lambda a: pl.BlockSpec(a.shape, lambda i: (0, 0))
    consts = [ln_g.reshape(1, aw), ln_b.reshape(1, aw), _seg_ones(aw),
              w_out[:aw].astype(BF16), w_out[aw:].astype(BF16)]
    return pl.pallas_call(
        _even_out_kernel, grid=(n // tm,),
        in_specs=[tile(d), tile(aw), tile(aw), tile(aw), tile(y_b.shape[1])] + [const(a) for a in consts],
        out_specs=tile(d), out_shape=jax.ShapeDtypeStruct((n, d), F32),
        compiler_params=_cparams(("parallel",)), name="even_out",
    )(x2d, y, bonus, gate, y_b, *consts)


SB_PAIRS = 2


def _stickbreak_kernel(q_ref, k_ref, vt_ref, o_ref, acc_ref, carry_ref, *, tq, n_pairs):
    qi = pl.program_id(2)
    first_sub = lax.broadcasted_iota(I32, (PAIR, 1), 0) < HEAD_DIM
    key_i = lax.broadcasted_iota(I32, (tq, tq), 0)
    qry_i = lax.broadcasted_iota(I32, (tq, tq), 1)
    before = key_i < qry_i
    later = (qry_i > key_i).astype(BF16)
    heads = [(p, x) for p in range(n_pairs) for x in range(2)]
    q_h = {}
    for p in range(n_pairs):
        q_t = q_ref[0, :, p * PAIR:(p + 1) * PAIR].astype(F32).T * HEAD_DIM ** -0.5
        q_h[p, 0] = jnp.where(first_sub, q_t, 0.0).astype(BF16)
        q_h[p, 1] = jnp.where(first_sub, 0.0, q_t).astype(BF16)
    acc_ref[...] = jnp.zeros_like(acc_ref)
    carry_ref[...] = jnp.zeros_like(carry_ref)

    def tile(j, diag):
        off = pl.multiple_of(j * tq, tq)
        k2 = [k_ref[0, pl.ds(off, tq), p * PAIR:(p + 1) * PAIR] for p in range(n_pairs)]
        z = {h: _dot(k2[h[0]], q_h[h]) * math.log2(math.e) for h in heads}
        lk = {h: -(jnp.maximum(z[h], 0.0) + jnp.log2(1.0 + jnp.exp2(-jnp.abs(z[h])))) for h in heads}
        if diag:
            lk = {h: jnp.where(before, lk[h], 0.0) for h in heads}
        hi = {h: lk[h].astype(BF16) for h in heads}
        lo = {h: (lk[h] - hi[h].astype(F32)).astype(BF16) for h in heads}
        tail = {h: _dot(later, hi[h]) + _dot(later, lo[h]) + carry_ref[h[0], h[1]] for h in heads}
        w = {h: jnp.exp2(z[h] + lk[h] + tail[h]) for h in heads}
        if diag:
            w = {h: jnp.where(before, w[h], 0.0) for h in heads}
        for p in range(n_pairs):
            v_t = vt_ref[0, p, j]
            acc_ref[p] += (_dot(jnp.where(first_sub, v_t, 0.0), w[p, 0].astype(BF16))
                           + _dot(jnp.where(first_sub, 0.0, v_t), w[p, 1].astype(BF16)))
        for h in heads:
            carry_ref[h[0], h[1]] += jnp.sum(lk[h], axis=0, keepdims=True)
        return (jnp.max(carry_ref[...]) > -110.0 * math.log2(math.e)).astype(I32)

    live = tile(qi, True)

    def cond(st):
        return (st[0] >= 0) & (st[1] > 0)

    def body(st):
        return st[0] - 1, tile(st[0], False)

    lax.while_loop(cond, body, (qi - 1, live))
    for p in range(n_pairs):
        o_ref[0, :, p * PAIR:(p + 1) * PAIR] = acc_ref[p].T.astype(o_ref.dtype)


def _qkv_proj_kernel(x_ref, g_ref, wqk_ref, wv_ref, qk_ref, vt_ref):
    h = _rms(x_ref[...], g_ref[...]).astype(BF16)
    qk_ref[...] = _dot(h, wqk_ref[...]).astype(qk_ref.dtype)
    v = _dot(h, wv_ref[...])
    for p in range(v.shape[1] // PAIR):
        vt_ref[0, p, 0] = v[:, p * PAIR:(p + 1) * PAIR].T.astype(vt_ref.dtype)


def qkv_proj(x, g, w_qkv, tq):
    b, t, d = x.shape
    width = w_qkv.shape[1] // 3
    n_pair = width // PAIR
    nq = t // tq
    wqk, wv = w_qkv[:, :2 * width].astype(BF16), w_qkv[:, 2 * width:].astype(BF16)
    return pl.pallas_call(
        _qkv_proj_kernel, grid=(b * nq,),
        in_specs=[pl.BlockSpec((tq, d), lambda i: (i, 0)), pl.BlockSpec((1, d), lambda i: (0, 0)),
                  pl.BlockSpec(wqk.shape, lambda i: (0, 0)), pl.BlockSpec(wv.shape, lambda i: (0, 0))],
        out_specs=[pl.BlockSpec((tq, 2 * width), lambda i: (i, 0)),
                   pl.BlockSpec((1, n_pair, 1, PAIR, tq), lambda i: (i // nq, 0, i % nq, 0, 0))],
        out_shape=[jax.ShapeDtypeStruct((b * t, 2 * width), BF16),
                   jax.ShapeDtypeStruct((b, n_pair, nq, PAIR, tq), BF16)],
        compiler_params=_cparams(("parallel",)), name="qkv_proj",
    )(x.reshape(b * t, d), g.reshape(1, d), wqk, wv)


def stickbreak_attention(qk, v_t, n_heads, tq):
    b, t, _ = qk.shape
    n_pair = n_heads // 2
    width = n_heads * HEAD_DIM
    gw = SB_PAIRS * PAIR
    n_grp = n_pair // SB_PAIRS
    qkv = qk
    q_spec = pl.BlockSpec((1, tq, gw), lambda i, p, j: (i, j, p))
    k_spec = pl.BlockSpec((1, t, gw), lambda i, p, j: (i, 0, n_grp + p))
    v_spec = pl.BlockSpec((1, SB_PAIRS, t // tq, PAIR, tq), lambda i, p, j: (i, p, 0, 0, 0))
    return pl.pallas_call(
        functools.partial(_stickbreak_kernel, tq=tq, n_pairs=SB_PAIRS), grid=(b, n_grp, t // tq),
        in_specs=[q_spec, k_spec, v_spec], out_specs=q_spec,
        out_shape=jax.ShapeDtypeStruct((b, t, width), BF16),
        scratch_shapes=[pltpu.VMEM((SB_PAIRS, PAIR, tq), F32), pltpu.VMEM((SB_PAIRS, 2, 1, tq), F32)],
        compiler_params=_cparams(("parallel", "parallel", "arbitrary")), name="stickbreak",
    )(qkv, qkv, v_t)


def _proj_residual_kernel(x_ref, y_ref, w_ref, o_ref):
    o_ref[...] = x_ref[...] + _dot(y_ref[...], w_ref[...])


def proj_residual(x2d, y, w, tm=256):
    n, d = x2d.shape
    c = y.shape[1]
    return pl.pallas_call(
        _proj_residual_kernel, grid=(n // tm,),
        in_specs=[pl.BlockSpec((tm, d), lambda i: (i, 0)), pl.BlockSpec((tm, c), lambda i: (i, 0)),
                  pl.BlockSpec(w.shape, lambda i: (0, 0))],
        out_specs=pl.BlockSpec((tm, d), lambda i: (i, 0)), out_shape=jax.ShapeDtypeStruct((n, d), F32),
        compiler_params=_cparams(("parallel",)), name="proj_residual",
    )(x2d, y, w)


def _router_kernel(x_ref, g_ref, wr_hi_ref, wr_lo_ref, h_ref, ids_ref, gates_ref):
    h = _rms(x_ref[...], g_ref[...])
    h_hi = h.astype(BF16)
    h_ref[...] = h_hi
    h_lo = (h - h_hi.astype(F32)).astype(BF16)
    logits = _dot(h_hi, wr_hi_ref[...]) + _dot(h_hi, wr_lo_ref[...]) + _dot(h_lo, wr_hi_ref[...])
    lane = lax.broadcasted_iota(I32, logits.shape, 1)
    lane_f = lane.astype(F32)

    def top1(vals):
        mx = jnp.max(vals, axis=1, keepdims=True)
        return mx, jnp.min(jnp.where(vals == mx, lane_f, 1e9), axis=1, keepdims=True).astype(I32)

    is_g = lane < N_GROUPS
    g_max, g_sel = top1(jnp.where(is_g, logits, -jnp.inf))
    g_gate = 1.0 / jnp.sum(jnp.where(is_g, jnp.exp(logits - g_max), 0.0), axis=1, keepdims=True)
    e_lane = lane - N_GROUPS
    in_grp = (e_lane >= g_sel * EXPERTS_PER_GROUP) & (e_lane < (g_sel + 1) * EXPERTS_PER_GROUP)
    el = jnp.where(in_grp, logits, -jnp.inf)
    m1, i1 = top1(el)
    m2, i2 = top1(jnp.where(lane == i1, -jnp.inf, el))
    e21 = jnp.exp(m2 - m1)
    p1 = 1.0 / (1.0 + e21)
    ids_ref[...] = jnp.where(lane == 0, i1 - N_GROUPS, jnp.where(lane == 1, i2 - N_GROUPS, 0))
    gates_ref[...] = jnp.where(lane == 0, g_gate * p1, jnp.where(lane == 1, g_gate * (e21 * p1), 0.0))


def moe_router(x2d, g, w_router, tm=256):
    n, d = x2d.shape
    tile = lambda c: pl.BlockSpec((tm, c), lambda i: (i, 0))
    w_spec = pl.BlockSpec(w_router.shape, lambda i: (0, 0))
    w_hi = w_router.astype(BF16)
    w_lo = (w_router - w_hi.astype(F32)).astype(BF16)
    return pl.pallas_call(
        _router_kernel, grid=(n // tm,),
        in_specs=[tile(d), pl.BlockSpec((1, d), lambda i: (0, 0)), w_spec, w_spec],
        out_specs=[tile(d), tile(LANES), tile(LANES)],
        out_shape=[jax.ShapeDtypeStruct((n, d), BF16), jax.ShapeDtypeStruct((n, LANES), I32),
                   jax.ShapeDtypeStruct((n, LANES), F32)],
        compiler_params=_cparams(("parallel",)), name="moe_router",
    )(x2d, g.reshape(1, d), w_hi, w_lo)


MOE_TT = 2048
MOE_CH = 128
MOE_KC = 512
_PLAN_BLK = 256


def _moe_plan_kernel(ids_ref, idst_ref, rank_ref, dest_ref, meta_ref, *, tt):
    nb = tt // _PLAN_BLK
    row = lax.broadcasted_iota(I32, (_PLAN_BLK, _PLAN_BLK), 0)
    col = lax.broadcasted_iota(I32, (_PLAN_BLK, _PLAN_BLK), 1)
    earlier_t = (row < col).astype(BF16)
    earlier = (col < row).astype(BF16)
    e_sub = lax.broadcasted_iota(I32, (N_EXPERTS, _PLAN_BLK), 0)
    carry_t = jnp.zeros((N_EXPERTS, 1), F32)
    for b in range(nb):
        sl = slice(b * _PLAN_BLK, (b + 1) * _PLAN_BLK)
        member = (e_sub == idst_ref[0:1, sl]) | (e_sub == idst_ref[1:2, sl])
        m_t = jnp.where(member, 1.0, 0.0)
        before = _dot(m_t.astype(BF16), earlier_t) + carry_t
        rank_ref[0, :, sl] = jnp.where(member, before, -1.0)
        carry_t = carry_t + jnp.sum(m_t, axis=1, keepdims=True)
    lane = lax.broadcasted_iota(I32, (_PLAN_BLK, LANES), 1)
    carry = jnp.zeros((1, LANES), F32)
    prefix = []
    for b in range(nb):
        sl = slice(b * _PLAN_BLK, (b + 1) * _PLAN_BLK)
        m = jnp.where((lane == ids_ref[sl, 0:1]) | (lane == ids_ref[sl, 1:2]), 1.0, 0.0)
        prefix.append(_dot(earlier, m.astype(BF16)) + carry)
        carry = carry + jnp.sum(m, axis=0, keepdims=True)
    counts = carry
    padded = jnp.ceil(counts * (1.0 / MOE_CH)) * MOE_CH
    l_row = lax.broadcasted_iota(I32, (LANES, LANES), 0)
    l_col = lax.broadcasted_iota(I32, (LANES, LANES), 1)
    offs = _dot_hi(padded, (l_row < l_col).astype(F32))
    for b in range(nb):
        sl = slice(b * _PLAN_BLK, (b + 1) * _PLAN_BLK)
        where_row = prefix[b] + offs
        d0 = jnp.sum(jnp.where(lane == ids_ref[sl, 0:1], where_row, 0.0), axis=1, keepdims=True)
        d1 = jnp.sum(jnp.where(lane == ids_ref[sl, 1:2], where_row, 0.0), axis=1, keepdims=True)
        dest_ref[sl, :] = jnp.where(lane == 0, d0, jnp.where(lane == 1, d1, 0.0)).astype(I32)
    r_used = jnp.sum(padded, axis=1, keepdims=True)
    sub8 = lax.broadcasted_iota(I32, (8, LANES), 0)
    meta_ref[0] = jnp.where(sub8 == 0, counts, jnp.where(sub8 == 1, offs, jnp.where(sub8 == 2, r_used, 0.0))).astype(I32)


def moe_plan(ids, ids_t, tt):
    n = ids.shape[0]
    n_tiles = n // tt
    return pl.pallas_call(
        functools.partial(_moe_plan_kernel, tt=tt), grid=(n_tiles,),
        in_specs=[pl.BlockSpec((tt, LANES), lambda i: (i, 0)), pl.BlockSpec((8, tt), lambda i: (0, i))],
        out_specs=[pl.BlockSpec((1, N_EXPERTS, tt), lambda i: (i, 0, 0)), pl.BlockSpec((tt, LANES), lambda i: (i, 0)),
                   pl.BlockSpec((1, 8, LANES), lambda i: (i, 0, 0))],
        out_shape=[jax.ShapeDtypeStruct((n_tiles, N_EXPERTS, tt), F32), jax.ShapeDtypeStruct((n, LANES), I32),
                   jax.ShapeDtypeStruct((n_tiles, 8, LANES), I32)],
        compiler_params=_cparams(("parallel",)), name="moe_plan",
    )(ids, ids_t)


def _pack_bf16_halves(y):
    half = y.shape[1] // 2
    lo = pltpu.bitcast(y[:, :half].astype(BF16).astype(F32), I32)
    hi = pltpu.bitcast(y[:, half:].astype(BF16).astype(F32), I32)
    return lax.shift_right_logical(lo, 16) | hi


def _unpack_bf16_halves(w):
    return pltpu.bitcast(w << 16, F32), pltpu.bitcast(w & jnp.int32(-65536), F32)


def _moe_ffn_kernel(cnt_ref, off_ref, h_ref, rank_ref, idst_ref, gt_ref, w1_ref, w3_ref, w2_ref, y_ref, *, tt):
    t, e = pl.program_id(0), pl.program_id(1)

    @pl.when(e == 0)
    def _():
        y_ref[...] = jnp.zeros_like(y_ref)

    cnt = cnt_ref[t * N_EXPERTS + e]
    off = off_ref[t * N_EXPERTS + e]
    rank_row = rank_ref[0, pl.ds(e, 1), :]
    gate_row = jnp.where(idst_ref[0:1, :] == e, gt_ref[0:1, :], gt_ref[1:2, :])
    sub = lax.broadcasted_iota(I32, (MOE_CH, 1), 0)

    def chunk(c, carry):
        r0 = c * MOE_CH
        pick = rank_row == (r0 + sub).astype(F32)
        x = _dot(jnp.where(pick, 1.0, 0.0).astype(BF16), h_ref[...]).astype(BF16)
        a = _dot(x, w1_ref[0])
        hdn = (a * jax.nn.sigmoid(a)) * _dot(x, w3_ref[0])
        y = _dot(hdn.astype(BF16), w2_ref[0])
        gate = jnp.sum(jnp.where(pick, gate_row, 0.0), axis=1, keepdims=True)
        y_ref[0, pl.ds(pl.multiple_of(off + r0, MOE_CH), MOE_CH), :] = _pack_bf16_halves(y * gate)
        return carry

    lax.fori_loop(0, (cnt + MOE_CH - 1) // MOE_CH, chunk, 0)


def moe_ffn(h, rank_t, ids_t, gates_t, counts, offs, w1, w3, w2, tt):
    n, d = h.shape
    ff = w1.shape[-1]
    n_tiles = n // tt
    r_max = TOP_K_EXPERTS * tt + N_EXPERTS * MOE_CH
    grid_spec = pltpu.PrefetchScalarGridSpec(
        num_scalar_prefetch=2, grid=(n_tiles, N_EXPERTS),
        in_specs=[pl.BlockSpec((tt, d), lambda i, e, c, o: (i, 0)),
                  pl.BlockSpec((1, N_EXPERTS, tt), lambda i, e, c, o: (i, 0, 0)),
                  pl.BlockSpec((8, tt), lambda i, e, c, o: (0, i)),
                  pl.BlockSpec((8, tt), lambda i, e, c, o: (0, i)),
                  pl.BlockSpec((1, d, ff), lambda i, e, c, o: (e, 0, 0)),
                  pl.BlockSpec((1, d, ff), lambda i, e, c, o: (e, 0, 0)),
                  pl.BlockSpec((1, ff, d), lambda i, e, c, o: (e, 0, 0))],
        out_specs=pl.BlockSpec((1, r_max, d // 2), lambda i, e, c, o: (i, 0, 0)))
    return pl.pallas_call(
        functools.partial(_moe_ffn_kernel, tt=tt), grid_spec=grid_spec,
        out_shape=jax.ShapeDtypeStruct((n_tiles, r_max, d // 2), I32),
        compiler_params=_cparams(("arbitrary", "arbitrary")), name="moe_ffn",
    )(counts, offs, h, rank_t, ids_t, gates_t, w1, w3, w2)


def _moe_combine_kernel(dest_ref, x_ref, y_ref, o_ref, ga_ref, gb_ref, *, tm, per):
    base = (pl.program_id(0) * per + pl.program_id(1)) * tm

    def fetch(i, carry):
        tok = (base + i) * TOP_K_EXPERTS
        ga_ref[pl.ds(i, 1), :] = y_ref[0, pl.ds(dest_ref[tok], 1), :]
        gb_ref[pl.ds(i, 1), :] = y_ref[0, pl.ds(dest_ref[tok + 1], 1), :]
        return carry

    lax.fori_loop(0, tm, fetch, 0, unroll=8)
    half = x_ref.shape[1] // 2
    lo_a, hi_a = _unpack_bf16_halves(ga_ref[...])
    lo_b, hi_b = _unpack_bf16_halves(gb_ref[...])
    o_ref[:, :half] = x_ref[:, :half] + (lo_a + lo_b)
    o_ref[:, half:] = x_ref[:, half:] + (hi_a + hi_b)


def moe_combine(x2d, dest_flat, y_all, tt, tm=256):
    n, d = x2d.shape
    r_max = y_all.shape[1]
    per = tt // tm
    grid_spec = pltpu.PrefetchScalarGridSpec(
        num_scalar_prefetch=1, grid=(n // tt, per),
        in_specs=[pl.BlockSpec((tm, d), lambda i, j, u: (i * per + j, 0)),
                  pl.BlockSpec((1, r_max, d // 2), lambda i, j, u: (i, 0, 0))],
        out_specs=pl.BlockSpec((tm, d), lambda i, j, u: (i * per + j, 0)),
        scratch_shapes=[pltpu.VMEM((tm, d // 2), I32), pltpu.VMEM((tm, d // 2), I32)])
    return pl.pallas_call(
        functools.partial(_moe_combine_kernel, tm=tm, per=per), grid_spec=grid_spec,
        out_shape=jax.ShapeDtypeStruct((n, d), F32),
        compiler_params=_cparams(("arbitrary", "arbitrary")), name="moe_combine",
    )(dest_flat, x2d, y_all)


def _final_norm_kernel(x_ref, g_ref, o_ref):
    o_ref[...] = _rms(x_ref[...], g_ref[...])


def final_norm(x2d, g, tm=512):
    n, d = x2d.shape
    return pl.pallas_call(
        _final_norm_kernel, grid=(n // tm,),
        in_specs=[pl.BlockSpec((tm, d), lambda i: (i, 0)), pl.BlockSpec((1, d), lambda i: (0, 0))],
        out_specs=pl.BlockSpec((tm, d), lambda i: (i, 0)), out_shape=jax.ShapeDtypeStruct((n, d), F32),
        compiler_params=_cparams(("parallel",)), name="final_norm",
    )(x2d, g.reshape(1, d))


def moe_layer(x2d, norm_g, router_group, router_expert, w1, w3, w2):
    n, d = x2d.shape
    tt = min(MOE_TT, n)
    w_router = jnp.concatenate([router_group, jnp.moveaxis(router_expert, 0, 1).reshape(d, N_EXPERTS)], axis=1)
    h, ids, gates = moe_router(x2d, norm_g, _pad_cols(w_router, LANES))
    ids_t = jnp.pad(ids[:, :TOP_K_EXPERTS].T, ((0, 8 - TOP_K_EXPERTS), (0, 0)))
    gates_t = jnp.pad(gates[:, :TOP_K_EXPERTS].T, ((0, 8 - TOP_K_EXPERTS), (0, 0)))
    rank_t, dest, meta = moe_plan(ids, ids_t, tt)
    counts = meta[:, 0, :N_EXPERTS].reshape(-1)
    offs = meta[:, 1, :N_EXPERTS].reshape(-1)
    y_all = moe_ffn(h, rank_t, ids_t, gates_t, counts, offs, w1.astype(BF16), w3.astype(BF16), w2.astype(BF16), tt)
    return moe_combine(x2d, dest[:, :TOP_K_EXPERTS].reshape(-1), y_all, tt)


def even_layer(x, v_first, p):
    b, t, d = x.shape
    n = b * t
    w_in = p['w_in']
    aw = p['k_k'].shape[0]
    bw = p['w_kv_up'].shape[1] // 2
    rank = p['w_kv_up'].shape[0]
    n_idx = w_in.shape[1] - (3 * aw + bw + rank + bw + HEAD_DIM)
    o = 0
    segs = []
    for width in (3 * aw, bw, rank, bw, HEAD_DIM, n_idx):
        segs.append(w_in[:, o:o + width])
        o += width
    segs[4] = jnp.concatenate([segs[4], segs[4]], axis=1)
    segs[5] = _pad_cols(segs[5], LANES)
    rkv, q_b, c_kv, q_idx, k_dup, w_idx = norm_proj(
        x.reshape(n, d), p['norm'], [s.astype(BF16) for s in segs], [F32] * 6)

    r, lw, k_mod, v, kk, iclr, gate, bonus = rwkv_prep(x, rkv.reshape(b, t, 3 * aw), v_first, p)
    y = rwkv_scan(r, lw, k_mod, v, kk, iclr)

    cn, kn = kv_prep(c_kv, k_dup, p['kv_norm'], jnp.concatenate([p['idx_k_norm']] * 2))
    wk_t = p['w_kv_up'][:, :bw].astype(BF16)
    wv = p['w_kv_up'][:, bw:].astype(BF16)
    y_b = dsa_attention(q_idx.reshape(b, t, bw), w_idx.reshape(b, t, LANES), q_b.reshape(b, t, bw),
                        kn.reshape(b, t, LANES), cn.reshape(b, t, rank), wk_t, wv)

    x_new = even_out(x.reshape(n, d), y.reshape(n, aw), bonus.reshape(n, aw), gate.reshape(n, aw),
                     y_b.reshape(n, bw), p['ln_g'], p['ln_b'], p['w_out'])
    return x_new.reshape(b, t, d), v


def odd_layer(x, norm_g, w_qkv, w_out):
    b, t, d = x.shape
    n = b * t
    cw = w_qkv.shape[1] // 3
    tq = min(256, t)
    qk, v_t = qkv_proj(x, norm_g, w_qkv, tq)
    y_c = stickbreak_attention(qk.reshape(b, t, 2 * cw), v_t, cw // HEAD_DIM, tq)
    return proj_residual(x.reshape(n, d), y_c.reshape(n, cw), w_out.astype(BF16)).reshape(b, t, d)


@jax.jit
def _forward(x, ev_norm, ev_w_in, ev_w_out, a_mu_rkv, a_mu_lora, a_w0, a_w1, a_w2, a_a0, a_a1, a_a2, a_g1,
             a_g2, a_k_k, a_k_a, a_r_k, a_ln_g, a_ln_b, a_mu_vres, a_v0, a_v1, a_v2, b_kv_norm, b_w_kv_up,
             b_idx_k_norm, od_norm, od_w_qkv, od_w_out, moe_norm, moe_router_group, moe_router_expert,
             moe_w1, moe_w3, moe_w2, final_norm_g):
    b, t, d = x.shape
    depth = moe_norm.shape[0]
    v_first = None
    for i in range(depth):
        if i % 2 == 0:
            e = i // 2
            p = dict(norm=ev_norm[e], w_in=ev_w_in[e], w_out=ev_w_out[e], mu_rkv=a_mu_rkv[e], mu_lora=a_mu_lora[e],
                     w0=a_w0[e], w1=a_w1[e], w2=a_w2[e], a0=a_a0[e], a1=a_a1[e], a2=a_a2[e], g1=a_g1[e], g2=a_g2[e],
                     k_k=a_k_k[e], k_a=a_k_a[e], r_k=a_r_k[e].reshape(-1), ln_g=a_ln_g[e], ln_b=a_ln_b[e],
                     kv_norm=b_kv_norm[e], w_kv_up=b_w_kv_up[e], idx_k_norm=b_idx_k_norm[e])
            if e > 0:
                p.update(mu_v=a_mu_vres[e - 1], v0=a_v0[e - 1], v1=a_v1[e - 1], v2=a_v2[e - 1])
            x, v_used = even_layer(x, v_first if e > 0 else None, p)
            if e == 0:
                v_first = v_used
        else:
            o = i // 2
            x = odd_layer(x, od_norm[o], od_w_qkv[o], od_w_out[o])
        x = moe_layer(x.reshape(b * t, d), moe_norm[i], moe_router_group[i], moe_router_expert[i],
                      moe_w1[i], moe_w3[i], moe_w2[i]).reshape(b, t, d)
    return final_norm(x.reshape(b * t, d), final_norm_g).reshape(b, t, d)


def kernel(x, ev_norm, ev_w_in, ev_w_out, a_mu_rkv, a_mu_lora, a_w0, a_w1, a_w2, a_a0, a_a1, a_a2, a_g1, a_g2, a_k_k, a_k_a, a_r_k, a_ln_g, a_ln_b, a_mu_vres, a_v0, a_v1, a_v2, b_kv_norm, b_w_kv_up, b_idx_k_norm, od_norm, od_w_qkv, od_w_out, moe_norm, moe_router_group, moe_router_expert, moe_w1, moe_w3, moe_w2, final_norm):
    return _forward(x, ev_norm, ev_w_in, ev_w_out, a_mu_rkv, a_mu_lora, a_w0, a_w1, a_w2, a_a0, a_a1, a_a2, a_g1,
                    a_g2, a_k_k, a_k_a, a_r_k, a_ln_g, a_ln_b, a_mu_vres, a_v0, a_v1, a_v2, b_kv_norm, b_w_kv_up,
                    b_idx_k_norm, od_norm, od_w_qkv, od_w_out, moe_norm, moe_router_group, moe_router_expert,
                    moe_w1, moe_w3, moe_w2, final_norm)
```

```python
import functools
import math

import jax
import jax.numpy as jnp
from jax import lax
from jax.experimental import pallas as pl
from jax.experimental.pallas import tpu as pltpu

F32 = jnp.float32
BF16 = jnp.bfloat16
I32 = jnp.int32
HI = lax.Precision.HIGHEST

LANES = 128
HEAD_DIM = 64
PAIR = 2 * HEAD_DIM
RMS_EPS = 1e-6
GN_EPS = 64e-5
N_GROUPS = 4
EXPERTS_PER_GROUP = 8
N_EXPERTS = N_GROUPS * EXPERTS_PER_GROUP
TOP_K_EXPERTS = 2
IDX_TOPK_MAX = 256
INT_MIN = -2147483648
NEG_BIG = -1e30
VMEM_LIMIT = 56 * 1024 * 1024

_NT = (((1,), (1,)), ((), ()))
_TN = (((0,), (0,)), ((), ()))


def _cparams(sem):
    return pltpu.CompilerParams(dimension_semantics=sem, vmem_limit_bytes=VMEM_LIMIT)


def _rms(x, g):
    return x * lax.rsqrt(jnp.mean(x * x, axis=-1, keepdims=True) + RMS_EPS) * g


def _dot(a, b):
    return jnp.dot(a, b, preferred_element_type=F32)


def _dot_hi(a, b):
    return jnp.dot(a, b, preferred_element_type=F32, precision=HI)


def _seg_dot(x, seg):
    hi = x.astype(BF16)
    return _dot(hi, seg) + _dot((x - hi.astype(F32)).astype(BF16), seg)


def _dot_nt(a, b):
    return lax.dot_general(a, b, _NT, preferred_element_type=F32)


def _lane_is_first_head(width=PAIR):
    return lax.broadcasted_iota(I32, (1, width), 1) % PAIR < HEAD_DIM


def _norm_proj_kernel(x_ref, g_ref, *refs, n_seg):
    w_refs, o_refs = refs[:n_seg], refs[n_seg:]
    h = _rms(x_ref[...], g_ref[...]).astype(BF16)
    for w_ref, o_ref in zip(w_refs, o_refs):
        o_ref[...] = _dot(h, w_ref[...]).astype(o_ref.dtype)


def norm_proj(x2d, g, ws, out_dtypes, tm=256):
    n, d = x2d.shape
    in_specs = [pl.BlockSpec((tm, d), lambda i: (i, 0)), pl.BlockSpec((1, d), lambda i: (0, 0))]
    in_specs += [pl.BlockSpec(w.shape, lambda i: (0, 0)) for w in ws]
    out_specs = [pl.BlockSpec((tm, w.shape[1]), lambda i: (i, 0)) for w in ws]
    out_shape = [jax.ShapeDtypeStruct((n, w.shape[1]), dt) for w, dt in zip(ws, out_dtypes)]
    return pl.pallas_call(
        functools.partial(_norm_proj_kernel, n_seg=len(ws)),
        grid=(n // tm,), in_specs=in_specs, out_specs=out_specs, out_shape=out_shape,
        compiler_params=_cparams(("parallel",)), name="norm_proj",
    )(x2d, g.reshape(1, d), *ws)


def _shift_rows(z, carry_ref):
    tm = z.shape[0]
    first = lax.broadcasted_iota(I32, (tm, 1), 0) == 0
    prev = jnp.where(first, carry_ref[...], pltpu.roll(z, 1, 0))
    carry_ref[...] = z[tm - 1:tm, :]
    return prev


def _rwkv_prep_kernel(*refs, has_vres):
    it = iter(refs)
    x_ref, rkv_ref = next(it), next(it)
    vfirst_ref = next(it) if has_vres else None
    g_ref, mu_rkv_ref, mu_lora_ref = next(it), next(it), next(it)
    w0_ref, w1_ref, w2_ref = next(it), next(it), next(it)
    a0_ref, a1_ref, a2_ref = next(it), next(it), next(it)
    g1_ref, g2_ref = next(it), next(it)
    if has_vres:
        muv_ref, v0_ref, v1_ref, v2_ref = next(it), next(it), next(it), next(it)
    kk_w_ref, ka_ref, rk_ref, seg_ref = next(it), next(it), next(it), next(it)
    r_o, lw_o, k_o, v_o, kk_o, a_o, gate_o, bonus_o = (next(it) for _ in range(8))
    hcarry, rkvcarry = next(it), next(it)

    @pl.when(pl.program_id(1) == 0)
    def _():
        hcarry[...] = jnp.zeros_like(hcarry)
        rkvcarry[...] = jnp.zeros_like(rkvcarry)

    aw = r_o.shape[-1]
    h = _rms(x_ref[0], g_ref[...])
    dh = _shift_rows(h, hcarry) - h
    rkv = rkv_ref[0]
    rkv = rkv + (_shift_rows(rkv, rkvcarry) - rkv) * mu_rkv_ref[...]
    r, k, v = rkv[:, :aw], rkv[:, aw:2 * aw], rkv[:, 2 * aw:]

    def lora_in(row):
        return (h + dh * mu_lora_ref[row:row + 1, :]).astype(BF16)

    dec = w0_ref[...] + _dot(jnp.tanh(_dot(lora_in(0), w1_ref[...])).astype(BF16), w2_ref[...])
    logw = -math.exp(-0.5) * jax.nn.sigmoid(dec)
    iclr = jax.nn.sigmoid(a0_ref[...] + _dot(_dot(lora_in(1), a1_ref[...]).astype(BF16), a2_ref[...]))
    gate = _dot(jax.nn.sigmoid(_dot(lora_in(2), g1_ref[...])).astype(BF16), g2_ref[...])
    if has_vres:
        xv = (h + dh * muv_ref[...]).astype(BF16)
        mix = jax.nn.sigmoid(v0_ref[...] + _dot(_dot(xv, v1_ref[...]).astype(BF16), v2_ref[...]))
        v = v + (vfirst_ref[0] - v) * mix

    seg = seg_ref[...]
    kk = k * kk_w_ref[...]
    kk = kk * lax.rsqrt(jnp.maximum(_seg_dot(kk * kk, seg), 1e-12))
    k_mod = k * (1.0 + (iclr - 1.0) * ka_ref[...])
    bonus = _seg_dot(r * k_mod * rk_ref[...], seg) * v

    r_o[0], lw_o[0], k_o[0], v_o[0] = r, logw, k_mod, v
    kk_o[0], a_o[0], gate_o[0], bonus_o[0] = kk, iclr, gate, bonus


def _pad_cols(w, n):
    return jnp.pad(w, ((0, 0), (0, n - w.shape[1])))


def _pad_rows(w, n):
    return jnp.pad(w, ((0, n - w.shape[0]), (0, 0)))


def _seg_ones(width):
    i = jnp.arange(width) // HEAD_DIM
    return (i[:, None] == i[None, :]).astype(BF16)


def rwkv_prep(x, rkv, v_first, p, tm=256):
    b, t, d = x.shape
    aw = rkv.shape[-1] // 3
    has_vres = v_first is not None
    row = lambda a: a.reshape(1, -1)
    const = lambda a: pl.BlockSpec(a.shape, lambda i, j: (0,) * a.ndim)
    tile = lambda c: pl.BlockSpec((1, tm, c), lambda i, j: (i, j, 0))

    args, specs = [x, rkv], [tile(d), tile(3 * aw)]
    if has_vres:
        args.append(v_first)
        specs.append(tile(aw))
    consts = [row(p['norm']), p['mu_rkv'].reshape(1, 3 * aw), p['mu_lora'],
              row(p['w0']), _pad_cols(p['w1'], LANES).astype(BF16), _pad_rows(p['w2'], LANES).astype(BF16),
              row(p['a0']), _pad_cols(p['a1'], LANES).astype(BF16), _pad_rows(p['a2'], LANES).astype(BF16),
              p['g1'].astype(BF16), p['g2'].astype(BF16)]
    if has_vres:
        consts += [row(p['mu_v']), row(p['v0']), _pad_cols(p['v1'], LANES).astype(BF16),
                   _pad_rows(p['v2'], LANES).astype(BF16)]
    consts += [row(p['k_k']), row(p['k_a']), row(p['r_k']), _seg_ones(aw)]
    args += consts
    specs += [const(a) for a in consts]
    out_shape = [jax.ShapeDtypeStruct((b, t, aw), F32)] * 8
    return pl.pallas_call(
        functools.partial(_rwkv_prep_kernel, has_vres=has_vres),
        grid=(b, t // tm), in_specs=specs, out_specs=[tile(aw)] * 8, out_shape=out_shape,
        scratch_shapes=[pltpu.VMEM((1, d), F32), pltpu.VMEM((1, 3 * aw), F32)],
        compiler_params=_cparams(("arbitrary", "arbitrary")), name="rwkv_prep",
    )(*args)


def _split_bf16(x):
    hi = x.astype(BF16)
    return hi, (x - hi.astype(F32)).astype(BF16)


def _mm3(a, b, dims=None):
    a_hi, a_lo = _split_bf16(a)
    b_hi, b_lo = _split_bf16(b)
    if dims is None:
        f = lambda p, q: jnp.dot(p, q, preferred_element_type=F32)
    else:
        f = lambda p, q: lax.dot_general(p, q, dims, preferred_element_type=F32)
    return f(a_hi, b_hi) + f(a_hi, b_lo) + f(a_lo, b_hi)


def _mmc(a, b, dims=None):
    a, b = a.astype(BF16), b.astype(BF16)
    if dims is None:
        return jnp.dot(a, b, preferred_element_type=F32)
    return lax.dot_general(a, b, dims, preferred_element_type=F32)


def _cumsum_rows(x):
    n = x.shape[0]
    row = lax.broadcasted_iota(I32, (n, 1), 0)
    s = 1
    while s < n:
        x = x + jnp.where(row >= s, pltpu.roll(x, s, 0), 0.0)
        s *= 2
    return x


def _rwkv_chunk_kernel(r_ref, lw_ref, k_ref, v_ref, kk_ref, a_ref, qh_ref, y0_ref, m_ref, c0_ref, *, chunk, cps):
    L = chunk
    n_pair = r_ref.shape[-1] // PAIR
    row = lax.broadcasted_iota(I32, (L, L), 0)
    col = lax.broadcasted_iota(I32, (L, L), 1)
    strict = col < row
    incl = col <= row
    eye = (col == row).astype(F32)
    first = _lane_is_first_head()
    p_row = lax.broadcasted_iota(I32, (PAIR, PAIR), 0)
    p_col = lax.broadcasted_iota(I32, (PAIR, PAIR), 1)
    blockdiag = p_row // HEAD_DIM == p_col // HEAD_DIM
    pair_eye = p_row == p_col

    units = [(ci, p) for ci in range(cps) for p in range(n_pair)]
    heads = (first, ~first)
    pre = {}
    for u in units:
        ci, p = u
        rows = slice(ci * L, (ci + 1) * L)
        sl = slice(p * PAIR, (p + 1) * PAIR)
        lw = lw_ref[0, rows, sl]
        c = _cumsum_rows(lw)
        g = jnp.exp(c)
        g_inv = jnp.exp(-c)
        kk = kk_ref[0, rows, sl]
        pre[u] = dict(rows=rows, sl=sl, g_last=g[L - 1:L, :],
                      at=-kk * jnp.exp(c - lw),
                      bt=kk * a_ref[0, rows, sl] * g_inv,
                      kt=k_ref[0, rows, sl] * g_inv,
                      rt=r_ref[0, rows, sl] * g,
                      vv=v_ref[0, rows, sl])
    uh = [(u, h) for u in units for h in range(2)]

    gm = {}
    for u, h in uh:
        d = pre[u]
        lhs = jnp.concatenate([jnp.where(heads[h], d['at'], 0.0), jnp.where(heads[h], d['rt'], 0.0)], axis=0)
        gm[u, h] = _mm3(lhs, jnp.concatenate([d['bt'], d['kt']], axis=0), _NT)
    a_ak = {x: jnp.where(strict, gm[x][:L, L:], 0.0) for x in uh}
    a_rb = {x: jnp.where(incl, gm[x][L:, :L], 0.0) for x in uh}
    a_rk = {x: jnp.where(incl, gm[x][L:, L:], 0.0) for x in uh}
    apow = {x: jnp.where(strict, gm[x][:L, :L], 0.0) for x in uh}
    inv = {x: eye + apow[x] for x in uh}
    for _ in range(max(1, (L - 1).bit_length()) - 1):
        apow = {x: _mmc(apow[x], apow[x]) for x in uh}
        inv = {x: inv[x] + _mmc(inv[x], apow[x]) for x in uh}
    w_h = {(u, h): _mmc(inv[u, h], pre[u]['at']) for u, h in uh}
    akv = {(u, h): _mmc(a_ak[u, h], pre[u]['vv']) for u, h in uh}
    u0_h = {x: _mmc(inv[x], akv[x]) for x in uh}
    w = {u: jnp.where(first, w_h[u, 0], w_h[u, 1]) for u in units}
    u0 = {u: jnp.where(first, u0_h[u, 0], u0_h[u, 1]) for u in units}
    wu = {u: jnp.concatenate([w[u], u0[u]], axis=1) for u in units}
    arb = {(u, h): _mmc(a_rb[u, h], wu[u]) for u, h in uh}
    ark = {(u, h): _mmc(a_rk[u, h], pre[u]['vv']) for u, h in uh}
    first2 = jnp.concatenate([first, first], axis=1)
    for u in units:
        d = pre[u]
        ci, p = u
        arb_w = jnp.where(first2, arb[u, 0], arb[u, 1])
        qh_ref[0, d['rows'], d['sl']] = d['rt'] + arb_w[:, :PAIR]
        y0_ref[0, d['rows'], d['sl']] = arb_w[:, PAIR:] + jnp.where(first, ark[u, 0], ark[u, 1])
        bg = d['bt'] * d['g_last']
        m_kk = _mmc(bg, w[u], _TN)
        m_ref[0, ci, p] = jnp.where(pair_eye, d['g_last'], 0.0) + jnp.where(blockdiag, m_kk, 0.0)
        c0 = _mmc(jnp.concatenate([u0[u], d['vv']], axis=0),
                  jnp.concatenate([bg, d['kt'] * d['g_last']], axis=0), _TN)
        c0_ref[0, ci, p] = jnp.where(blockdiag, c0, 0.0)


def _rwkv_state_kernel(qh_ref, y0_ref, m_ref, c0_ref, y_ref, s_ref, *, chunk, cps):
    L = chunk

    @pl.when(pl.program_id(0) == 0)
    def _():
        s_ref[...] = jnp.zeros_like(s_ref)

    n_batch = qh_ref.shape[0]
    n_pair = qh_ref.shape[-1] // PAIR
    streams = [(b, p) for b in range(n_batch) for p in range(n_pair)]
    s2 = {x: s_ref[x[0], x[1]] for x in streams}
    for ci in range(cps):
        rows = slice(ci * L, (ci + 1) * L)
        for b, p in streams:
            sl = slice(p * PAIR, (p + 1) * PAIR)
            y_ref[b, rows, sl] = _mm3(qh_ref[b, rows, sl], s2[b, p], _NT) + y0_ref[b, rows, sl]
            s2[b, p] = _mm3(s2[b, p], m_ref[b, ci, p], _NT) + c0_ref[b, ci, p]
    for b, p in streams:
        s_ref[b, p] = s2[b, p]


def rwkv_scan(r, lw, k, v, kk, a, chunk=64, cps_a=4, cps_b=4):
    b, t, aw = r.shape
    n_pair = aw // PAIR
    n_chunk = t // chunk
    cps_a, cps_b = min(cps_a, n_chunk), min(cps_b, n_chunk)
    rows = lambda c: pl.BlockSpec((1, c * chunk, aw), lambda i, j: (i, j, 0))
    mats = lambda c: pl.BlockSpec((1, c, n_pair, PAIR, PAIR), lambda i, j: (i, j, 0, 0, 0))
    seq = jax.ShapeDtypeStruct((b, t, aw), F32)
    mat = jax.ShapeDtypeStruct((b, n_chunk, n_pair, PAIR, PAIR), F32)
    qh, y0, m, c0 = pl.pallas_call(
        functools.partial(_rwkv_chunk_kernel, chunk=chunk, cps=cps_a),
        grid=(b, n_chunk // cps_a), in_specs=[rows(cps_a)] * 6,
        out_specs=[rows(cps_a), rows(cps_a), mats(cps_a), mats(cps_a)], out_shape=[seq, seq, mat, mat],
        compiler_params=_cparams(("parallel", "parallel")), name="rwkv_chunk",
    )(r, lw, k, v, kk, a)
    rows_b = pl.BlockSpec((b, cps_b * chunk, aw), lambda j: (0, j, 0))
    mats_b = pl.BlockSpec((b, cps_b, n_pair, PAIR, PAIR), lambda j: (0, j, 0, 0, 0))
    return pl.pallas_call(
        functools.partial(_rwkv_state_kernel, chunk=chunk, cps=cps_b),
        grid=(n_chunk // cps_b,), in_specs=[rows_b, rows_b, mats_b, mats_b],
        out_specs=rows_b, out_shape=seq,
        scratch_shapes=[pltpu.VMEM((b, n_pair, PAIR, PAIR), F32)],
        compiler_params=_cparams(("arbitrary",)), name="rwkv_state",
    )(qh, y0, m, c0)


def _kv_prep_kernel(c_ref, k_ref, gc_ref, gk_ref, cn_ref, kn_ref):
    cn_ref[...] = _rms(c_ref[...], gc_ref[...]).astype(cn_ref.dtype)
    kn_ref[...] = _rms(k_ref[...], gk_ref[...]).astype(kn_ref.dtype)


def kv_prep(c_kv, k_dup, g_c, g_k_dup, tm=512):
    n, w = c_kv.shape
    tile = pl.BlockSpec((tm, w), lambda i: (i, 0))
    const = pl.BlockSpec((1, w), lambda i: (0, 0))
    return pl.pallas_call(
        _kv_prep_kernel, grid=(n // tm,), in_specs=[tile, tile, const, const], out_specs=[tile, tile],
        out_shape=[jax.ShapeDtypeStruct((n, w), BF16)] * 2,
        compiler_params=_cparams(("parallel",)), name="kv_prep",
    )(c_kv, k_dup, g_c.reshape(1, w), g_k_dup.reshape(1, w))


SEARCH_VALUE_STEPS = 24
SEARCH_STEPS_PER_CHECK = 2


def _tree_reduce(combine, xs):
    while len(xs) > 1:
        xs = [combine(xs[i], xs[i + 1]) for i in range(0, len(xs) - 1, 2)] + ([xs[-1]] if len(xs) % 2 else [])
    return xs[0]


def _key_to_f32(key):
    return pltpu.bitcast(jnp.where(key >= 0, key, key ^ 0x7FFFFFFF), F32)


def _f32_to_key(x):
    bits = pltpu.bitcast(x, I32)
    return jnp.where(bits >= 0, bits, bits ^ 0x7FFFFFFF)


def _dsa_kernel(qi_ref, wit_ref, q_ref, kn_ref, cn_ref, cnt_ref, wk_ref, wv_ref, o_ref,
                skey_ref, qpt_ref, acc_ref, m_ref, l_ref, *, qb, kt, ka, topk, n_heads, idx_bits):
    blk = pl.program_id(1)
    nt = ((blk + 1) * qb + kt - 1) // kt
    first = _lane_is_first_head()
    first_sub = lax.broadcasted_iota(I32, (PAIR, 1), 0) < HEAD_DIM
    q_pos = blk * qb + lax.broadcasted_iota(I32, (1, qb), 1)
    row0 = lax.broadcasted_iota(I32, (kt, 1), 0)
    heads = range(n_heads)

    qi = qi_ref[0]
    wit = wit_ref[0]
    qi_t = []
    for p in range(n_heads // 2):
        pair_t = qi[:, p * PAIR:(p + 1) * PAIR].T * HEAD_DIM ** -0.5
        qi_t += [jnp.where(first_sub, pair_t, 0.0).astype(BF16), jnp.where(first_sub, 0.0, pair_t).astype(BF16)]

    def score_tile(j, carry):
        off = pl.multiple_of(j * kt, kt)
        kn = kn_ref[0, pl.ds(off, kt), :]
        sc = jnp.zeros((kt, qb), F32)
        for h in heads:
            sc = sc + wit[h:h + 1, :] * jnp.maximum(_dot(kn, qi_t[h]), 0.0)
        sc = sc * n_heads ** -0.5
        key = jnp.where(sc == 0.0, 0, _f32_to_key(sc))
        causal = off + row0 <= q_pos
        skey_ref[j] = jnp.where(causal, key, INT_MIN)
        hi8 = _tree_reduce(jnp.maximum, list(jnp.where(causal, sc, -jnp.inf).reshape(kt // 8, 8, qb)))
        lo8 = _tree_reduce(jnp.minimum, list(jnp.where(causal, sc, jnp.inf).reshape(kt // 8, 8, qb)))
        return jnp.maximum(carry[0], hi8), jnp.minimum(carry[1], lo8)

    hi8, lo8 = lax.fori_loop(0, nt, score_tile,
                             (jnp.full((8, qb), -jnp.inf, F32), jnp.full((8, qb), jnp.inf, F32)))

    def fold(tile_fn, init, combine):
        def body(j, acc):
            part = tile_fn(skey_ref[j], j * kt + row0).reshape(kt // 8, 8, qb)
            return combine(acc, _tree_reduce(combine, [part[i] for i in range(kt // 8)]))
        return lax.fori_loop(0, nt, body, jnp.full((8, qb), init, F32))

    def count(pred):
        acc = fold(lambda sk, idx: jnp.where(pred(sk, idx), 1.0, 0.0), 0.0, jnp.add)
        return jnp.sum(acc, axis=0, keepdims=True)

    few = q_pos + 1 <= topk
    v_max = jnp.max(hi8, axis=0, keepdims=True)
    v_min = jnp.min(lo8, axis=0, keepdims=True)
    lo0 = _f32_to_key(jnp.where(v_min == 0.0, 0.0, v_min))
    hi0 = _f32_to_key(jnp.where(v_max == 0.0, 0.0, v_max)) + 1
    cnt0 = (q_pos + 1).astype(F32)

    def unresolved(lo, hi, cnt_lo):
        return ~(few | (cnt_lo == topk) | (hi <= lo + 1))

    def any_open(lo, hi, cnt_lo):
        return jnp.max(jnp.where(unresolved(lo, hi, cnt_lo), 1.0, 0.0))

    def search_cond(st):
        return (st[0] < SEARCH_VALUE_STEPS + 34) & (st[4] > 0.0)

    def search_step(st):
        it, lo, hi, cnt_lo, _ = st
        for _ in range(SEARCH_STEPS_PER_CHECK):
            mid_v = _f32_to_key(0.5 * _key_to_f32(lo) + 0.5 * _key_to_f32(hi))
            mid_k = (lo >> 1) + (hi >> 1) + (lo & hi & 1)
            mid = jnp.where((it < SEARCH_VALUE_STEPS) & (mid_v > lo) & (mid_v < hi), mid_v, mid_k)
            cnt = count(lambda sk, _: sk >= mid)
            open_ = unresolved(lo, hi, cnt_lo)
            up = open_ & (cnt >= topk)
            lo, cnt_lo = jnp.where(up, mid, lo), jnp.where(up, cnt, cnt_lo)
            hi = jnp.where(open_ & ~up, mid, hi)
            it = it + 1
        return it, lo, hi, cnt_lo, any_open(lo, hi, cnt_lo)

    st0 = (jnp.int32(0), lo0, hi0, cnt0, any_open(lo0, hi0, cnt0))
    _, lo, _, cnt_lo, _ = lax.while_loop(search_cond, search_step, st0)
    thr = jnp.where(few, INT_MIN, lo)
    tie = (cnt_lo > topk) & ~few
    any_tie = jnp.max(jnp.where(tie, 1, 0))
    n_gt = lax.fori_loop(0, any_tie, lambda i, c: count(lambda sk, _: sk > thr), jnp.zeros((1, qb), F32))
    need = topk - n_gt

    def idx_bit(i, prefix):
        cand = prefix | jnp.left_shift(jnp.int32(1), idx_bits - 1 - i)
        cnt = count(lambda sk, kidx: (sk == thr) & (kidx < cand))
        return jnp.where(cnt < need, cand, prefix)

    jmax = lax.fori_loop(0, idx_bits * any_tie, idx_bit, jnp.zeros((1, qb), I32))
    jmax = jnp.where(tie, jmax, 0x7FFFFFFF)

    q = q_ref[0]
    for h in heads:
        pr = slice((h // 2) * PAIR, (h // 2 + 1) * PAIR)
        wk_h = jnp.where(first if h % 2 == 0 else ~first, wk_ref[:, pr], 0.0)
        qpt_ref[h] = (_dot_nt(wk_h, q[:, pr].astype(BF16)) * (HEAD_DIM ** -0.5 * math.log2(math.e))).astype(BF16)
    m_ref[...] = jnp.full_like(m_ref, NEG_BIG)
    l_ref[...] = jnp.zeros_like(l_ref)
    acc_ref[...] = jnp.zeros_like(acc_ref)

    def attn_tile(j, carry):
        sk_all = skey_ref[j]
        for part in range(kt // ka):
            off = pl.multiple_of(j * kt + part * ka, ka)
            sk = sk_all[part * ka:(part + 1) * ka, :]
            kidx = off + row0[:ka]
            sel = ((sk > thr) | ((sk == thr) & (kidx <= jmax))) & (kidx <= q_pos)
            bias = jnp.where(sel, 0.0, NEG_BIG)
            cn = cn_ref[0, pl.ds(off, ka), :]
            cn_t = cnt_ref[0, j * (kt // ka) + part]
            s = [_dot(cn, qpt_ref[h]) + bias for h in heads]
            m_old = [m_ref[h] for h in heads]
            m_new = [jnp.maximum(m_old[h], jnp.max(s[h], axis=0, keepdims=True)) for h in heads]
            pexp = [jnp.exp2(s[h] - m_new[h]) for h in heads]
            alpha = [jnp.exp2(m_old[h] - m_new[h]) for h in heads]
            pv = [_dot(cn_t, pexp[h].astype(BF16)) for h in heads]
            for h in heads:
                l_ref[h] = alpha[h] * l_ref[h] + jnp.sum(pexp[h], axis=0, keepdims=True)
                acc_ref[h] = alpha[h] * acc_ref[h] + pv[h]
                m_ref[h] = m_new[h]
        return carry

    lax.fori_loop(0, nt, attn_tile, 0)

    rank = cn_ref.shape[-1]
    for p in range(n_heads // 2):
        pr = slice(p * PAIR, (p + 1) * PAIR)
        out = jnp.zeros((qb, PAIR), F32)
        for h in (2 * p, 2 * p + 1):
            lat_t = (acc_ref[h] / l_ref[h]).astype(BF16)
            wv_h = jnp.where(first if h % 2 == 0 else ~first, wv_ref[:, pr], 0.0)
            out = out + lax.dot_general(lat_t, wv_h, _TN, preferred_element_type=F32)
        o_ref[0, :, pr] = out.astype(o_ref.dtype)


def dsa_attention(q_idx, w_idx, q_b, kn, cn, wk_t, wv, qb=128, kt=512, ka=256):
    b, t, width = q_b.shape
    n_heads = width // HEAD_DIM
    rank = cn.shape[-1]
    kt, ka = min(kt, t), min(ka, t)
    topk = min(IDX_TOPK_MAX, t // 4)
    w_t = jnp.swapaxes(w_idx[:, :, :n_heads], 1, 2)
    cn_t = jnp.swapaxes(cn.reshape(b, t // ka, ka, rank), 2, 3)
    qtile = lambda c: pl.BlockSpec((1, qb, c), lambda i, j: (i, j, 0))
    full = lambda c: pl.BlockSpec((1, t, c), lambda i, j: (i, 0, 0))
    const = lambda a: pl.BlockSpec(a.shape, lambda i, j: (0, 0))
    kern = functools.partial(_dsa_kernel, qb=qb, kt=kt, ka=ka, topk=topk, n_heads=n_heads,
                             idx_bits=max(1, (t - 1).bit_length()))
    return pl.pallas_call(
        kern, grid=(b, t // qb),
        in_specs=[qtile(width), pl.BlockSpec((1, n_heads, qb), lambda i, j: (i, 0, j)), qtile(width),
                  full(LANES), full(rank), pl.BlockSpec((1, t // ka, rank, ka), lambda i, j: (i, 0, 0, 0)),
                  const(wk_t), const(wv)],
        out_specs=qtile(width), out_shape=jax.ShapeDtypeStruct((b, t, width), BF16),
        scratch_shapes=[pltpu.VMEM((t // kt, kt, qb), I32), pltpu.VMEM((n_heads, rank, qb), BF16),
                        pltpu.VMEM((n_heads, rank, qb), F32), pltpu.VMEM((n_heads, 1, qb), F32),
                        pltpu.VMEM((n_heads, 1, qb), F32)],
        compiler_params=_cparams(("parallel", "arbitrary")), name="dsa_attention",
    )(q_idx, w_t, q_b, kn, cn, cn_t, wk_t, wv)


def _even_out_kernel(x_ref, y_ref, bonus_ref, gate_ref, yb_ref, lng_ref, lnb_ref, seg_ref, wa_ref, wb_ref, o_ref):
    seg = seg_ref[...]
    y = y_ref[...]
    yc = y - _seg_dot(y, seg) * (1.0 / HEAD_DIM)
    yn = yc * lax.rsqrt(_seg_dot(yc * yc, seg) * (1.0 / HEAD_DIM) + GN_EPS)
    ya = (yn * lng_ref[...] + lnb_ref[...] + bonus_ref[...]) * gate_ref[...]
    o_ref[...] = x_ref[...] + _dot(ya.astype(BF16), wa_ref[...]) + _dot(yb_ref[...], wb_ref[...])


def even_out(x2d, y, bonus, gate, y_b, ln_g, ln_b, w_out, tm=256):
    n, d = x2d.shape
    aw = y.shape[1]
    tile = lambda c: pl.BlockSpec((tm, c), lambda i: (i, 0))
    const = lambda a: pl.BlockSpec(a.shape, lambda i: (0, 0))
    consts = [ln_g.reshape(1, aw), ln_b.reshape(1, aw), _seg_ones(aw),
              w_out[:aw].astype(BF16), w_out[aw:].astype(BF16)]
    return pl.pallas_call(
        _even_out_kernel, grid=(n // tm,),
        in_specs=[tile(d), tile(aw), tile(aw), tile(aw), tile(y_b.shape[1])] + [const(a) for a in consts],
        out_specs=tile(d), out_shape=jax.ShapeDtypeStruct((n, d), F32),
        compiler_params=_cparams(("parallel",)), name="even_out",
    )(x2d, y, bonus, gate, y_b, *consts)


SB_PAIRS = 2


def _stickbreak_kernel(q_ref, k_ref, vt_ref, o_ref, acc_ref, carry_ref, *, tq, n_pairs):
    qi = pl.program_id(2)
    first_sub = lax.broadcasted_iota(I32, (PAIR, 1), 0) < HEAD_DIM
    key_i = lax.broadcasted_iota(I32, (tq, tq), 0)
    qry_i = lax.broadcasted_iota(I32, (tq, tq), 1)
    before = key_i < qry_i
    later = (qry_i > key_i).astype(BF16)
    heads = [(p, x) for p in range(n_pairs) for x in range(2)]
    q_h = {}
    for p in range(n_pairs):
        q_t = q_ref[0, :, p * PAIR:(p + 1) * PAIR].astype(F32).T * HEAD_DIM ** -0.5
        q_h[p, 0] = jnp.where(first_sub, q_t, 0.0).astype(BF16)
        q_h[p, 1] = jnp.where(first_sub, 0.0, q_t).astype(BF16)
    acc_ref[...] = jnp.zeros_like(acc_ref)
    carry_ref[...] = jnp.zeros_like(carry_ref)

    def tile(j, diag):
        off = pl.multiple_of(j * tq, tq)
        k2 = [k_ref[0, pl.ds(off, tq), p * PAIR:(p + 1) * PAIR] for p in range(n_pairs)]
        z = {h: _dot(k2[h[0]], q_h[h]) * math.log2(math.e) for h in heads}
        lk = {h: -(jnp.maximum(z[h], 0.0) + jnp.log2(1.0 + jnp.exp2(-jnp.abs(z[h])))) for h in heads}
        if diag:
            lk = {h: jnp.where(before, lk[h], 0.0) for h in heads}
        hi = {h: lk[h].astype(BF16) for h in heads}
        lo = {h: (lk[h] - hi[h].astype(F32)).astype(BF16) for h in heads}
        tail = {h: _dot(later, hi[h]) + _dot(later, lo[h]) + carry_ref[h[0], h[1]] for h in heads}
        w = {h: jnp.exp2(z[h] + lk[h] + tail[h]) for h in heads}
        if diag:
            w = {h: jnp.where(before, w[h], 0.0) for h in heads}
        for p in range(n_pairs):
            v_t = vt_ref[0, p, j]
            acc_ref[p] += (_dot(jnp.where(first_sub, v_t, 0.0), w[p, 0].astype(BF16))
                           + _dot(jnp.where(first_sub, 0.0, v_t), w[p, 1].astype(BF16)))
        for h in heads:
            carry_ref[h[0], h[1]] += jnp.sum(lk[h], axis=0, keepdims=True)
        return (jnp.max(carry_ref[...]) > -110.0 * math.log2(math.e)).astype(I32)

    live = tile(qi, True)

    def cond(st):
        return (st[0] >= 0) & (st[1] > 0)

    def body(st):
        return st[0] - 1, tile(st[0], False)

    lax.while_loop(cond, body, (qi - 1, live))
    for p in range(n_pairs):
        o_ref[0, :, p * PAIR:(p + 1) * PAIR] = acc_ref[p].T.astype(o_ref.dtype)


def _qkv_proj_kernel(x_ref, g_ref, wqk_ref, wv_ref, qk_ref, vt_ref):
    h = _rms(x_ref[...], g_ref[...]).astype(BF16)
    qk_ref[...] = _dot(h, wqk_ref[...]).astype(qk_ref.dtype)
    v = _dot(h, wv_ref[...])
    for p in range(v.shape[1] // PAIR):
        vt_ref[0, p, 0] = v[:, p * PAIR:(p + 1) * PAIR].T.astype(vt_ref.dtype)


def qkv_proj(x, g, w_qkv, tq):
    b, t, d = x.shape
    width = w_qkv.shape[1] // 3
    n_pair = width // PAIR
    nq = t // tq
    wqk, wv = w_qkv[:, :2 * width].astype(BF16), w_qkv[:, 2 * width:].astype(BF16)
    return pl.pallas_call(
        _qkv_proj_kernel, grid=(b * nq,),
        in_specs=[pl.BlockSpec((tq, d), lambda i: (i, 0)), pl.BlockSpec((1, d), lambda i: (0, 0)),
                  pl.BlockSpec(wqk.shape, lambda i: (0, 0)), pl.BlockSpec(wv.shape, lambda i: (0, 0))],
        out_specs=[pl.BlockSpec((tq, 2 * width), lambda i: (i, 0)),
                   pl.BlockSpec((1, n_pair, 1, PAIR, tq), lambda i: (i // nq, 0, i % nq, 0, 0))],
        out_shape=[jax.ShapeDtypeStruct((b * t, 2 * width), BF16),
                   jax.ShapeDtypeStruct((b, n_pair, nq, PAIR, tq), BF16)],
        compiler_params=_cparams(("parallel",)), name="qkv_proj",
    )(x.reshape(b * t, d), g.reshape(1, d), wqk, wv)


def stickbreak_attention(qk, v_t, n_heads, tq):
    b, t, _ = qk.shape
    n_pair = n_heads // 2
    width = n_heads * HEAD_DIM
    gw = SB_PAIRS * PAIR
    n_grp = n_pair // SB_PAIRS
    qkv = qk
    q_spec = pl.BlockSpec((1, tq, gw), lambda i, p, j: (i, j, p))
    k_spec = pl.BlockSpec((1, t, gw), lambda i, p, j: (i, 0, n_grp + p))
    v_spec = pl.BlockSpec((1, SB_PAIRS, t // tq, PAIR, tq), lambda i, p, j: (i, p, 0, 0, 0))
    return pl.pallas_call(
        functools.partial(_stickbreak_kernel, tq=tq, n_pairs=SB_PAIRS), grid=(b, n_grp, t // tq),
        in_specs=[q_spec, k_spec, v_spec], out_specs=q_spec,
        out_shape=jax.ShapeDtypeStruct((b, t, width), BF16),
        scratch_shapes=[pltpu.VMEM((SB_PAIRS, PAIR, tq), F32), pltpu.VMEM((SB_PAIRS, 2, 1, tq), F32)],
        compiler_params=_cparams(("parallel", "parallel", "arbitrary")), name="stickbreak",
    )(qkv, qkv, v_t)


def _proj_residual_kernel(x_ref, y_ref, w_ref, o_ref):
    o_ref[...] = x_ref[...] + _dot(y_ref[...], w_ref[...])


def proj_residual(x2d, y, w, tm=256):
    n, d = x2d.shape
    c = y.shape[1]
    return pl.pallas_call(
        _proj_residual_kernel, grid=(n // tm,),
        in_specs=[pl.BlockSpec((tm, d), lambda i: (i, 0)), pl.BlockSpec((tm, c), lambda i: (i, 0)),
                  pl.BlockSpec(w.shape, lambda i: (0, 0))],
        out_specs=pl.BlockSpec((tm, d), lambda i: (i, 0)), out_shape=jax.ShapeDtypeStruct((n, d), F32),
        compiler_params=_cparams(("parallel",)), name="proj_residual",
    )(x2d, y, w)


def _router_kernel(x_ref, g_ref, wr_hi_ref, wr_lo_ref, h_ref, ids_ref, gates_ref):
    h = _rms(x_ref[...], g_ref[...])
    h_hi = h.astype(BF16)
    h_ref[...] = h_hi
    h_lo = (h - h_hi.astype(F32)).astype(BF16)
    logits = _dot(h_hi, wr_hi_ref[...]) + _dot(h_hi, wr_lo_ref[...]) + _dot(h_lo, wr_hi_ref[...])
    lane = lax.broadcasted_iota(I32, logits.shape, 1)
    lane_f = lane.astype(F32)

    def top1(vals):
        mx = jnp.max(vals, axis=1, keepdims=True)
        return mx, jnp.min(jnp.where(vals == mx, lane_f, 1e9), axis=1, keepdims=True).astype(I32)

    is_g = lane < N_GROUPS
    g_max, g_sel = top1(jnp.where(is_g, logits, -jnp.inf))
    g_gate = 1.0 / jnp.sum(jnp.where(is_g, jnp.exp(logits - g_max), 0.0), axis=1, keepdims=True)
    e_lane = lane - N_GROUPS
    in_grp = (e_lane >= g_sel * EXPERTS_PER_GROUP) & (e_lane < (g_sel + 1) * EXPERTS_PER_GROUP)
    el = jnp.where(in_grp, logits, -jnp.inf)
    m1, i1 = top1(el)
    m2, i2 = top1(jnp.where(lane == i1, -jnp.inf, el))
    e21 = jnp.exp(m2 - m1)
    p1 = 1.0 / (1.0 + e21)
    ids_ref[...] = jnp.where(lane == 0, i1 - N_GROUPS, jnp.where(lane == 1, i2 - N_GROUPS, 0))
    gates_ref[...] = jnp.where(lane == 0, g_gate * p1, jnp.where(lane == 1, g_gate * (e21 * p1), 0.0))


def moe_router(x2d, g, w_router, tm=256):
    n, d = x2d.shape
    tile = lambda c: pl.BlockSpec((tm, c), lambda i: (i, 0))
    w_spec = pl.BlockSpec(w_router.shape, lambda i: (0, 0))
    w_hi = w_router.astype(BF16)
    w_lo = (w_router - w_hi.astype(F32)).astype(BF16)
    return pl.pallas_call(
        _router_kernel, grid=(n // tm,),
        in_specs=[tile(d), pl.BlockSpec((1, d), lambda i: (0, 0)), w_spec, w_spec],
        out_specs=[tile(d), tile(LANES), tile(LANES)],
        out_shape=[jax.ShapeDtypeStruct((n, d), BF16), jax.ShapeDtypeStruct((n, LANES), I32),
                   jax.ShapeDtypeStruct((n, LANES), F32)],
        compiler_params=_cparams(("parallel",)), name="moe_router",
    )(x2d, g.reshape(1, d), w_hi, w_lo)


MOE_TT = 2048
MOE_CH = 128
MOE_KC = 512
_PLAN_BLK = 256


def _moe_plan_kernel(ids_ref, idst_ref, rank_ref, dest_ref, meta_ref, *, tt):
    nb = tt // _PLAN_BLK
    row = lax.broadcasted_iota(I32, (_PLAN_BLK, _PLAN_BLK), 0)
    col = lax.broadcasted_iota(I32, (_PLAN_BLK, _PLAN_BLK), 1)
    earlier_t = (row < col).astype(BF16)
    earlier = (col < row).astype(BF16)
    e_sub = lax.broadcasted_iota(I32, (N_EXPERTS, _PLAN_BLK), 0)
    carry_t = jnp.zeros((N_EXPERTS, 1), F32)
    for b in range(nb):
        sl = slice(b * _PLAN_BLK, (b + 1) * _PLAN_BLK)
        member = (e_sub == idst_ref[0:1, sl]) | (e_sub == idst_ref[1:2, sl])
        m_t = jnp.where(member, 1.0, 0.0)
        before = _dot(m_t.astype(BF16), earlier_t) + carry_t
        rank_ref[0, :, sl] = jnp.where(member, before, -1.0)
        carry_t = carry_t + jnp.sum(m_t, axis=1, keepdims=True)
    lane = lax.broadcasted_iota(I32, (_PLAN_BLK, LANES), 1)
    carry = jnp.zeros((1, LANES), F32)
    prefix = []
    for b in range(nb):
        sl = slice(b * _PLAN_BLK, (b + 1) * _PLAN_BLK)
        m = jnp.where((lane == ids_ref[sl, 0:1]) | (lane == ids_ref[sl, 1:2]), 1.0, 0.0)
        prefix.append(_dot(earlier, m.astype(BF16)) + carry)
        carry = carry + jnp.sum(m, axis=0, keepdims=True)
    counts = carry
    padded = jnp.ceil(counts * (1.0 / MOE_CH)) * MOE_CH
    l_row = lax.broadcasted_iota(I32, (LANES, LANES), 0)
    l_col = lax.broadcasted_iota(I32, (LANES, LANES), 1)
    offs = _dot_hi(padded, (l_row < l_col).astype(F32))
    for b in range(nb):
        sl = slice(b * _PLAN_BLK, (b + 1) * _PLAN_BLK)
        where_row = prefix[b] + offs
        d0 = jnp.sum(jnp.where(lane == ids_ref[sl, 0:1], where_row, 0.0), axis=1, keepdims=True)
        d1 = jnp.sum(jnp.where(lane == ids_ref[sl, 1:2], where_row, 0.0), axis=1, keepdims=True)
        dest_ref[sl, :] = jnp.where(lane == 0, d0, jnp.where(lane == 1, d1, 0.0)).astype(I32)
    r_used = jnp.sum(padded, axis=1, keepdims=True)
    sub8 = lax.broadcasted_iota(I32, (8, LANES), 0)
    meta_ref[0] = jnp.where(sub8 == 0, counts, jnp.where(sub8 == 1, offs, jnp.where(sub8 == 2, r_used, 0.0))).astype(I32)


def moe_plan(ids, ids_t, tt):
    n = ids.shape[0]
    n_tiles = n // tt
    return pl.pallas_call(
        functools.partial(_moe_plan_kernel, tt=tt), grid=(n_tiles,),
        in_specs=[pl.BlockSpec((tt, LANES), lambda i: (i, 0)), pl.BlockSpec((8, tt), lambda i: (0, i))],
        out_specs=[pl.BlockSpec((1, N_EXPERTS, tt), lambda i: (i, 0, 0)), pl.BlockSpec((tt, LANES), lambda i: (i, 0)),
                   pl.BlockSpec((1, 8, LANES), lambda i: (i, 0, 0))],
        out_shape=[jax.ShapeDtypeStruct((n_tiles, N_EXPERTS, tt), F32), jax.ShapeDtypeStruct((n, LANES), I32),
                   jax.ShapeDtypeStruct((n_tiles, 8, LANES), I32)],
        compiler_params=_cparams(("parallel",)), name="moe_plan",
    )(ids, ids_t)


def _pack_bf16_halves(y):
    half = y.shape[1] // 2
    lo = pltpu.bitcast(y[:, :half].astype(BF16).astype(F32), I32)
    hi = pltpu.bitcast(y[:, half:].astype(BF16).astype(F32), I32)
    return lax.shift_right_logical(lo, 16) | hi


def _unpack_bf16_halves(w):
    return pltpu.bitcast(w << 16, F32), pltpu.bitcast(w & jnp.int32(-65536), F32)


def _moe_ffn_kernel(cnt_ref, off_ref, h_ref, rank_ref, idst_ref, gt_ref, w1_ref, w3_ref, w2_ref, y_ref, *, tt):
    t, e = pl.program_id(0), pl.program_id(1)

    @pl.when(e == 0)
    def _():
        y_ref[...] = jnp.zeros_like(y_ref)

    cnt = cnt_ref[t * N_EXPERTS + e]
    off = off_ref[t * N_EXPERTS + e]
    rank_row = rank_ref[0, pl.ds(e, 1), :]
    gate_row = jnp.where(idst_ref[0:1, :] == e, gt_ref[0:1, :], gt_ref[1:2, :])
    sub = lax.broadcasted_iota(I32, (MOE_CH, 1), 0)

    def chunk(c, carry):
        r0 = c * MOE_CH
        pick = rank_row == (r0 + sub).astype(F32)
        x = _dot(jnp.where(pick, 1.0, 0.0).astype(BF16), h_ref[...]).astype(BF16)
        a = _dot(x, w1_ref[0, 0])
        hdn = (a * jax.nn.sigmoid(a)) * _dot(x, w3_ref[0, 0])
        y = _dot(hdn.astype(BF16), w2_ref[0, 0])
        gate = jnp.sum(jnp.where(pick, gate_row, 0.0), axis=1, keepdims=True)
        y_ref[0, pl.ds(pl.multiple_of(off + r0, MOE_CH), MOE_CH), :] = _pack_bf16_halves(y * gate)
        return carry

    lax.fori_loop(0, (cnt + MOE_CH - 1) // MOE_CH, chunk, 0)


def moe_ffn(h, rank_t, ids_t, gates_t, counts, offs, w1, w3, w2, layer, tt):
    n, d = h.shape
    ff = w1.shape[-1]
    n_tiles = n // tt
    r_max = TOP_K_EXPERTS * tt + N_EXPERTS * MOE_CH
    grid_spec = pltpu.PrefetchScalarGridSpec(
        num_scalar_prefetch=2, grid=(n_tiles, N_EXPERTS),
        in_specs=[pl.BlockSpec((tt, d), lambda i, e, c, o: (i, 0)),
                  pl.BlockSpec((1, N_EXPERTS, tt), lambda i, e, c, o: (i, 0, 0)),
                  pl.BlockSpec((8, tt), lambda i, e, c, o: (0, i)),
                  pl.BlockSpec((8, tt), lambda i, e, c, o: (0, i)),
                  pl.BlockSpec((1, 1, d, ff), lambda i, e, c, o: (layer, e, 0, 0)),
                  pl.BlockSpec((1, 1, d, ff), lambda i, e, c, o: (layer, e, 0, 0)),
                  pl.BlockSpec((1, 1, ff, d), lambda i, e, c, o: (layer, e, 0, 0))],
        out_specs=pl.BlockSpec((1, r_max, d // 2), lambda i, e, c, o: (i, 0, 0)))
    return pl.pallas_call(
        functools.partial(_moe_ffn_kernel, tt=tt), grid_spec=grid_spec,
        out_shape=jax.ShapeDtypeStruct((n_tiles, r_max, d // 2), I32),
        compiler_params=_cparams(("arbitrary", "arbitrary")), name="moe_ffn",
    )(counts, offs, h, rank_t, ids_t, gates_t, w1, w3, w2)


def _moe_combine_kernel(dest_ref, x_ref, y_ref, o_ref, ga_ref, gb_ref, *, tm, per):
    base = (pl.program_id(0) * per + pl.program_id(1)) * tm

    def fetch(i, carry):
        tok = (base + i) * TOP_K_EXPERTS
        ga_ref[pl.ds(i, 1), :] = y_ref[0, pl.ds(dest_ref[tok], 1), :]
        gb_ref[pl.ds(i, 1), :] = y_ref[0, pl.ds(dest_ref[tok + 1], 1), :]
        return carry

    lax.fori_loop(0, tm, fetch, 0, unroll=8)
    half = x_ref.shape[1] // 2
    lo_a, hi_a = _unpack_bf16_halves(ga_ref[...])
    lo_b, hi_b = _unpack_bf16_halves(gb_ref[...])
    o_ref[:, :half] = x_ref[:, :half] + (lo_a + lo_b)
    o_ref[:, half:] = x_ref[:, half:] + (hi_a + hi_b)


def moe_combine(x2d, dest_flat, y_all, tt, tm=256):
    n, d = x2d.shape
    r_max = y_all.shape[1]
    per = tt // tm
    grid_spec = pltpu.PrefetchScalarGridSpec(
        num_scalar_prefetch=1, grid=(n // tt, per),
        in_specs=[pl.BlockSpec((tm, d), lambda i, j, u: (i * per + j, 0)),
                  pl.BlockSpec((1, r_max, d // 2), lambda i, j, u: (i, 0, 0))],
        out_specs=pl.BlockSpec((tm, d), lambda i, j, u: (i * per + j, 0)),
        scratch_shapes=[pltpu.VMEM((tm, d // 2), I32), pltpu.VMEM((tm, d // 2), I32)])
    return pl.pallas_call(
        functools.partial(_moe_combine_kernel, tm=tm, per=per), grid_spec=grid_spec,
        out_shape=jax.ShapeDtypeStruct((n, d), F32),
        compiler_params=_cparams(("arbitrary", "arbitrary")), name="moe_combine",
    )(dest_flat, x2d, y_all)


def _final_norm_kernel(x_ref, g_ref, o_ref):
    o_ref[...] = _rms(x_ref[...], g_ref[...])


def final_norm(x2d, g, tm=512):
    n, d = x2d.shape
    return pl.pallas_call(
        _final_norm_kernel, grid=(n // tm,),
        in_specs=[pl.BlockSpec((tm, d), lambda i: (i, 0)), pl.BlockSpec((1, d), lambda i: (0, 0))],
        out_specs=pl.BlockSpec((tm, d), lambda i: (i, 0)), out_shape=jax.ShapeDtypeStruct((n, d), F32),
        compiler_params=_cparams(("parallel",)), name="final_norm",
    )(x2d, g.reshape(1, d))


def moe_layer(x2d, norm_g, router_group, router_expert, w1, w3, w2, layer):
    n, d = x2d.shape
    tt = min(MOE_TT, n)
    w_router = jnp.concatenate([router_group, jnp.moveaxis(router_expert, 0, 1).reshape(d, N_EXPERTS)], axis=1)
    h, ids, gates = moe_router(x2d, norm_g, _pad_cols(w_router, LANES))
    ids_t = jnp.pad(ids[:, :TOP_K_EXPERTS].T, ((0, 8 - TOP_K_EXPERTS), (0, 0)))
    gates_t = jnp.pad(gates[:, :TOP_K_EXPERTS].T, ((0, 8 - TOP_K_EXPERTS), (0, 0)))
    rank_t, dest, meta = moe_plan(ids, ids_t, tt)
    counts = meta[:, 0, :N_EXPERTS].reshape(-1)
    offs = meta[:, 1, :N_EXPERTS].reshape(-1)
    y_all = moe_ffn(h, rank_t, ids_t, gates_t, counts, offs, w1, w3, w2, layer, tt)
    return moe_combine(x2d, dest[:, :TOP_K_EXPERTS].reshape(-1), y_all, tt)


def even_layer(x, v_first, p):
    b, t, d = x.shape
    n = b * t
    w_in = p['w_in']
    aw = p['k_k'].shape[0]
    bw = p['w_kv_up'].shape[1] // 2
    rank = p['w_kv_up'].shape[0]
    n_idx = w_in.shape[1] - (3 * aw + bw + rank + bw + HEAD_DIM)
    o = 0
    segs = []
    for width in (3 * aw, bw, rank, bw, HEAD_DIM, n_idx):
        segs.append(w_in[:, o:o + width])
        o += width
    segs[4] = jnp.concatenate([segs[4], segs[4]], axis=1)
    segs[5] = _pad_cols(segs[5], LANES)
    rkv, q_b, c_kv, q_idx, k_dup, w_idx = norm_proj(
        x.reshape(n, d), p['norm'], [s.astype(BF16) for s in segs], [F32] * 6)

    r, lw, k_mod, v, kk, iclr, gate, bonus = rwkv_prep(x, rkv.reshape(b, t, 3 * aw), v_first, p)
    y = rwkv_scan(r, lw, k_mod, v, kk, iclr)

    cn, kn = kv_prep(c_kv, k_dup, p['kv_norm'], jnp.concatenate([p['idx_k_norm']] * 2))
    wk_t = p['w_kv_up'][:, :bw].astype(BF16)
    wv = p['w_kv_up'][:, bw:].astype(BF16)
    y_b = dsa_attention(q_idx.reshape(b, t, bw), w_idx.reshape(b, t, LANES), q_b.reshape(b, t, bw),
                        kn.reshape(b, t, LANES), cn.reshape(b, t, rank), wk_t, wv)

    x_new = even_out(x.reshape(n, d), y.reshape(n, aw), bonus.reshape(n, aw), gate.reshape(n, aw),
                     y_b.reshape(n, bw), p['ln_g'], p['ln_b'], p['w_out'])
    return x_new.reshape(b, t, d), v


def odd_layer(x, norm_g, w_qkv, w_out):
    b, t, d = x.shape
    n = b * t
    cw = w_qkv.shape[1] // 3
    tq = min(256, t)
    qk, v_t = qkv_proj(x, norm_g, w_qkv, tq)
    y_c = stickbreak_attention(qk.reshape(b, t, 2 * cw), v_t, cw // HEAD_DIM, tq)
    return proj_residual(x.reshape(n, d), y_c.reshape(n, cw), w_out.astype(BF16)).reshape(b, t, d)


@jax.jit
def _forward(x, ev_norm, ev_w_in, ev_w_out, a_mu_rkv, a_mu_lora, a_w0, a_w1, a_w2, a_a0, a_a1, a_a2, a_g1,
             a_g2, a_k_k, a_k_a, a_r_k, a_ln_g, a_ln_b, a_mu_vres, a_v0, a_v1, a_v2, b_kv_norm, b_w_kv_up,
             b_idx_k_norm, od_norm, od_w_qkv, od_w_out, moe_norm, moe_router_group, moe_router_expert,
             moe_w1, moe_w3, moe_w2, final_norm_g):
    b, t, d = x.shape
    depth = moe_norm.shape[0]
    w1_bf16, w3_bf16, w2_bf16 = moe_w1.astype(BF16), moe_w3.astype(BF16), moe_w2.astype(BF16)
    v_first = None
    for i in range(depth):
        if i % 2 == 0:
            e = i // 2
            p = dict(norm=ev_norm[e], w_in=ev_w_in[e], w_out=ev_w_out[e], mu_rkv=a_mu_rkv[e], mu_lora=a_mu_lora[e],
                     w0=a_w0[e], w1=a_w1[e], w2=a_w2[e], a0=a_a0[e], a1=a_a1[e], a2=a_a2[e], g1=a_g1[e], g2=a_g2[e],
                     k_k=a_k_k[e], k_a=a_k_a[e], r_k=a_r_k[e].reshape(-1), ln_g=a_ln_g[e], ln_b=a_ln_b[e],
                     kv_norm=b_kv_norm[e], w_kv_up=b_w_kv_up[e], idx_k_norm=b_idx_k_norm[e])
            if e > 0:
                p.update(mu_v=a_mu_vres[e - 1], v0=a_v0[e - 1], v1=a_v1[e - 1], v2=a_v2[e - 1])
            x, v_used = even_layer(x, v_first if e > 0 else None, p)
            if e == 0:
                v_first = v_used
        else:
            o = i // 2
            x = odd_layer(x, od_norm[o], od_w_qkv[o], od_w_out[o])
        x = moe_layer(x.reshape(b * t, d), moe_norm[i], moe_router_group[i], moe_router_expert[i],
                      w1_bf16, w3_bf16, w2_bf16, i).reshape(b, t, d)
    return final_norm(x.reshape(b * t, d), final_norm_g).reshape(b, t, d)


def kernel(x, ev_norm, ev_w_in, ev_w_out, a_mu_rkv, a_mu_lora, a_w0, a_w1, a_w2, a_a0, a_a1, a_a2, a_g1, a_g2, a_k_k, a_k_a, a_r_k, a_ln_g, a_ln_b, a_mu_vres, a_v0, a_v1, a_v2, b_kv_norm, b_w_kv_up, b_idx_k_norm, od_norm, od_w_qkv, od_w_out, moe_norm, moe_router_group, moe_router_expert, moe_w1, moe_w3, moe_w2, final_norm):
    return _forward(x, ev_norm, ev_w_in, ev_w_out, a_mu_rkv, a_mu_lora, a_w0, a_w1, a_w2, a_a0, a_a1, a_a2, a_g1,
                    a_g2, a_k_k, a_k_a, a_r_k, a_ln_g, a_ln_b, a_mu_vres, a_v0, a_v1, a_v2, b_kv_norm, b_w_kv_up,
                    b_idx_k_norm, od_norm, od_w_qkv, od_w_out, moe_norm, moe_router_group, moe_router_expert,
                    moe_w1, moe_w3, moe_w2, final_norm)
```

```python
import functools
import math

import jax
import jax.numpy as jnp
from jax import lax
from jax.experimental import pallas as pl
from jax.experimental.pallas import tpu as pltpu

F32 = jnp.float32
BF16 = jnp.bfloat16
I32 = jnp.int32
HI = lax.Precision.HIGHEST

LANES = 128
HEAD_DIM = 64
PAIR = 2 * HEAD_DIM
RMS_EPS = 1e-6
GN_EPS = 64e-5
N_GROUPS = 4
EXPERTS_PER_GROUP = 8
N_EXPERTS = N_GROUPS * EXPERTS_PER_GROUP
TOP_K_EXPERTS = 2
IDX_TOPK_MAX = 256
INT_MIN = -2147483648
NEG_BIG = -1e30
VMEM_LIMIT = 56 * 1024 * 1024

_NT = (((1,), (1,)), ((), ()))
_TN = (((0,), (0,)), ((), ()))


def _cparams(sem):
    return pltpu.CompilerParams(dimension_semantics=sem, vmem_limit_bytes=VMEM_LIMIT)


def _rms(x, g):
    return x * lax.rsqrt(jnp.mean(x * x, axis=-1, keepdims=True) + RMS_EPS) * g


def _dot(a, b):
    return jnp.dot(a, b, preferred_element_type=F32)


def _dot_hi(a, b):
    return jnp.dot(a, b, preferred_element_type=F32, precision=HI)


def _seg_dot(x, seg):
    hi = x.astype(BF16)
    return _dot(hi, seg) + _dot((x - hi.astype(F32)).astype(BF16), seg)


def _dot_nt(a, b):
    return lax.dot_general(a, b, _NT, preferred_element_type=F32)


def _lane_is_first_head(width=PAIR):
    return lax.broadcasted_iota(I32, (1, width), 1) % PAIR < HEAD_DIM


def _norm_proj_kernel(x_ref, g_ref, *refs, n_seg):
    w_refs, o_refs = refs[:n_seg], refs[n_seg:]
    h = _rms(x_ref[...], g_ref[...]).astype(BF16)
    for w_ref, o_ref in zip(w_refs, o_refs):
        o_ref[...] = _dot(h, w_ref[...]).astype(o_ref.dtype)


def norm_proj(x2d, g, ws, out_dtypes, tm=256):
    n, d = x2d.shape
    in_specs = [pl.BlockSpec((tm, d), lambda i: (i, 0)), pl.BlockSpec((1, d), lambda i: (0, 0))]
    in_specs += [pl.BlockSpec(w.shape, lambda i: (0, 0)) for w in ws]
    out_specs = [pl.BlockSpec((tm, w.shape[1]), lambda i: (i, 0)) for w in ws]
    out_shape = [jax.ShapeDtypeStruct((n, w.shape[1]), dt) for w, dt in zip(ws, out_dtypes)]
    return pl.pallas_call(
        functools.partial(_norm_proj_kernel, n_seg=len(ws)),
        grid=(n // tm,), in_specs=in_specs, out_specs=out_specs, out_shape=out_shape,
        compiler_params=_cparams(("parallel",)), name="norm_proj",
    )(x2d, g.reshape(1, d), *ws)


def _shift_rows(z, carry_ref):
    tm = z.shape[0]
    first = lax.broadcasted_iota(I32, (tm, 1), 0) == 0
    prev = jnp.where(first, carry_ref[...], pltpu.roll(z, 1, 0))
    carry_ref[...] = z[tm - 1:tm, :]
    return prev


def _rwkv_prep_kernel(*refs, has_vres):
    it = iter(refs)
    x_ref, rkv_ref = next(it), next(it)
    vfirst_ref = next(it) if has_vres else None
    g_ref, mu_rkv_ref, mu_lora_ref = next(it), next(it), next(it)
    w0_ref, w1_ref, w2_ref = next(it), next(it), next(it)
    a0_ref, a1_ref, a2_ref = next(it), next(it), next(it)
    g1_ref, g2_ref = next(it), next(it)
    if has_vres:
        muv_ref, v0_ref, v1_ref, v2_ref = next(it), next(it), next(it), next(it)
    kk_w_ref, ka_ref, rk_ref, seg_ref = next(it), next(it), next(it), next(it)
    r_o, lw_o, k_o, v_o, kk_o, a_o, gate_o, bonus_o = (next(it) for _ in range(8))
    hcarry, rkvcarry = next(it), next(it)

    @pl.when(pl.program_id(1) == 0)
    def _():
        hcarry[...] = jnp.zeros_like(hcarry)
        rkvcarry[...] = jnp.zeros_like(rkvcarry)

    aw = r_o.shape[-1]
    h = _rms(x_ref[0], g_ref[...])
    dh = _shift_rows(h, hcarry) - h
    rkv = rkv_ref[0]
    rkv = rkv + (_shift_rows(rkv, rkvcarry) - rkv) * mu_rkv_ref[...]
    r, k, v = rkv[:, :aw], rkv[:, aw:2 * aw], rkv[:, 2 * aw:]

    def lora_in(row):
        return (h + dh * mu_lora_ref[row:row + 1, :]).astype(BF16)

    dec = w0_ref[...] + _dot(jnp.tanh(_dot(lora_in(0), w1_ref[...])).astype(BF16), w2_ref[...])
    logw = -math.exp(-0.5) * jax.nn.sigmoid(dec)
    iclr = jax.nn.sigmoid(a0_ref[...] + _dot(_dot(lora_in(1), a1_ref[...]).astype(BF16), a2_ref[...]))
    gate = _dot(jax.nn.sigmoid(_dot(lora_in(2), g1_ref[...])).astype(BF16), g2_ref[...])
    if has_vres:
        xv = (h + dh * muv_ref[...]).astype(BF16)
        mix = jax.nn.sigmoid(v0_ref[...] + _dot(_dot(xv, v1_ref[...]).astype(BF16), v2_ref[...]))
        v = v + (vfirst_ref[0] - v) * mix

    seg = seg_ref[...]
    kk = k * kk_w_ref[...]
    kk = kk * lax.rsqrt(jnp.maximum(_seg_dot(kk * kk, seg), 1e-12))
    k_mod = k * (1.0 + (iclr - 1.0) * ka_ref[...])
    bonus = _seg_dot(r * k_mod * rk_ref[...], seg) * v

    r_o[0], lw_o[0], k_o[0], v_o[0] = r, logw, k_mod, v
    kk_o[0], a_o[0], gate_o[0], bonus_o[0] = kk, iclr, gate, bonus


def _pad_cols(w, n):
    return jnp.pad(w, ((0, 0), (0, n - w.shape[1])))


def _pad_rows(w, n):
    return jnp.pad(w, ((0, n - w.shape[0]), (0, 0)))


def _seg_ones(width):
    i = jnp.arange(width) // HEAD_DIM
    return (i[:, None] == i[None, :]).astype(BF16)


def rwkv_prep(x, rkv, v_first, p, tm=256):
    b, t, d = x.shape
    aw = rkv.shape[-1] // 3
    has_vres = v_first is not None
    row = lambda a: a.reshape(1, -1)
    const = lambda a: pl.BlockSpec(a.shape, lambda i, j: (0,) * a.ndim)
    tile = lambda c: pl.BlockSpec((1, tm, c), lambda i, j: (i, j, 0))

    args, specs = [x, rkv], [tile(d), tile(3 * aw)]
    if has_vres:
        args.append(v_first)
        specs.append(tile(aw))
    consts = [row(p['norm']), p['mu_rkv'].reshape(1, 3 * aw), p['mu_lora'],
              row(p['w0']), _pad_cols(p['w1'], LANES).astype(BF16), _pad_rows(p['w2'], LANES).astype(BF16),
              row(p['a0']), _pad_cols(p['a1'], LANES).astype(BF16), _pad_rows(p['a2'], LANES).astype(BF16),
              p['g1'].astype(BF16), p['g2'].astype(BF16)]
    if has_vres:
        consts += [row(p['mu_v']), row(p['v0']), _pad_cols(p['v1'], LANES).astype(BF16),
                   _pad_rows(p['v2'], LANES).astype(BF16)]
    consts += [row(p['k_k']), row(p['k_a']), row(p['r_k']), _seg_ones(aw)]
    args += consts
    specs += [const(a) for a in consts]
    out_shape = [jax.ShapeDtypeStruct((b, t, aw), F32)] * 8
    return pl.pallas_call(
        functools.partial(_rwkv_prep_kernel, has_vres=has_vres),
        grid=(b, t // tm), in_specs=specs, out_specs=[tile(aw)] * 8, out_shape=out_shape,
        scratch_shapes=[pltpu.VMEM((1, d), F32), pltpu.VMEM((1, 3 * aw), F32)],
        compiler_params=_cparams(("arbitrary", "arbitrary")), name="rwkv_prep",
    )(*args)


def _split_bf16(x):
    hi = x.astype(BF16)
    return hi, (x - hi.astype(F32)).astype(BF16)


def _mm3(a, b, dims=None):
    a_hi, a_lo = _split_bf16(a)
    b_hi, b_lo = _split_bf16(b)
    if dims is None:
        f = lambda p, q: jnp.dot(p, q, preferred_element_type=F32)
    else:
        f = lambda p, q: lax.dot_general(p, q, dims, preferred_element_type=F32)
    return f(a_hi, b_hi) + f(a_hi, b_lo) + f(a_lo, b_hi)


def _mmc(a, b, dims=None):
    a, b = a.astype(BF16), b.astype(BF16)
    if dims is None:
        return jnp.dot(a, b, preferred_element_type=F32)
    return lax.dot_general(a, b, dims, preferred_element_type=F32)


def _cumsum_rows(x):
    n = x.shape[0]
    row = lax.broadcasted_iota(I32, (n, 1), 0)
    s = 1
    while s < n:
        x = x + jnp.where(row >= s, pltpu.roll(x, s, 0), 0.0)
        s *= 2
    return x


def _rwkv_chunk_kernel(r_ref, lw_ref, k_ref, v_ref, kk_ref, a_ref, qh_ref, y0_ref, m_ref, c0_ref, *, chunk, cps):
    L = chunk
    n_pair = r_ref.shape[-1] // PAIR
    row = lax.broadcasted_iota(I32, (L, L), 0)
    col = lax.broadcasted_iota(I32, (L, L), 1)
    strict = col < row
    incl = col <= row
    eye = (col == row).astype(F32)
    first = _lane_is_first_head()
    p_row = lax.broadcasted_iota(I32, (PAIR, PAIR), 0)
    p_col = lax.broadcasted_iota(I32, (PAIR, PAIR), 1)
    blockdiag = p_row // HEAD_DIM == p_col // HEAD_DIM
    pair_eye = p_row == p_col

    units = [(ci, p) for ci in range(cps) for p in range(n_pair)]
    heads = (first, ~first)
    pre = {}
    for u in units:
        ci, p = u
        rows = slice(ci * L, (ci + 1) * L)
        sl = slice(p * PAIR, (p + 1) * PAIR)
        lw = lw_ref[0, rows, sl]
        c = _cumsum_rows(lw)
        g = jnp.exp(c)
        g_inv = jnp.exp(-c)
        kk = kk_ref[0, rows, sl]
        pre[u] = dict(rows=rows, sl=sl, g_last=g[L - 1:L, :],
                      at=-kk * jnp.exp(c - lw),
                      bt=kk * a_ref[0, rows, sl] * g_inv,
                      kt=k_ref[0, rows, sl] * g_inv,
                      rt=r_ref[0, rows, sl] * g,
                      vv=v_ref[0, rows, sl])
    uh = [(u, h) for u in units for h in range(2)]

    gm = {}
    for u, h in uh:
        d = pre[u]
        lhs = jnp.concatenate([jnp.where(heads[h], d['at'], 0.0), jnp.where(heads[h], d['rt'], 0.0)], axis=0)
        gm[u, h] = _mm3(lhs, jnp.concatenate([d['bt'], d['kt']], axis=0), _NT)
    a_ak = {x: jnp.where(strict, gm[x][:L, L:], 0.0) for x in uh}
    a_rb = {x: jnp.where(incl, gm[x][L:, :L], 0.0) for x in uh}
    a_rk = {x: jnp.where(incl, gm[x][L:, L:], 0.0) for x in uh}
    apow = {x: jnp.where(strict, gm[x][:L, :L], 0.0) for x in uh}
    inv = {x: eye + apow[x] for x in uh}
    for _ in range(max(1, (L - 1).bit_length()) - 1):
        apow = {x: _mmc(apow[x], apow[x]) for x in uh}
        inv = {x: inv[x] + _mmc(inv[x], apow[x]) for x in uh}
    w_h = {(u, h): _mmc(inv[u, h], pre[u]['at']) for u, h in uh}
    akv = {(u, h): _mmc(a_ak[u, h], pre[u]['vv']) for u, h in uh}
    u0_h = {x: _mmc(inv[x], akv[x]) for x in uh}
    w = {u: jnp.where(first, w_h[u, 0], w_h[u, 1]) for u in units}
    u0 = {u: jnp.where(first, u0_h[u, 0], u0_h[u, 1]) for u in units}
    wu = {u: jnp.concatenate([w[u], u0[u]], axis=1) for u in units}
    arb = {(u, h): _mmc(a_rb[u, h], wu[u]) for u, h in uh}
    ark = {(u, h): _mmc(a_rk[u, h], pre[u]['vv']) for u, h in uh}
    first2 = jnp.concatenate([first, first], axis=1)
    for u in units:
        d = pre[u]
        ci, p = u
        arb_w = jnp.where(first2, arb[u, 0], arb[u, 1])
        qh_ref[0, d['rows'], d['sl']] = d['rt'] + arb_w[:, :PAIR]
        y0_ref[0, d['rows'], d['sl']] = arb_w[:, PAIR:] + jnp.where(first, ark[u, 0], ark[u, 1])
        bg = d['bt'] * d['g_last']
        m_kk = _mmc(bg, w[u], _TN)
        m_ref[0, ci, p] = jnp.where(pair_eye, d['g_last'], 0.0) + jnp.where(blockdiag, m_kk, 0.0)
        c0 = _mmc(jnp.concatenate([u0[u], d['vv']], axis=0),
                  jnp.concatenate([bg, d['kt'] * d['g_last']], axis=0), _TN)
        c0_ref[0, ci, p] = jnp.where(blockdiag, c0, 0.0)


def _rwkv_state_kernel(qh_ref, y0_ref, m_ref, c0_ref, y_ref, s_ref, *, chunk, cps):
    L = chunk

    @pl.when(pl.program_id(0) == 0)
    def _():
        s_ref[...] = jnp.zeros_like(s_ref)

    n_batch = qh_ref.shape[0]
    n_pair = qh_ref.shape[-1] // PAIR
    streams = [(b, p) for b in range(n_batch) for p in range(n_pair)]
    s2 = {x: s_ref[x[0], x[1]] for x in streams}
    for ci in range(cps):
        rows = slice(ci * L, (ci + 1) * L)
        for b, p in streams:
            sl = slice(p * PAIR, (p + 1) * PAIR)
            y_ref[b, rows, sl] = _mm3(qh_ref[b, rows, sl], s2[b, p], _NT) + y0_ref[b, rows, sl]
            s2[b, p] = _mm3(s2[b, p], m_ref[b, ci, p], _NT) + c0_ref[b, ci, p]
    for b, p in streams:
        s_ref[b, p] = s2[b, p]


def rwkv_scan(r, lw, k, v, kk, a, chunk=64, cps_a=4, cps_b=4):
    b, t, aw = r.shape
    n_pair = aw // PAIR
    n_chunk = t // chunk
    cps_a, cps_b = min(cps_a, n_chunk), min(cps_b, n_chunk)
    rows = lambda c: pl.BlockSpec((1, c * chunk, aw), lambda i, j: (i, j, 0))
    mats = lambda c: pl.BlockSpec((1, c, n_pair, PAIR, PAIR), lambda i, j: (i, j, 0, 0, 0))
    seq = jax.ShapeDtypeStruct((b, t, aw), F32)
    mat = jax.ShapeDtypeStruct((b, n_chunk, n_pair, PAIR, PAIR), F32)
    qh, y0, m, c0 = pl.pallas_call(
        functools.partial(_rwkv_chunk_kernel, chunk=chunk, cps=cps_a),
        grid=(b, n_chunk // cps_a), in_specs=[rows(cps_a)] * 6,
        out_specs=[rows(cps_a), rows(cps_a), mats(cps_a), mats(cps_a)], out_shape=[seq, seq, mat, mat],
        compiler_params=_cparams(("parallel", "parallel")), name="rwkv_chunk",
    )(r, lw, k, v, kk, a)
    rows_b = pl.BlockSpec((b, cps_b * chunk, aw), lambda j: (0, j, 0))
    mats_b = pl.BlockSpec((b, cps_b, n_pair, PAIR, PAIR), lambda j: (0, j, 0, 0, 0))
    return pl.pallas_call(
        functools.partial(_rwkv_state_kernel, chunk=chunk, cps=cps_b),
        grid=(n_chunk // cps_b,), in_specs=[rows_b, rows_b, mats_b, mats_b],
        out_specs=rows_b, out_shape=seq,
        scratch_shapes=[pltpu.VMEM((b, n_pair, PAIR, PAIR), F32)],
        compiler_params=_cparams(("arbitrary",)), name="rwkv_state",
    )(qh, y0, m, c0)


def _kv_prep_kernel(c_ref, k_ref, gc_ref, gk_ref, cn_ref, kn_ref):
    cn_ref[...] = _rms(c_ref[...], gc_ref[...]).astype(cn_ref.dtype)
    kn_ref[...] = _rms(k_ref[...], gk_ref[...]).astype(kn_ref.dtype)


def kv_prep(c_kv, k_dup, g_c, g_k_dup, tm=512):
    n, w = c_kv.shape
    tile = pl.BlockSpec((tm, w), lambda i: (i, 0))
    const = pl.BlockSpec((1, w), lambda i: (0, 0))
    return pl.pallas_call(
        _kv_prep_kernel, grid=(n // tm,), in_specs=[tile, tile, const, const], out_specs=[tile, tile],
        out_shape=[jax.ShapeDtypeStruct((n, w), BF16)] * 2,
        compiler_params=_cparams(("parallel",)), name="kv_prep",
    )(c_kv, k_dup, g_c.reshape(1, w), g_k_dup.reshape(1, w))


SEARCH_VALUE_STEPS = 24
SEARCH_STEPS_PER_CHECK = 2


def _tree_reduce(combine, xs):
    while len(xs) > 1:
        xs = [combine(xs[i], xs[i + 1]) for i in range(0, len(xs) - 1, 2)] + ([xs[-1]] if len(xs) % 2 else [])
    return xs[0]


def _key_to_f32(key):
    return pltpu.bitcast(jnp.where(key >= 0, key, key ^ 0x7FFFFFFF), F32)


def _f32_to_key(x):
    bits = pltpu.bitcast(x, I32)
    return jnp.where(bits >= 0, bits, bits ^ 0x7FFFFFFF)


def _dsa_kernel(qi_ref, wit_ref, q_ref, kn_ref, cn_ref, cnt_ref, wk_ref, wv_ref, o_ref,
                skey_ref, qpt_ref, acc_ref, m_ref, l_ref, *, qb, kt, ka, topk, n_heads, idx_bits):
    blk = pl.program_id(1)
    nt = ((blk + 1) * qb + kt - 1) // kt
    first = _lane_is_first_head()
    first_sub = lax.broadcasted_iota(I32, (PAIR, 1), 0) < HEAD_DIM
    q_pos = blk * qb + lax.broadcasted_iota(I32, (1, qb), 1)
    row0 = lax.broadcasted_iota(I32, (kt, 1), 0)
    heads = range(n_heads)

    qi = qi_ref[0]
    wit = wit_ref[0]
    qi_t = []
    for p in range(n_heads // 2):
        pair_t = qi[:, p * PAIR:(p + 1) * PAIR].T * HEAD_DIM ** -0.5
        qi_t += [jnp.where(first_sub, pair_t, 0.0).astype(BF16), jnp.where(first_sub, 0.0, pair_t).astype(BF16)]

    def score_tile(j, carry):
        off = pl.multiple_of(j * kt, kt)
        kn = kn_ref[0, pl.ds(off, kt), :]
        sc = jnp.zeros((kt, qb), F32)
        for h in heads:
            sc = sc + wit[h:h + 1, :] * jnp.maximum(_dot(kn, qi_t[h]), 0.0)
        sc = sc * n_heads ** -0.5
        key = jnp.where(sc == 0.0, 0, _f32_to_key(sc))
        causal = off + row0 <= q_pos
        skey_ref[j] = jnp.where(causal, key, INT_MIN)
        hi8 = _tree_reduce(jnp.maximum, list(jnp.where(causal, sc, -jnp.inf).reshape(kt // 8, 8, qb)))
        lo8 = _tree_reduce(jnp.minimum, list(jnp.where(causal, sc, jnp.inf).reshape(kt // 8, 8, qb)))
        return jnp.maximum(carry[0], hi8), jnp.minimum(carry[1], lo8)

    hi8, lo8 = lax.fori_loop(0, nt, score_tile,
                             (jnp.full((8, qb), -jnp.inf, F32), jnp.full((8, qb), jnp.inf, F32)))

    def fold(tile_fn, init, combine):
        def body(j, acc):
            part = tile_fn(skey_ref[j], j * kt + row0).reshape(kt // 8, 8, qb)
            return combine(acc, _tree_reduce(combine, [part[i] for i in range(kt // 8)]))
        return lax.fori_loop(0, nt, body, jnp.full((8, qb), init, F32))

    def count(pred):
        acc = fold(lambda sk, idx: jnp.where(pred(sk, idx), 1.0, 0.0), 0.0, jnp.add)
        return jnp.sum(acc, axis=0, keepdims=True)

    few = q_pos + 1 <= topk
    v_max = jnp.max(hi8, axis=0, keepdims=True)
    v_min = jnp.min(lo8, axis=0, keepdims=True)
    lo0 = _f32_to_key(jnp.where(v_min == 0.0, 0.0, v_min))
    hi0 = _f32_to_key(jnp.where(v_max == 0.0, 0.0, v_max)) + 1
    cnt0 = (q_pos + 1).astype(F32)

    def unresolved(lo, hi, cnt_lo):
        return ~(few | (cnt_lo == topk) | (hi <= lo + 1))

    def any_open(lo, hi, cnt_lo):
        return jnp.max(jnp.where(unresolved(lo, hi, cnt_lo), 1.0, 0.0))

    def search_cond(st):
        return (st[0] < SEARCH_VALUE_STEPS + 34) & (st[4] > 0.0)

    def search_step(st):
        it, lo, hi, cnt_lo, _ = st
        for _ in range(SEARCH_STEPS_PER_CHECK):
            mid_v = _f32_to_key(0.5 * _key_to_f32(lo) + 0.5 * _key_to_f32(hi))
            mid_k = (lo >> 1) + (hi >> 1) + (lo & hi & 1)
            mid = jnp.where((it < SEARCH_VALUE_STEPS) & (mid_v > lo) & (mid_v < hi), mid_v, mid_k)
            mid = jnp.where((it == 0) & (lo < 0) & (hi > 0), 0, mid)
            mid = jnp.where((it == 1) & (lo == 0) & (hi > 1), 1, mid)
            cnt = count(lambda sk, _: sk >= mid)
            open_ = unresolved(lo, hi, cnt_lo)
            up = open_ & (cnt >= topk)
            lo, cnt_lo = jnp.where(up, mid, lo), jnp.where(up, cnt, cnt_lo)
            hi = jnp.where(open_ & ~up, mid, hi)
            it = it + 1
        return it, lo, hi, cnt_lo, any_open(lo, hi, cnt_lo)

    st0 = (jnp.int32(0), lo0, hi0, cnt0, any_open(lo0, hi0, cnt0))
    _, lo, _, cnt_lo, _ = lax.while_loop(search_cond, search_step, st0)
    thr = jnp.where(few, INT_MIN, lo)
    tie = (cnt_lo > topk) & ~few
    any_tie = jnp.max(jnp.where(tie, 1, 0))
    n_gt = lax.fori_loop(0, any_tie, lambda i, c: count(lambda sk, _: sk > thr), jnp.zeros((1, qb), F32))
    need = topk - n_gt

    def idx_bit(i, prefix):
        cand = prefix | jnp.left_shift(jnp.int32(1), idx_bits - 1 - i)
        cnt = count(lambda sk, kidx: (sk == thr) & (kidx < cand))
        return jnp.where(cnt < need, cand, prefix)

    jmax = lax.fori_loop(0, idx_bits * any_tie, idx_bit, jnp.zeros((1, qb), I32))
    jmax = jnp.where(tie, jmax, 0x7FFFFFFF)

    q = q_ref[0]
    for h in heads:
        pr = slice((h // 2) * PAIR, (h // 2 + 1) * PAIR)
        wk_h = jnp.where(first if h % 2 == 0 else ~first, wk_ref[:, pr], 0.0)
        qpt_ref[h] = (_dot_nt(wk_h, q[:, pr].astype(BF16)) * (HEAD_DIM ** -0.5 * math.log2(math.e))).astype(BF16)
    m_ref[...] = jnp.full_like(m_ref, NEG_BIG)
    l_ref[...] = jnp.zeros_like(l_ref)
    acc_ref[...] = jnp.zeros_like(acc_ref)

    def attn_tile(j, carry):
        sk_all = skey_ref[j]
        for part in range(kt // ka):
            off = pl.multiple_of(j * kt + part * ka, ka)
            sk = sk_all[part * ka:(part + 1) * ka, :]
            kidx = off + row0[:ka]
            sel = ((sk > thr) | ((sk == thr) & (kidx <= jmax))) & (kidx <= q_pos)
            bias = jnp.where(sel, 0.0, NEG_BIG)
            cn = cn_ref[0, pl.ds(off, ka), :]
            cn_t = cnt_ref[0, j * (kt // ka) + part]
            s = [_dot(cn, qpt_ref[h]) + bias for h in heads]
            m_old = [m_ref[h] for h in heads]
            m_new = [jnp.maximum(m_old[h], jnp.max(s[h], axis=0, keepdims=True)) for h in heads]
            pexp = [jnp.exp2(s[h] - m_new[h]) for h in heads]
            alpha = [jnp.exp2(m_old[h] - m_new[h]) for h in heads]
            pv = [_dot(cn_t, pexp[h].astype(BF16)) for h in heads]
            for h in heads:
                l_ref[h] = alpha[h] * l_ref[h] + jnp.sum(pexp[h], axis=0, keepdims=True)
                acc_ref[h] = alpha[h] * acc_ref[h] + pv[h]
                m_ref[h] = m_new[h]
        return carry

    lax.fori_loop(0, nt, attn_tile, 0)

    rank = cn_ref.shape[-1]
    for p in range(n_heads // 2):
        pr = slice(p * PAIR, (p + 1) * PAIR)
        out = jnp.zeros((qb, PAIR), F32)
        for h in (2 * p, 2 * p + 1):
            lat_t = (acc_ref[h] / l_ref[h]).astype(BF16)
            wv_h = jnp.where(first if h % 2 == 0 else ~first, wv_ref[:, pr], 0.0)
            out = out + lax.dot_general(lat_t, wv_h, _TN, preferred_element_type=F32)
        o_ref[0, :, pr] = out.astype(o_ref.dtype)


def dsa_attention(q_idx, w_idx, q_b, kn, cn, wk_t, wv, qb=128, kt=512, ka=256):
    b, t, width = q_b.shape
    n_heads = width // HEAD_DIM
    rank = cn.shape[-1]
    kt, ka = min(kt, t), min(ka, t)
    topk = min(IDX_TOPK_MAX, t // 4)
    w_t = jnp.swapaxes(w_idx[:, :, :n_heads], 1, 2)
    cn_t = jnp.swapaxes(cn.reshape(b, t // ka, ka, rank), 2, 3)
    qtile = lambda c: pl.BlockSpec((1, qb, c), lambda i, j: (i, j, 0))
    full = lambda c: pl.BlockSpec((1, t, c), lambda i, j: (i, 0, 0))
    const = lambda a: pl.BlockSpec(a.shape, lambda i, j: (0, 0))
    kern = functools.partial(_dsa_kernel, qb=qb, kt=kt, ka=ka, topk=topk, n_heads=n_heads,
                             idx_bits=max(1, (t - 1).bit_length()))
    return pl.pallas_call(
        kern, grid=(b, t // qb),
        in_specs=[qtile(width), pl.BlockSpec((1, n_heads, qb), lambda i, j: (i, 0, j)), qtile(width),
                  full(LANES), full(rank), pl.BlockSpec((1, t // ka, rank, ka), lambda i, j: (i, 0, 0, 0)),
                  const(wk_t), const(wv)],
        out_specs=qtile(width), out_shape=jax.ShapeDtypeStruct((b, t, width), BF16),
        scratch_shapes=[pltpu.VMEM((t // kt, kt, qb), I32), pltpu.VMEM((n_heads, rank, qb), BF16),
                        pltpu.VMEM((n_heads, rank, qb), F32), pltpu.VMEM((n_heads, 1, qb), F32),
                        pltpu.VMEM((n_heads, 1, qb), F32)],
        compiler_params=_cparams(("parallel", "arbitrary")), name="dsa_attention",
    )(q_idx, w_t, q_b, kn, cn, cn_t, wk_t, wv)


def _even_out_kernel(x_ref, y_ref, bonus_ref, gate_ref, yb_ref, lng_ref, lnb_ref, seg_ref, wa_ref, wb_ref, o_ref):
    seg = seg_ref[...]
    y = y_ref[...]
    yc = y - _seg_dot(y, seg) * (1.0 / HEAD_DIM)
    yn = yc * lax.rsqrt(_seg_dot(yc * yc, seg) * (1.0 / HEAD_DIM) + GN_EPS)
    ya = (yn * lng_ref[...] + lnb_ref[...] + bonus_ref[...]) * gate_ref[...]
    o_ref[...] = x_ref[...] + _dot(ya.astype(BF16), wa_ref[...]) + _dot(yb_ref[...], wb_ref[...])


def even_out(x2d, y, bonus, gate, y_b, ln_g, ln_b, w_out, tm=256):
    n, d = x2d.shape
    aw = y.shape[1]
    tile = lambda c: pl.BlockSpec((tm, c), lambda i: (i, 0))
    const = lambda a: pl.BlockSpec(a.shape, lambda i: (0, 0))
    consts = [ln_g.reshape(1, aw), ln_b.reshape(1, aw), _seg_ones(aw),
              w_out[:aw].astype(BF16), w_out[aw:].astype(BF16)]
    return pl.pallas_call(
        _even_out_kernel, grid=(n // tm,),
        in_specs=[tile(d), tile(aw), tile(aw), tile(aw), tile(y_b.shape[1])] + [const(a) for a in consts],
        out_specs=tile(d), out_shape=jax.ShapeDtypeStruct((n, d), F32),
        compiler_params=_cparams(("parallel",)), name="even_out",
    )(x2d, y, bonus, gate, y_b, *consts)


SB_PAIRS = 4


def _stickbreak_kernel(q_ref, k_ref, vt_ref, o_ref, acc_ref, carry_ref, *, tq, n_pairs):
    qi = pl.program_id(2)
    first_sub = lax.broadcasted_iota(I32, (PAIR, 1), 0) < HEAD_DIM
    key_i = lax.broadcasted_iota(I32, (tq, tq), 0)
    qry_i = lax.broadcasted_iota(I32, (tq, tq), 1)
    before = key_i < qry_i
    later = (qry_i > key_i).astype(BF16)
    heads = [(p, x) for p in range(n_pairs) for x in range(2)]
    q_h = {}
    for p in range(n_pairs):
        q_t = q_ref[0, :, p * PAIR:(p + 1) * PAIR].astype(F32).T * HEAD_DIM ** -0.5
        q_h[p, 0] = jnp.where(first_sub, q_t, 0.0).astype(BF16)
        q_h[p, 1] = jnp.where(first_sub, 0.0, q_t).astype(BF16)
    acc_ref[...] = jnp.zeros_like(acc_ref)
    carry_ref[...] = jnp.zeros_like(carry_ref)

    def tile(j, diag):
        off = pl.multiple_of(j * tq, tq)
        k2 = [k_ref[0, pl.ds(off, tq), p * PAIR:(p + 1) * PAIR] for p in range(n_pairs)]
        z = {h: _dot(k2[h[0]], q_h[h]) * math.log2(math.e) for h in heads}
        lk = {h: -(jnp.maximum(z[h], 0.0) + jnp.log2(1.0 + jnp.exp2(-jnp.abs(z[h])))) for h in heads}
        if diag:
            lk = {h: jnp.where(before, lk[h], 0.0) for h in heads}
        hi = {h: lk[h].astype(BF16) for h in heads}
        lo = {h: (lk[h] - hi[h].astype(F32)).astype(BF16) for h in heads}
        tail = {h: _dot(later, hi[h]) + _dot(later, lo[h]) + carry_ref[h[0], h[1]] for h in heads}
        w = {h: jnp.exp2(z[h] + lk[h] + tail[h]) for h in heads}
        if diag:
            w = {h: jnp.where(before, w[h], 0.0) for h in heads}
        for p in range(n_pairs):
            v_t = vt_ref[0, p, j]
            acc_ref[p] += (_dot(jnp.where(first_sub, v_t, 0.0), w[p, 0].astype(BF16))
                           + _dot(jnp.where(first_sub, 0.0, v_t), w[p, 1].astype(BF16)))
        for h in heads:
            carry_ref[h[0], h[1]] += jnp.sum(lk[h], axis=0, keepdims=True)
        return (jnp.max(carry_ref[...]) > -110.0 * math.log2(math.e)).astype(I32)

    live = tile(qi, True)

    def cond(st):
        return (st[0] >= 0) & (st[1] > 0)

    def body(st):
        return st[0] - 1, tile(st[0], False)

    lax.while_loop(cond, body, (qi - 1, live))
    for p in range(n_pairs):
        o_ref[0, :, p * PAIR:(p + 1) * PAIR] = acc_ref[p].T.astype(o_ref.dtype)


def _qkv_proj_kernel(x_ref, g_ref, wqk_ref, wv_ref, qk_ref, vt_ref):
    h = _rms(x_ref[...], g_ref[...]).astype(BF16)
    qk_ref[...] = _dot(h, wqk_ref[...]).astype(qk_ref.dtype)
    v = _dot(h, wv_ref[...])
    for p in range(v.shape[1] // PAIR):
        vt_ref[0, p, 0] = v[:, p * PAIR:(p + 1) * PAIR].T.astype(vt_ref.dtype)


def qkv_proj(x, g, w_qkv, tq):
    b, t, d = x.shape
    width = w_qkv.shape[1] // 3
    n_pair = width // PAIR
    nq = t // tq
    wqk, wv = w_qkv[:, :2 * width].astype(BF16), w_qkv[:, 2 * width:].astype(BF16)
    return pl.pallas_call(
        _qkv_proj_kernel, grid=(b * nq,),
        in_specs=[pl.BlockSpec((tq, d), lambda i: (i, 0)), pl.BlockSpec((1, d), lambda i: (0, 0)),
                  pl.BlockSpec(wqk.shape, lambda i: (0, 0)), pl.BlockSpec(wv.shape, lambda i: (0, 0))],
        out_specs=[pl.BlockSpec((tq, 2 * width), lambda i: (i, 0)),
                   pl.BlockSpec((1, n_pair, 1, PAIR, tq), lambda i: (i // nq, 0, i % nq, 0, 0))],
        out_shape=[jax.ShapeDtypeStruct((b * t, 2 * width), BF16),
                   jax.ShapeDtypeStruct((b, n_pair, nq, PAIR, tq), BF16)],
        compiler_params=_cparams(("parallel",)), name="qkv_proj",
    )(x.reshape(b * t, d), g.reshape(1, d), wqk, wv)


def stickbreak_attention(qk, v_t, n_heads, tq):
    b, t, _ = qk.shape
    n_pair = n_heads // 2
    width = n_heads * HEAD_DIM
    gw = SB_PAIRS * PAIR
    n_grp = n_pair // SB_PAIRS
    qkv = qk
    q_spec = pl.BlockSpec((1, tq, gw), lambda i, p, j: (i, j, p))
    k_spec = pl.BlockSpec((1, t, gw), lambda i, p, j: (i, 0, n_grp + p))
    v_spec = pl.BlockSpec((1, SB_PAIRS, t // tq, PAIR, tq), lambda i, p, j: (i, p, 0, 0, 0))
    return pl.pallas_call(
        functools.partial(_stickbreak_kernel, tq=tq, n_pairs=SB_PAIRS), grid=(b, n_grp, t // tq),
        in_specs=[q_spec, k_spec, v_spec], out_specs=q_spec,
        out_shape=jax.ShapeDtypeStruct((b, t, width), BF16),
        scratch_shapes=[pltpu.VMEM((SB_PAIRS, PAIR, tq), F32), pltpu.VMEM((SB_PAIRS, 2, 1, tq), F32)],
        compiler_params=_cparams(("parallel", "parallel", "arbitrary")), name="stickbreak",
    )(qkv, qkv, v_t)


def _proj_residual_kernel(x_ref, y_ref, w_ref, o_ref):
    o_ref[...] = x_ref[...] + _dot(y_ref[...], w_ref[...])


def proj_residual(x2d, y, w, tm=256):
    n, d = x2d.shape
    c = y.shape[1]
    return pl.pallas_call(
        _proj_residual_kernel, grid=(n // tm,),
        in_specs=[pl.BlockSpec((tm, d), lambda i: (i, 0)), pl.BlockSpec((tm, c), lambda i: (i, 0)),
                  pl.BlockSpec(w.shape, lambda i: (0, 0))],
        out_specs=pl.BlockSpec((tm, d), lambda i: (i, 0)), out_shape=jax.ShapeDtypeStruct((n, d), F32),
        compiler_params=_cparams(("parallel",)), name="proj_residual",
    )(x2d, y, w)


def _router_kernel(x_ref, g_ref, wr_hi_ref, wr_lo_ref, h_ref, ids_ref, gates_ref):
    h = _rms(x_ref[...], g_ref[...])
    h_hi = h.astype(BF16)
    h_ref[...] = h_hi
    h_lo = (h - h_hi.astype(F32)).astype(BF16)
    logits = _dot(h_hi, wr_hi_ref[...]) + _dot(h_hi, wr_lo_ref[...]) + _dot(h_lo, wr_hi_ref[...])
    lane = lax.broadcasted_iota(I32, logits.shape, 1)
    lane_f = lane.astype(F32)

    def top1(vals):
        mx = jnp.max(vals, axis=1, keepdims=True)
        return mx, jnp.min(jnp.where(vals == mx, lane_f, 1e9), axis=1, keepdims=True).astype(I32)

    is_g = lane < N_GROUPS
    g_max, g_sel = top1(jnp.where(is_g, logits, -jnp.inf))
    g_gate = 1.0 / jnp.sum(jnp.where(is_g, jnp.exp(logits - g_max), 0.0), axis=1, keepdims=True)
    e_lane = lane - N_GROUPS
    in_grp = (e_lane >= g_sel * EXPERTS_PER_GROUP) & (e_lane < (g_sel + 1) * EXPERTS_PER_GROUP)
    el = jnp.where(in_grp, logits, -jnp.inf)
    m1, i1 = top1(el)
    m2, i2 = top1(jnp.where(lane == i1, -jnp.inf, el))
    e21 = jnp.exp(m2 - m1)
    p1 = 1.0 / (1.0 + e21)
    ids_ref[...] = jnp.where(lane == 0, i1 - N_GROUPS, jnp.where(lane == 1, i2 - N_GROUPS, 0))
    gates_ref[...] = jnp.where(lane == 0, g_gate * p1, jnp.where(lane == 1, g_gate * (e21 * p1), 0.0))


def moe_router(x2d, g, w_router, tm=256):
    n, d = x2d.shape
    tile = lambda c: pl.BlockSpec((tm, c), lambda i: (i, 0))
    w_spec = pl.BlockSpec(w_router.shape, lambda i: (0, 0))
    w_hi = w_router.astype(BF16)
    w_lo = (w_router - w_hi.astype(F32)).astype(BF16)
    return pl.pallas_call(
        _router_kernel, grid=(n // tm,),
        in_specs=[tile(d), pl.BlockSpec((1, d), lambda i: (0, 0)), w_spec, w_spec],
        out_specs=[tile(d), tile(LANES), tile(LANES)],
        out_shape=[jax.ShapeDtypeStruct((n, d), BF16), jax.ShapeDtypeStruct((n, LANES), I32),
                   jax.ShapeDtypeStruct((n, LANES), F32)],
        compiler_params=_cparams(("parallel",)), name="moe_router",
    )(x2d, g.reshape(1, d), w_hi, w_lo)


MOE_TT = 2048
MOE_CH = 128
MOE_KC = 512
_PLAN_BLK = 256


def _moe_plan_kernel(ids_ref, idst_ref, rank_ref, dest_ref, meta_ref, *, tt):
    nb = tt // _PLAN_BLK
    row = lax.broadcasted_iota(I32, (_PLAN_BLK, _PLAN_BLK), 0)
    col = lax.broadcasted_iota(I32, (_PLAN_BLK, _PLAN_BLK), 1)
    earlier_t = (row < col).astype(BF16)
    earlier = (col < row).astype(BF16)
    e_sub = lax.broadcasted_iota(I32, (N_EXPERTS, _PLAN_BLK), 0)
    carry_t = jnp.zeros((N_EXPERTS, 1), F32)
    for b in range(nb):
        sl = slice(b * _PLAN_BLK, (b + 1) * _PLAN_BLK)
        member = (e_sub == idst_ref[0:1, sl]) | (e_sub == idst_ref[1:2, sl])
        m_t = jnp.where(member, 1.0, 0.0)
        before = _dot(m_t.astype(BF16), earlier_t) + carry_t
        rank_ref[0, :, sl] = jnp.where(member, before, -1.0)
        carry_t = carry_t + jnp.sum(m_t, axis=1, keepdims=True)
    lane = lax.broadcasted_iota(I32, (_PLAN_BLK, LANES), 1)
    carry = jnp.zeros((1, LANES), F32)
    prefix = []
    for b in range(nb):
        sl = slice(b * _PLAN_BLK, (b + 1) * _PLAN_BLK)
        m = jnp.where((lane == ids_ref[sl, 0:1]) | (lane == ids_ref[sl, 1:2]), 1.0, 0.0)
        prefix.append(_dot(earlier, m.astype(BF16)) + carry)
        carry = carry + jnp.sum(m, axis=0, keepdims=True)
    counts = carry
    padded = jnp.ceil(counts * (1.0 / MOE_CH)) * MOE_CH
    l_row = lax.broadcasted_iota(I32, (LANES, LANES), 0)
    l_col = lax.broadcasted_iota(I32, (LANES, LANES), 1)
    offs = _dot_hi(padded, (l_row < l_col).astype(F32))
    for b in range(nb):
        sl = slice(b * _PLAN_BLK, (b + 1) * _PLAN_BLK)
        where_row = prefix[b] + offs
        d0 = jnp.sum(jnp.where(lane == ids_ref[sl, 0:1], where_row, 0.0), axis=1, keepdims=True)
        d1 = jnp.sum(jnp.where(lane == ids_ref[sl, 1:2], where_row, 0.0), axis=1, keepdims=True)
        dest_ref[sl, :] = jnp.where(lane == 0, d0, jnp.where(lane == 1, d1, 0.0)).astype(I32)
    r_used = jnp.sum(padded, axis=1, keepdims=True)
    sub8 = lax.broadcasted_iota(I32, (8, LANES), 0)
    meta_ref[0] = jnp.where(sub8 == 0, counts, jnp.where(sub8 == 1, offs, jnp.where(sub8 == 2, r_used, 0.0))).astype(I32)


def moe_plan(ids, ids_t, tt):
    n = ids.shape[0]
    n_tiles = n // tt
    return pl.pallas_call(
        functools.partial(_moe_plan_kernel, tt=tt), grid=(n_tiles,),
        in_specs=[pl.BlockSpec((tt, LANES), lambda i: (i, 0)), pl.BlockSpec((8, tt), lambda i: (0, i))],
        out_specs=[pl.BlockSpec((1, N_EXPERTS, tt), lambda i: (i, 0, 0)), pl.BlockSpec((tt, LANES), lambda i: (i, 0)),
                   pl.BlockSpec((1, 8, LANES), lambda i: (i, 0, 0))],
        out_shape=[jax.ShapeDtypeStruct((n_tiles, N_EXPERTS, tt), F32), jax.ShapeDtypeStruct((n, LANES), I32),
                   jax.ShapeDtypeStruct((n_tiles, 8, LANES), I32)],
        compiler_params=_cparams(("parallel",)), name="moe_plan",
    )(ids, ids_t)


def _pack_bf16_halves(y):
    half = y.shape[1] // 2
    lo = pltpu.bitcast(y[:, :half].astype(BF16).astype(F32), I32)
    hi = pltpu.bitcast(y[:, half:].astype(BF16).astype(F32), I32)
    return lax.shift_right_logical(lo, 16) | hi


def _unpack_bf16_halves(w):
    return pltpu.bitcast(w << 16, F32), pltpu.bitcast(w & jnp.int32(-65536), F32)


def _moe_ffn_kernel(cnt_ref, off_ref, h_ref, rank_ref, idst_ref, gt_ref, w1_ref, w3_ref, w2_ref, y_ref, *, tt):
    t, e = pl.program_id(0), pl.program_id(1)

    @pl.when(e == 0)
    def _():
        y_ref[...] = jnp.zeros_like(y_ref)

    cnt = cnt_ref[t * N_EXPERTS + e]
    off = off_ref[t * N_EXPERTS + e]
    rank_row = rank_ref[0, pl.ds(e, 1), :]
    gate_row = jnp.where(idst_ref[0:1, :] == e, gt_ref[0:1, :], gt_ref[1:2, :])
    sub = lax.broadcasted_iota(I32, (MOE_CH, 1), 0)

    def chunk(c, carry):
        r0 = c * MOE_CH
        pick = rank_row == (r0 + sub).astype(F32)
        x = _dot(jnp.where(pick, 1.0, 0.0).astype(BF16), h_ref[...]).astype(BF16)
        a = _dot(x, w1_ref[0, 0])
        hdn = (a * jax.nn.sigmoid(a)) * _dot(x, w3_ref[0, 0])
        y = _dot(hdn.astype(BF16), w2_ref[0, 0])
        gate = jnp.sum(jnp.where(pick, gate_row, 0.0), axis=1, keepdims=True)
        y_ref[0, pl.ds(pl.multiple_of(off + r0, MOE_CH), MOE_CH), :] = _pack_bf16_halves(y * gate)
        return carry

    lax.fori_loop(0, (cnt + MOE_CH - 1) // MOE_CH, chunk, 0)


def moe_ffn(h, rank_t, ids_t, gates_t, counts, offs, w1, w3, w2, layer, tt):
    n, d = h.shape
    ff = w1.shape[-1]
    n_tiles = n // tt
    r_max = TOP_K_EXPERTS * tt + N_EXPERTS * MOE_CH
    grid_spec = pltpu.PrefetchScalarGridSpec(
        num_scalar_prefetch=2, grid=(n_tiles, N_EXPERTS),
        in_specs=[pl.BlockSpec((tt, d), lambda i, e, c, o: (i, 0)),
                  pl.BlockSpec((1, N_EXPERTS, tt), lambda i, e, c, o: (i, 0, 0)),
                  pl.BlockSpec((8, tt), lambda i, e, c, o: (0, i)),
                  pl.BlockSpec((8, tt), lambda i, e, c, o: (0, i)),
                  pl.BlockSpec((1, 1, d, ff), lambda i, e, c, o: (layer, e, 0, 0)),
                  pl.BlockSpec((1, 1, d, ff), lambda i, e, c, o: (layer, e, 0, 0)),
                  pl.BlockSpec((1, 1, ff, d), lambda i, e, c, o: (layer, e, 0, 0))],
        out_specs=pl.BlockSpec((1, r_max, d // 2), lambda i, e, c, o: (i, 0, 0)))
    return pl.pallas_call(
        functools.partial(_moe_ffn_kernel, tt=tt), grid_spec=grid_spec,
        out_shape=jax.ShapeDtypeStruct((n_tiles, r_max, d // 2), I32),
        compiler_params=_cparams(("arbitrary", "arbitrary")), name="moe_ffn",
    )(counts, offs, h, rank_t, ids_t, gates_t, w1, w3, w2)


def _moe_combine_kernel(dest_ref, x_ref, y_ref, o_ref, ga_ref, gb_ref, *, tm, per):
    base = (pl.program_id(0) * per + pl.program_id(1)) * tm

    def fetch(i, carry):
        tok = (base + i) * TOP_K_EXPERTS
        ga_ref[pl.ds(i, 1), :] = y_ref[0, pl.ds(dest_ref[tok], 1), :]
        gb_ref[pl.ds(i, 1), :] = y_ref[0, pl.ds(dest_ref[tok + 1], 1), :]
        return carry

    lax.fori_loop(0, tm, fetch, 0, unroll=8)
    half = x_ref.shape[1] // 2
    lo_a, hi_a = _unpack_bf16_halves(ga_ref[...])
    lo_b, hi_b = _unpack_bf16_halves(gb_ref[...])
    o_ref[:, :half] = x_ref[:, :half] + (lo_a + lo_b)
    o_ref[:, half:] = x_ref[:, half:] + (hi_a + hi_b)


def moe_combine(x2d, dest_flat, y_all, tt, tm=256):
    n, d = x2d.shape
    r_max = y_all.shape[1]
    per = tt // tm
    grid_spec = pltpu.PrefetchScalarGridSpec(
        num_scalar_prefetch=1, grid=(n // tt, per),
        in_specs=[pl.BlockSpec((tm, d), lambda i, j, u: (i * per + j, 0)),
                  pl.BlockSpec((1, r_max, d // 2), lambda i, j, u: (i, 0, 0))],
        out_specs=pl.BlockSpec((tm, d), lambda i, j, u: (i * per + j, 0)),
        scratch_shapes=[pltpu.VMEM((tm, d // 2), I32), pltpu.VMEM((tm, d // 2), I32)])
    return pl.pallas_call(
        functools.partial(_moe_combine_kernel, tm=tm, per=per), grid_spec=grid_spec,
        out_shape=jax.ShapeDtypeStruct((n, d), F32),
        compiler_params=_cparams(("arbitrary", "arbitrary")), name="moe_combine",
    )(dest_flat, x2d, y_all)


def _final_norm_kernel(x_ref, g_ref, o_ref):
    o_ref[...] = _rms(x_ref[...], g_ref[...])


def final_norm(x2d, g, tm=512):
    n, d = x2d.shape
    return pl.pallas_call(
        _final_norm_kernel, grid=(n // tm,),
        in_specs=[pl.BlockSpec((tm, d), lambda i: (i, 0)), pl.BlockSpec((1, d), lambda i: (0, 0))],
        out_specs=pl.BlockSpec((tm, d), lambda i: (i, 0)), out_shape=jax.ShapeDtypeStruct((n, d), F32),
        compiler_params=_cparams(("parallel",)), name="final_norm",
    )(x2d, g.reshape(1, d))


def moe_layer(x2d, norm_g, router_group, router_expert, w1, w3, w2, layer):
    n, d = x2d.shape
    tt = min(MOE_TT, n)
    w_router = jnp.concatenate([router_group, jnp.moveaxis(router_expert, 0, 1).reshape(d, N_EXPERTS)], axis=1)
    h, ids, gates = moe_router(x2d, norm_g, _pad_cols(w_router, LANES))
    ids_t = jnp.pad(ids[:, :TOP_K_EXPERTS].T, ((0, 8 - TOP_K_EXPERTS), (0, 0)))
    gates_t = jnp.pad(gates[:, :TOP_K_EXPERTS].T, ((0, 8 - TOP_K_EXPERTS), (0, 0)))
    rank_t, dest, meta = moe_plan(ids, ids_t, tt)
    counts = meta[:, 0, :N_EXPERTS].reshape(-1)
    offs = meta[:, 1, :N_EXPERTS].reshape(-1)
    y_all = moe_ffn(h, rank_t, ids_t, gates_t, counts, offs, w1, w3, w2, layer, tt)
    return moe_combine(x2d, dest[:, :TOP_K_EXPERTS].reshape(-1), y_all, tt)


def even_layer(x, v_first, p):
    b, t, d = x.shape
    n = b * t
    w_in = p['w_in']
    aw = p['k_k'].shape[0]
    bw = p['w_kv_up'].shape[1] // 2
    rank = p['w_kv_up'].shape[0]
    n_idx = w_in.shape[1] - (3 * aw + bw + rank + bw + HEAD_DIM)
    o = 0
    segs = []
    for width in (3 * aw, bw, rank, bw, HEAD_DIM, n_idx):
        segs.append(w_in[:, o:o + width])
        o += width
    segs[4] = jnp.concatenate([segs[4], segs[4]], axis=1)
    segs[5] = _pad_cols(segs[5], LANES)
    rkv, q_b, c_kv, q_idx, k_dup, w_idx = norm_proj(
        x.reshape(n, d), p['norm'], [s.astype(BF16) for s in segs], [F32] * 6)

    r, lw, k_mod, v, kk, iclr, gate, bonus = rwkv_prep(x, rkv.reshape(b, t, 3 * aw), v_first, p)
    y = rwkv_scan(r, lw, k_mod, v, kk, iclr)

    cn, kn = kv_prep(c_kv, k_dup, p['kv_norm'], jnp.concatenate([p['idx_k_norm']] * 2))
    wk_t = p['w_kv_up'][:, :bw].astype(BF16)
    wv = p['w_kv_up'][:, bw:].astype(BF16)
    y_b = dsa_attention(q_idx.reshape(b, t, bw), w_idx.reshape(b, t, LANES), q_b.reshape(b, t, bw),
                        kn.reshape(b, t, LANES), cn.reshape(b, t, rank), wk_t, wv)

    x_new = even_out(x.reshape(n, d), y.reshape(n, aw), bonus.reshape(n, aw), gate.reshape(n, aw),
                     y_b.reshape(n, bw), p['ln_g'], p['ln_b'], p['w_out'])
    return x_new.reshape(b, t, d), v


def odd_layer(x, norm_g, w_qkv, w_out):
    b, t, d = x.shape
    n = b * t
    cw = w_qkv.shape[1] // 3
    tq = min(256, t)
    qk, v_t = qkv_proj(x, norm_g, w_qkv, tq)
    y_c = stickbreak_attention(qk.reshape(b, t, 2 * cw), v_t, cw // HEAD_DIM, tq)
    return proj_residual(x.reshape(n, d), y_c.reshape(n, cw), w_out.astype(BF16)).reshape(b, t, d)


@jax.jit
def _forward(x, ev_norm, ev_w_in, ev_w_out, a_mu_rkv, a_mu_lora, a_w0, a_w1, a_w2, a_a0, a_a1, a_a2, a_g1,
             a_g2, a_k_k, a_k_a, a_r_k, a_ln_g, a_ln_b, a_mu_vres, a_v0, a_v1, a_v2, b_kv_norm, b_w_kv_up,
             b_idx_k_norm, od_norm, od_w_qkv, od_w_out, moe_norm, moe_router_group, moe_router_expert,
             moe_w1, moe_w3, moe_w2, final_norm_g):
    b, t, d = x.shape
    depth = moe_norm.shape[0]
    w1_bf16, w3_bf16, w2_bf16 = moe_w1.astype(BF16), moe_w3.astype(BF16), moe_w2.astype(BF16)
    v_first = None
    for i in range(depth):
        if i % 2 == 0:
            e = i // 2
            p = dict(norm=ev_norm[e], w_in=ev_w_in[e], w_out=ev_w_out[e], mu_rkv=a_mu_rkv[e], mu_lora=a_mu_lora[e],
                     w0=a_w0[e], w1=a_w1[e], w2=a_w2[e], a0=a_a0[e], a1=a_a1[e], a2=a_a2[e], g1=a_g1[e], g2=a_g2[e],
                     k_k=a_k_k[e], k_a=a_k_a[e], r_k=a_r_k[e].reshape(-1), ln_g=a_ln_g[e], ln_b=a_ln_b[e],
                     kv_norm=b_kv_norm[e], w_kv_up=b_w_kv_up[e], idx_k_norm=b_idx_k_norm[e])
            if e > 0:
                p.update(mu_v=a_mu_vres[e - 1], v0=a_v0[e - 1], v1=a_v1[e - 1], v2=a_v2[e - 1])
            x, v_used = even_layer(x, v_first if e > 0 else None, p)
            if e == 0:
                v_first = v_used
        else:
            o = i // 2
            x = odd_layer(x, od_norm[o], od_w_qkv[o], od_w_out[o])
        x = moe_layer(x.reshape(b * t, d), moe_norm[i], moe_router_group[i], moe_router_expert[i],
                      w1_bf16, w3_bf16, w2_bf16, i).reshape(b, t, d)
    return final_norm(x.reshape(b * t, d), final_norm_g).reshape(b, t, d)


def kernel(x, ev_norm, ev_w_in, ev_w_out, a_mu_rkv, a_mu_lora, a_w0, a_w1, a_w2, a_a0, a_a1, a_a2, a_g1, a_g2, a_k_k, a_k_a, a_r_k, a_ln_g, a_ln_b, a_mu_vres, a_v0, a_v1, a_v2, b_kv_norm, b_w_kv_up, b_idx_k_norm, od_norm, od_w_qkv, od_w_out, moe_norm, moe_router_group, moe_router_expert, moe_w1, moe_w3, moe_w2, final_norm):
    return _forward(x, ev_norm, ev_w_in, ev_w_out, a_mu_rkv, a_mu_lora, a_w0, a_w1, a_w2, a_a0, a_a1, a_a2, a_g1,
                    a_g2, a_k_k, a_k_a, a_r_k, a_ln_g, a_ln_b, a_mu_vres, a_v0, a_v1, a_v2, b_kv_norm, b_w_kv_up,
                    b_idx_k_norm, od_norm, od_w_qkv, od_w_out, moe_norm, moe_router_group, moe_router_expert,
                    moe_w1, moe_w3, moe_w2, final_norm)
```

```python
import functools
import math

import jax
import jax.numpy as jnp
from jax import lax
from jax.experimental import pallas as pl
from jax.experimental.pallas import tpu as pltpu

F32 = jnp.float32
BF16 = jnp.bfloat16
I32 = jnp.int32
HI = lax.Precision.HIGHEST

LANES = 128
HEAD_DIM = 64
PAIR = 2 * HEAD_DIM
RMS_EPS = 1e-6
GN_EPS = 64e-5
N_GROUPS = 4
EXPERTS_PER_GROUP = 8
N_EXPERTS = N_GROUPS * EXPERTS_PER_GROUP
TOP_K_EXPERTS = 2
IDX_TOPK_MAX = 256
INT_MIN = -2147483648
NEG_BIG = -1e30
VMEM_LIMIT = 56 * 1024 * 1024

_NT = (((1,), (1,)), ((), ()))
_TN = (((0,), (0,)), ((), ()))


def _cparams(sem):
    return pltpu.CompilerParams(dimension_semantics=sem, vmem_limit_bytes=VMEM_LIMIT)


def _rms(x, g):
    return x * lax.rsqrt(jnp.mean(x * x, axis=-1, keepdims=True) + RMS_EPS) * g


def _dot(a, b):
    return jnp.dot(a, b, preferred_element_type=F32)


def _dot_hi(a, b):
    return jnp.dot(a, b, preferred_element_type=F32, precision=HI)


def _seg_dot(x, seg):
    hi = x.astype(BF16)
    return _dot(hi, seg) + _dot((x - hi.astype(F32)).astype(BF16), seg)


def _dot_nt(a, b):
    return lax.dot_general(a, b, _NT, preferred_element_type=F32)


def _lane_is_first_head(width=PAIR):
    return lax.broadcasted_iota(I32, (1, width), 1) % PAIR < HEAD_DIM


def _norm_proj_kernel(x_ref, g_ref, *refs, n_seg):
    w_refs, o_refs = refs[:n_seg], refs[n_seg:]
    h = _rms(x_ref[...], g_ref[...]).astype(BF16)
    for w_ref, o_ref in zip(w_refs, o_refs):
        o_ref[...] = _dot(h, w_ref[...]).astype(o_ref.dtype)


def norm_proj(x2d, g, ws, out_dtypes, tm=256):
    n, d = x2d.shape
    in_specs = [pl.BlockSpec((tm, d), lambda i: (i, 0)), pl.BlockSpec((1, d), lambda i: (0, 0))]
    in_specs += [pl.BlockSpec(w.shape, lambda i: (0, 0)) for w in ws]
    out_specs = [pl.BlockSpec((tm, w.shape[1]), lambda i: (i, 0)) for w in ws]
    out_shape = [jax.ShapeDtypeStruct((n, w.shape[1]), dt) for w, dt in zip(ws, out_dtypes)]
    return pl.pallas_call(
        functools.partial(_norm_proj_kernel, n_seg=len(ws)),
        grid=(n // tm,), in_specs=in_specs, out_specs=out_specs, out_shape=out_shape,
        compiler_params=_cparams(("parallel",)), name="norm_proj",
    )(x2d, g.reshape(1, d), *ws)


def _shift_rows(z, carry_ref):
    tm = z.shape[0]
    first = lax.broadcasted_iota(I32, (tm, 1), 0) == 0
    prev = jnp.where(first, carry_ref[...], pltpu.roll(z, 1, 0))
    carry_ref[...] = z[tm - 1:tm, :]
    return prev


def _rwkv_prep_kernel(*refs, has_vres):
    it = iter(refs)
    x_ref, rkv_ref = next(it), next(it)
    vfirst_ref = next(it) if has_vres else None
    g_ref, mu_rkv_ref, mu_lora_ref = next(it), next(it), next(it)
    w0_ref, w1_ref, w2_ref = next(it), next(it), next(it)
    a0_ref, a1_ref, a2_ref = next(it), next(it), next(it)
    g1_ref, g2_ref = next(it), next(it)
    if has_vres:
        muv_ref, v0_ref, v1_ref, v2_ref = next(it), next(it), next(it), next(it)
    kk_w_ref, ka_ref, rk_ref, seg_ref = next(it), next(it), next(it), next(it)
    r_o, lw_o, k_o, v_o, kk_o, a_o, gate_o, bonus_o = (next(it) for _ in range(8))
    hcarry, rkvcarry = next(it), next(it)

    @pl.when(pl.program_id(1) == 0)
    def _():
        hcarry[...] = jnp.zeros_like(hcarry)
        rkvcarry[...] = jnp.zeros_like(rkvcarry)

    aw = r_o.shape[-1]
    h = _rms(x_ref[0], g_ref[...])
    dh = _shift_rows(h, hcarry) - h
    rkv = rkv_ref[0]
    rkv = rkv + (_shift_rows(rkv, rkvcarry) - rkv) * mu_rkv_ref[...]
    r, k, v = rkv[:, :aw], rkv[:, aw:2 * aw], rkv[:, 2 * aw:]

    def lora_in(row):
        return (h + dh * mu_lora_ref[row:row + 1, :]).astype(BF16)

    dec = w0_ref[...] + _dot(jnp.tanh(_dot(lora_in(0), w1_ref[...])).astype(BF16), w2_ref[...])
    logw = -math.exp(-0.5) * jax.nn.sigmoid(dec)
    iclr = jax.nn.sigmoid(a0_ref[...] + _dot(_dot(lora_in(1), a1_ref[...]).astype(BF16), a2_ref[...]))
    gate = _dot(jax.nn.sigmoid(_dot(lora_in(2), g1_ref[...])).astype(BF16), g2_ref[...])
    if has_vres:
        xv = (h + dh * muv_ref[...]).astype(BF16)
        mix = jax.nn.sigmoid(v0_ref[...] + _dot(_dot(xv, v1_ref[...]).astype(BF16), v2_ref[...]))
        v = v + (vfirst_ref[0] - v) * mix

    seg = seg_ref[...]
    kk = k * kk_w_ref[...]
    kk = kk * lax.rsqrt(jnp.maximum(_seg_dot(kk * kk, seg), 1e-12))
    k_mod = k * (1.0 + (iclr - 1.0) * ka_ref[...])
    bonus = _seg_dot(r * k_mod * rk_ref[...], seg) * v

    r_o[0], lw_o[0], k_o[0], v_o[0] = r, logw, k_mod, v
    kk_o[0], a_o[0], gate_o[0], bonus_o[0] = kk, iclr, gate, bonus


def _pad_cols(w, n):
    return jnp.pad(w, ((0, 0), (0, n - w.shape[1])))


def _pad_rows(w, n):
    return jnp.pad(w, ((0, n - w.shape[0]), (0, 0)))


def _seg_ones(width):
    i = jnp.arange(width) // HEAD_DIM
    return (i[:, None] == i[None, :]).astype(BF16)


def rwkv_prep(x, rkv, v_first, p, tm=256):
    b, t, d = x.shape
    aw = rkv.shape[-1] // 3
    has_vres = v_first is not None
    row = lambda a: a.reshape(1, -1)
    const = lambda a: pl.BlockSpec(a.shape, lambda i, j: (0,) * a.ndim)
    tile = lambda c: pl.BlockSpec((1, tm, c), lambda i, j: (i, j, 0))

    args, specs = [x, rkv], [tile(d), tile(3 * aw)]
    if has_vres:
        args.append(v_first)
        specs.append(tile(aw))
    consts = [row(p['norm']), p['mu_rkv'].reshape(1, 3 * aw), p['mu_lora'],
              row(p['w0']), _pad_cols(p['w1'], LANES).astype(BF16), _pad_rows(p['w2'], LANES).astype(BF16),
              row(p['a0']), _pad_cols(p['a1'], LANES).astype(BF16), _pad_rows(p['a2'], LANES).astype(BF16),
              p['g1'].astype(BF16), p['g2'].astype(BF16)]
    if has_vres:
        consts += [row(p['mu_v']), row(p['v0']), _pad_cols(p['v1'], LANES).astype(BF16),
                   _pad_rows(p['v2'], LANES).astype(BF16)]
    consts += [row(p['k_k']), row(p['k_a']), row(p['r_k']), _seg_ones(aw)]
    args += consts
    specs += [const(a) for a in consts]
    out_shape = [jax.ShapeDtypeStruct((b, t, aw), F32)] * 8
    return pl.pallas_call(
        functools.partial(_rwkv_prep_kernel, has_vres=has_vres),
        grid=(b, t // tm), in_specs=specs, out_specs=[tile(aw)] * 8, out_shape=out_shape,
        scratch_shapes=[pltpu.VMEM((1, d), F32), pltpu.VMEM((1, 3 * aw), F32)],
        compiler_params=_cparams(("arbitrary", "arbitrary")), name="rwkv_prep",
    )(*args)


def _split_bf16(x):
    hi = x.astype(BF16)
    return hi, (x - hi.astype(F32)).astype(BF16)


def _mm3(a, b, dims=None):
    a_hi, a_lo = _split_bf16(a)
    b_hi, b_lo = _split_bf16(b)
    if dims is None:
        f = lambda p, q: jnp.dot(p, q, preferred_element_type=F32)
    else:
        f = lambda p, q: lax.dot_general(p, q, dims, preferred_element_type=F32)
    return f(a_hi, b_hi) + f(a_hi, b_lo) + f(a_lo, b_hi)


def _mmc(a, b, dims=None):
    a, b = a.astype(BF16), b.astype(BF16)
    if dims is None:
        return jnp.dot(a, b, preferred_element_type=F32)
    return lax.dot_general(a, b, dims, preferred_element_type=F32)


def _cumsum_rows(x):
    n = x.shape[0]
    row = lax.broadcasted_iota(I32, (n, 1), 0)
    s = 1
    while s < n:
        x = x + jnp.where(row >= s, pltpu.roll(x, s, 0), 0.0)
        s *= 2
    return x


def _rwkv_chunk_kernel(r_ref, lw_ref, k_ref, v_ref, kk_ref, a_ref, qh_ref, y0_ref, m_ref, c0_ref, *, chunk, cps):
    L = chunk
    n_pair = r_ref.shape[-1] // PAIR
    row = lax.broadcasted_iota(I32, (L, L), 0)
    col = lax.broadcasted_iota(I32, (L, L), 1)
    strict = col < row
    incl = col <= row
    eye = (col == row).astype(F32)
    first = _lane_is_first_head()
    p_row = lax.broadcasted_iota(I32, (PAIR, PAIR), 0)
    p_col = lax.broadcasted_iota(I32, (PAIR, PAIR), 1)
    blockdiag = p_row // HEAD_DIM == p_col // HEAD_DIM
    pair_eye = p_row == p_col

    units = [(ci, p) for ci in range(cps) for p in range(n_pair)]
    heads = (first, ~first)
    pre = {}
    for u in units:
        ci, p = u
        rows = slice(ci * L, (ci + 1) * L)
        sl = slice(p * PAIR, (p + 1) * PAIR)
        lw = lw_ref[0, rows, sl]
        c = _cumsum_rows(lw)
        g = jnp.exp(c)
        g_inv = jnp.exp(-c)
        kk = kk_ref[0, rows, sl]
        pre[u] = dict(rows=rows, sl=sl, g_last=g[L - 1:L, :],
                      at=-kk * jnp.exp(c - lw),
                      bt=kk * a_ref[0, rows, sl] * g_inv,
                      kt=k_ref[0, rows, sl] * g_inv,
                      rt=r_ref[0, rows, sl] * g,
                      vv=v_ref[0, rows, sl])
    uh = [(u, h) for u in units for h in range(2)]

    gm = {}
    for u, h in uh:
        d = pre[u]
        lhs = jnp.concatenate([jnp.where(heads[h], d['at'], 0.0), jnp.where(heads[h], d['rt'], 0.0)], axis=0)
        gm[u, h] = _mm3(lhs, jnp.concatenate([d['bt'], d['kt']], axis=0), _NT)
    a_ak = {x: jnp.where(strict, gm[x][:L, L:], 0.0) for x in uh}
    a_rb = {x: jnp.where(incl, gm[x][L:, :L], 0.0) for x in uh}
    a_rk = {x: jnp.where(incl, gm[x][L:, L:], 0.0) for x in uh}
    apow = {x: jnp.where(strict, gm[x][:L, :L], 0.0) for x in uh}
    inv = {x: eye + apow[x] for x in uh}
    for _ in range(max(1, (L - 1).bit_length()) - 1):
        apow = {x: _mmc(apow[x], apow[x]) for x in uh}
        inv = {x: inv[x] + _mmc(inv[x], apow[x]) for x in uh}
    w_h = {(u, h): _mmc(inv[u, h], pre[u]['at']) for u, h in uh}
    akv = {(u, h): _mmc(a_ak[u, h], pre[u]['vv']) for u, h in uh}
    u0_h = {x: _mmc(inv[x], akv[x]) for x in uh}
    w = {u: jnp.where(first, w_h[u, 0], w_h[u, 1]) for u in units}
    u0 = {u: jnp.where(first, u0_h[u, 0], u0_h[u, 1]) for u in units}
    wu = {u: jnp.concatenate([w[u], u0[u]], axis=1) for u in units}
    arb = {(u, h): _mmc(a_rb[u, h], wu[u]) for u, h in uh}
    ark = {(u, h): _mmc(a_rk[u, h], pre[u]['vv']) for u, h in uh}
    first2 = jnp.concatenate([first, first], axis=1)
    for u in units:
        d = pre[u]
        ci, p = u
        arb_w = jnp.where(first2, arb[u, 0], arb[u, 1])
        qh_ref[0, d['rows'], d['sl']] = d['rt'] + arb_w[:, :PAIR]
        y0_ref[0, d['rows'], d['sl']] = arb_w[:, PAIR:] + jnp.where(first, ark[u, 0], ark[u, 1])
        bg = d['bt'] * d['g_last']
        m_kk = _mmc(bg, w[u], _TN)
        m_ref[0, ci, p] = jnp.where(pair_eye, d['g_last'], 0.0) + jnp.where(blockdiag, m_kk, 0.0)
        c0 = _mmc(jnp.concatenate([u0[u], d['vv']], axis=0),
                  jnp.concatenate([bg, d['kt'] * d['g_last']], axis=0), _TN)
        c0_ref[0, ci, p] = jnp.where(blockdiag, c0, 0.0)


def _rwkv_state_kernel(qh_ref, y0_ref, m_ref, c0_ref, y_ref, s_ref, *, chunk, cps):
    L = chunk

    @pl.when(pl.program_id(0) == 0)
    def _():
        s_ref[...] = jnp.zeros_like(s_ref)

    n_batch = qh_ref.shape[0]
    n_pair = qh_ref.shape[-1] // PAIR
    streams = [(b, p) for b in range(n_batch) for p in range(n_pair)]
    s2 = {x: s_ref[x[0], x[1]] for x in streams}
    for ci in range(cps):
        rows = slice(ci * L, (ci + 1) * L)
        for b, p in streams:
            sl = slice(p * PAIR, (p + 1) * PAIR)
            y_ref[b, rows, sl] = _mm3(qh_ref[b, rows, sl], s2[b, p], _NT) + y0_ref[b, rows, sl]
            s2[b, p] = _mm3(s2[b, p], m_ref[b, ci, p], _NT) + c0_ref[b, ci, p]
    for b, p in streams:
        s_ref[b, p] = s2[b, p]


def rwkv_scan(r, lw, k, v, kk, a, chunk=64, cps_a=4, cps_b=4):
    b, t, aw = r.shape
    n_pair = aw // PAIR
    n_chunk = t // chunk
    cps_a, cps_b = min(cps_a, n_chunk), min(cps_b, n_chunk)
    rows = lambda c: pl.BlockSpec((1, c * chunk, aw), lambda i, j: (i, j, 0))
    mats = lambda c: pl.BlockSpec((1, c, n_pair, PAIR, PAIR), lambda i, j: (i, j, 0, 0, 0))
    seq = jax.ShapeDtypeStruct((b, t, aw), F32)
    mat = jax.ShapeDtypeStruct((b, n_chunk, n_pair, PAIR, PAIR), F32)
    qh, y0, m, c0 = pl.pallas_call(
        functools.partial(_rwkv_chunk_kernel, chunk=chunk, cps=cps_a),
        grid=(b, n_chunk // cps_a), in_specs=[rows(cps_a)] * 6,
        out_specs=[rows(cps_a), rows(cps_a), mats(cps_a), mats(cps_a)], out_shape=[seq, seq, mat, mat],
        compiler_params=_cparams(("parallel", "parallel")), name="rwkv_chunk",
    )(r, lw, k, v, kk, a)
    rows_b = pl.BlockSpec((b, cps_b * chunk, aw), lambda j: (0, j, 0))
    mats_b = pl.BlockSpec((b, cps_b, n_pair, PAIR, PAIR), lambda j: (0, j, 0, 0, 0))
    return pl.pallas_call(
        functools.partial(_rwkv_state_kernel, chunk=chunk, cps=cps_b),
        grid=(n_chunk // cps_b,), in_specs=[rows_b, rows_b, mats_b, mats_b],
        out_specs=rows_b, out_shape=seq,
        scratch_shapes=[pltpu.VMEM((b, n_pair, PAIR, PAIR), F32)],
        compiler_params=_cparams(("arbitrary",)), name="rwkv_state",
    )(qh, y0, m, c0)


def _kv_prep_kernel(c_ref, k_ref, gc_ref, gk_ref, cn_ref, kn_ref):
    cn_ref[...] = _rms(c_ref[...], gc_ref[...]).astype(cn_ref.dtype)
    kn_ref[...] = _rms(k_ref[...], gk_ref[...]).astype(kn_ref.dtype)


def kv_prep(c_kv, k_dup, g_c, g_k_dup, tm=512):
    n, w = c_kv.shape
    tile = pl.BlockSpec((tm, w), lambda i: (i, 0))
    const = pl.BlockSpec((1, w), lambda i: (0, 0))
    return pl.pallas_call(
        _kv_prep_kernel, grid=(n // tm,), in_specs=[tile, tile, const, const], out_specs=[tile, tile],
        out_shape=[jax.ShapeDtypeStruct((n, w), BF16)] * 2,
        compiler_params=_cparams(("parallel",)), name="kv_prep",
    )(c_kv, k_dup, g_c.reshape(1, w), g_k_dup.reshape(1, w))


SEARCH_VALUE_STEPS = 24
SEARCH_STEPS_PER_CHECK = 2


def _tree_reduce(combine, xs):
    while len(xs) > 1:
        xs = [combine(xs[i], xs[i + 1]) for i in range(0, len(xs) - 1, 2)] + ([xs[-1]] if len(xs) % 2 else [])
    return xs[0]


def _key_to_f32(key):
    return pltpu.bitcast(jnp.where(key >= 0, key, key ^ 0x7FFFFFFF), F32)


def _f32_to_key(x):
    bits = pltpu.bitcast(x, I32)
    return jnp.where(bits >= 0, bits, bits ^ 0x7FFFFFFF)


def _dsa_kernel(qi_ref, wit_ref, q_ref, kn_ref, cn_ref, cnt_ref, wk_ref, wv_ref, o_ref,
                skey_ref, qpt_ref, acc_ref, m_ref, l_ref, *, qb, kt, ka, topk, n_heads, idx_bits):
    blk = pl.program_id(1)
    nt = ((blk + 1) * qb + kt - 1) // kt
    first = _lane_is_first_head()
    first_sub = lax.broadcasted_iota(I32, (PAIR, 1), 0) < HEAD_DIM
    q_pos = blk * qb + lax.broadcasted_iota(I32, (1, qb), 1)
    row0 = lax.broadcasted_iota(I32, (kt, 1), 0)
    heads = range(n_heads)

    qi = qi_ref[0]
    wit = wit_ref[0]
    qi_t = []
    for p in range(n_heads // 2):
        pair_t = qi[:, p * PAIR:(p + 1) * PAIR].T * HEAD_DIM ** -0.5
        qi_t += [jnp.where(first_sub, pair_t, 0.0).astype(BF16), jnp.where(first_sub, 0.0, pair_t).astype(BF16)]

    def score_tile(j, carry):
        off = pl.multiple_of(j * kt, kt)
        kn = kn_ref[0, pl.ds(off, kt), :]
        sc = jnp.zeros((kt, qb), F32)
        for h in heads:
            sc = sc + wit[h:h + 1, :] * jnp.maximum(_dot(kn, qi_t[h]), 0.0)
        sc = sc * n_heads ** -0.5
        key = jnp.where(sc == 0.0, 0, _f32_to_key(sc))
        causal = off + row0 <= q_pos
        skey_ref[j] = jnp.where(causal, key, INT_MIN)
        hi8 = _tree_reduce(jnp.maximum, list(jnp.where(causal, sc, -jnp.inf).reshape(kt // 8, 8, qb)))
        lo8 = _tree_reduce(jnp.minimum, list(jnp.where(causal, sc, jnp.inf).reshape(kt // 8, 8, qb)))
        return jnp.maximum(carry[0], hi8), jnp.minimum(carry[1], lo8)

    hi8, lo8 = lax.fori_loop(0, nt, score_tile,
                             (jnp.full((8, qb), -jnp.inf, F32), jnp.full((8, qb), jnp.inf, F32)))

    def count(pred):
        def hits(j):
            part = jnp.where(pred(skey_ref[j], j * kt + row0), 1.0, 0.0).reshape(kt // 8, 8, qb)
            return _tree_reduce(jnp.add, [part[i] for i in range(kt // 8)])

        def body(i, acc):
            j1 = 2 * i + 1
            second = jnp.where(j1 < nt, 1.0, 0.0)
            return acc + hits(2 * i) + hits(jnp.minimum(j1, nt - 1)) * second

        acc = lax.fori_loop(0, (nt + 1) // 2, body, jnp.zeros((8, qb), F32))
        return jnp.sum(acc, axis=0, keepdims=True)

    few = q_pos + 1 <= topk
    v_max = jnp.max(hi8, axis=0, keepdims=True)
    v_min = jnp.min(lo8, axis=0, keepdims=True)
    lo0 = _f32_to_key(jnp.where(v_min == 0.0, 0.0, v_min))
    hi0 = _f32_to_key(jnp.where(v_max == 0.0, 0.0, v_max)) + 1
    cnt0 = (q_pos + 1).astype(F32)

    def unresolved(lo, hi, cnt_lo):
        return ~(few | (cnt_lo == topk) | (hi <= lo + 1))

    def any_open(lo, hi, cnt_lo):
        return jnp.max(jnp.where(unresolved(lo, hi, cnt_lo), 1.0, 0.0))

    def search_cond(st):
        return (st[0] < SEARCH_VALUE_STEPS + 34) & (st[4] > 0.0)

    def search_step(st):
        it, lo, hi, cnt_lo, _ = st
        for _ in range(SEARCH_STEPS_PER_CHECK):
            mid_v = _f32_to_key(0.5 * _key_to_f32(lo) + 0.5 * _key_to_f32(hi))
            mid_k = (lo >> 1) + (hi >> 1) + (lo & hi & 1)
            mid = jnp.where((it < SEARCH_VALUE_STEPS) & (mid_v > lo) & (mid_v < hi), mid_v, mid_k)
            mid = jnp.where((it == 0) & (lo < 0) & (hi > 0), 0, mid)
            mid = jnp.where((it == 1) & (lo == 0) & (hi > 1), 1, mid)
            cnt = count(lambda sk, _: sk >= mid)
            open_ = unresolved(lo, hi, cnt_lo)
            up = open_ & (cnt >= topk)
            lo, cnt_lo = jnp.where(up, mid, lo), jnp.where(up, cnt, cnt_lo)
            hi = jnp.where(open_ & ~up, mid, hi)
            it = it + 1
        return it, lo, hi, cnt_lo, any_open(lo, hi, cnt_lo)

    st0 = (jnp.int32(0), lo0, hi0, cnt0, any_open(lo0, hi0, cnt0))
    _, lo, _, cnt_lo, _ = lax.while_loop(search_cond, search_step, st0)
    thr = jnp.where(few, INT_MIN, lo)
    tie = (cnt_lo > topk) & ~few
    any_tie = jnp.max(jnp.where(tie, 1, 0))
    n_gt = lax.fori_loop(0, any_tie, lambda i, c: count(lambda sk, _: sk > thr), jnp.zeros((1, qb), F32))
    need = topk - n_gt

    def idx_bit(i, prefix):
        cand = prefix | jnp.left_shift(jnp.int32(1), idx_bits - 1 - i)
        cnt = count(lambda sk, kidx: (sk == thr) & (kidx < cand))
        return jnp.where(cnt < need, cand, prefix)

    jmax = lax.fori_loop(0, idx_bits * any_tie, idx_bit, jnp.zeros((1, qb), I32))
    jmax = jnp.where(tie, jmax, 0x7FFFFFFF)

    q = q_ref[0]
    for h in heads:
        pr = slice((h // 2) * PAIR, (h // 2 + 1) * PAIR)
        wk_h = jnp.where(first if h % 2 == 0 else ~first, wk_ref[:, pr], 0.0)
        qpt_ref[h] = (_dot_nt(wk_h, q[:, pr].astype(BF16)) * (HEAD_DIM ** -0.5 * math.log2(math.e))).astype(BF16)
    m_ref[...] = jnp.full_like(m_ref, NEG_BIG)
    l_ref[...] = jnp.zeros_like(l_ref)
    acc_ref[...] = jnp.zeros_like(acc_ref)

    def attn_tile(j, carry):
        sk_all = skey_ref[j]
        for part in range(kt // ka):
            off = pl.multiple_of(j * kt + part * ka, ka)
            sk = sk_all[part * ka:(part + 1) * ka, :]
            kidx = off + row0[:ka]
            sel = ((sk > thr) | ((sk == thr) & (kidx <= jmax))) & (kidx <= q_pos)
            bias = jnp.where(sel, 0.0, NEG_BIG)
            cn = cn_ref[0, pl.ds(off, ka), :]
            cn_t = cnt_ref[0, j * (kt // ka) + part]
            s = [_dot(cn, qpt_ref[h]) + bias for h in heads]
            m_old = [m_ref[h] for h in heads]
            m_new = [jnp.maximum(m_old[h], jnp.max(s[h], axis=0, keepdims=True)) for h in heads]
            pexp = [jnp.exp2(s[h] - m_new[h]) for h in heads]
            alpha = [jnp.exp2(m_old[h] - m_new[h]) for h in heads]
            pv = [_dot(cn_t, pexp[h].astype(BF16)) for h in heads]
            for h in heads:
                l_ref[h] = alpha[h] * l_ref[h] + jnp.sum(pexp[h], axis=0, keepdims=True)
                acc_ref[h] = alpha[h] * acc_ref[h] + pv[h]
                m_ref[h] = m_new[h]
        return carry

    lax.fori_loop(0, nt, attn_tile, 0)

    rank = cn_ref.shape[-1]
    for p in range(n_heads // 2):
        pr = slice(p * PAIR, (p + 1) * PAIR)
        out = jnp.zeros((qb, PAIR), F32)
        for h in (2 * p, 2 * p + 1):
            lat_t = (acc_ref[h] / l_ref[h]).astype(BF16)
            wv_h = jnp.where(first if h % 2 == 0 else ~first, wv_ref[:, pr], 0.0)
            out = out + lax.dot_general(lat_t, wv_h, _TN, preferred_element_type=F32)
        o_ref[0, :, pr] = out.astype(o_ref.dtype)


def dsa_attention(q_idx, w_idx, q_b, kn, cn, wk_t, wv, qb=128, kt=512, ka=256):
    b, t, width = q_b.shape
    n_heads = width // HEAD_DIM
    rank = cn.shape[-1]
    kt, ka = min(kt, t), min(ka, t)
    topk = min(IDX_TOPK_MAX, t // 4)
    w_t = jnp.swapaxes(w_idx[:, :, :n_heads], 1, 2)
    cn_t = jnp.swapaxes(cn.reshape(b, t // ka, ka, rank), 2, 3)
    qtile = lambda c: pl.BlockSpec((1, qb, c), lambda i, j: (i, j, 0))
    full = lambda c: pl.BlockSpec((1, t, c), lambda i, j: (i, 0, 0))
    const = lambda a: pl.BlockSpec(a.shape, lambda i, j: (0, 0))
    kern = functools.partial(_dsa_kernel, qb=qb, kt=kt, ka=ka, topk=topk, n_heads=n_heads,
                             idx_bits=max(1, (t - 1).bit_length()))
    return pl.pallas_call(
        kern, grid=(b, t // qb),
        in_specs=[qtile(width), pl.BlockSpec((1, n_heads, qb), lambda i, j: (i, 0, j)), qtile(width),
                  full(LANES), full(rank), pl.BlockSpec((1, t // ka, rank, ka), lambda i, j: (i, 0, 0, 0)),
                  const(wk_t), const(wv)],
        out_specs=qtile(width), out_shape=jax.ShapeDtypeStruct((b, t, width), BF16),
        scratch_shapes=[pltpu.VMEM((t // kt, kt, qb), I32), pltpu.VMEM((n_heads, rank, qb), BF16),
                        pltpu.VMEM((n_heads, rank, qb), F32), pltpu.VMEM((n_heads, 1, qb), F32),
                        pltpu.VMEM((n_heads, 1, qb), F32)],
        compiler_params=_cparams(("parallel", "arbitrary")), name="dsa_attention",
    )(q_idx, w_t, q_b, kn, cn, cn_t, wk_t, wv)


def _even_out_kernel(x_ref, y_ref, bonus_ref, gate_ref, yb_ref, lng_ref, lnb_ref, seg_ref, wa_ref, wb_ref, o_ref):
    seg = seg_ref[...]
    y = y_ref[...]
    yc = y - _seg_dot(y, seg) * (1.0 / HEAD_DIM)
    yn = yc * lax.rsqrt(_seg_dot(yc * yc, seg) * (1.0 / HEAD_DIM) + GN_EPS)
    ya = (yn * lng_ref[...] + lnb_ref[...] + bonus_ref[...]) * gate_ref[...]
    o_ref[...] = x_ref[...] + _dot(ya.astype(BF16), wa_ref[...]) + _dot(yb_ref[...], wb_ref[...])


def even_out(x2d, y, bonus, gate, y_b, ln_g, ln_b, w_out, tm=256):
    n, d = x2d.shape
    aw = y.shape[1]
    tile = lambda c: pl.BlockSpec((tm, c), lambda i: (i, 0))
    const = lambda a: pl.BlockSpec(a.shape, lambda i: (0, 0))
    consts = [ln_g.reshape(1, aw), ln_b.reshape(1, aw), _seg_ones(aw),
              w_out[:aw].astype(BF16), w_out[aw:].astype(BF16)]
    return pl.pallas_call(
        _even_out_kernel, grid=(n // tm,),
        in_specs=[tile(d), tile(aw), tile(aw), tile(aw), tile(y_b.shape[1])] + [const(a) for a in consts],
        out_specs=tile(d), out_shape=jax.ShapeDtypeStruct((n, d), F32),
        compiler_params=_cparams(("parallel",)), name="even_out",
    )(x2d, y, bonus, gate, y_b, *consts)


SB_PAIRS = 4


def _stickbreak_kernel(q_ref, k_ref, vt_ref, o_ref, acc_ref, carry_ref, *, tq, n_pairs):
    qi = pl.program_id(2)
    first_sub = lax.broadcasted_iota(I32, (PAIR, 1), 0) < HEAD_DIM
    key_i = lax.broadcasted_iota(I32, (tq, tq), 0)
    qry_i = lax.broadcasted_iota(I32, (tq, tq), 1)
    before = key_i < qry_i
    later = (qry_i > key_i).astype(BF16)
    heads = [(p, x) for p in range(n_pairs) for x in range(2)]
    q_h = {}
    for p in range(n_pairs):
        q_t = q_ref[0, :, p * PAIR:(p + 1) * PAIR].astype(F32).T * HEAD_DIM ** -0.5
        q_h[p, 0] = jnp.where(first_sub, q_t, 0.0).astype(BF16)
        q_h[p, 1] = jnp.where(first_sub, 0.0, q_t).astype(BF16)
    acc_ref[...] = jnp.zeros_like(acc_ref)
    carry_ref[...] = jnp.zeros_like(carry_ref)

    def tile(j, diag):
        off = pl.multiple_of(j * tq, tq)
        k2 = [k_ref[0, pl.ds(off, tq), p * PAIR:(p + 1) * PAIR] for p in range(n_pairs)]
        z = {h: _dot(k2[h[0]], q_h[h]) * math.log2(math.e) for h in heads}
        lk = {h: -(jnp.maximum(z[h], 0.0) + jnp.log2(1.0 + jnp.exp2(-jnp.abs(z[h])))) for h in heads}
        if diag:
            lk = {h: jnp.where(before, lk[h], 0.0) for h in heads}
        hi = {h: lk[h].astype(BF16) for h in heads}
        lo = {h: (lk[h] - hi[h].astype(F32)).astype(BF16) for h in heads}
        tail = {h: _dot(later, hi[h]) + _dot(later, lo[h]) + carry_ref[h[0], h[1]] for h in heads}
        w = {h: jnp.exp2(z[h] + lk[h] + tail[h]) for h in heads}
        if diag:
            w = {h: jnp.where(before, w[h], 0.0) for h in heads}
        for p in range(n_pairs):
            v_t = vt_ref[0, p, j]
            acc_ref[p] += (_dot(jnp.where(first_sub, v_t, 0.0), w[p, 0].astype(BF16))
                           + _dot(jnp.where(first_sub, 0.0, v_t), w[p, 1].astype(BF16)))
        for h in heads:
            carry_ref[h[0], h[1]] += jnp.sum(lk[h], axis=0, keepdims=True)
        return (jnp.max(carry_ref[...]) > -110.0 * math.log2(math.e)).astype(I32)

    live = tile(qi, True)

    def cond(st):
        return (st[0] >= 0) & (st[1] > 0)

    def body(st):
        return st[0] - 1, tile(st[0], False)

    lax.while_loop(cond, body, (qi - 1, live))
    for p in range(n_pairs):
        o_ref[0, :, p * PAIR:(p + 1) * PAIR] = acc_ref[p].T.astype(o_ref.dtype)


def _qkv_proj_kernel(x_ref, g_ref, wqk_ref, wv_ref, qk_ref, vt_ref):
    h = _rms(x_ref[...], g_ref[...]).astype(BF16)
    qk_ref[...] = _dot(h, wqk_ref[...]).astype(qk_ref.dtype)
    v = _dot(h, wv_ref[...])
    for p in range(v.shape[1] // PAIR):
        vt_ref[0, p, 0] = v[:, p * PAIR:(p + 1) * PAIR].T.astype(vt_ref.dtype)


def qkv_proj(x, g, w_qkv, tq):
    b, t, d = x.shape
    width = w_qkv.shape[1] // 3
    n_pair = width // PAIR
    nq = t // tq
    wqk, wv = w_qkv[:, :2 * width].astype(BF16), w_qkv[:, 2 * width:].astype(BF16)
    return pl.pallas_call(
        _qkv_proj_kernel, grid=(b * nq,),
        in_specs=[pl.BlockSpec((tq, d), lambda i: (i, 0)), pl.BlockSpec((1, d), lambda i: (0, 0)),
                  pl.BlockSpec(wqk.shape, lambda i: (0, 0)), pl.BlockSpec(wv.shape, lambda i: (0, 0))],
        out_specs=[pl.BlockSpec((tq, 2 * width), lambda i: (i, 0)),
                   pl.BlockSpec((1, n_pair, 1, PAIR, tq), lambda i: (i // nq, 0, i % nq, 0, 0))],
        out_shape=[jax.ShapeDtypeStruct((b * t, 2 * width), BF16),
                   jax.ShapeDtypeStruct((b, n_pair, nq, PAIR, tq), BF16)],
        compiler_params=_cparams(("parallel",)), name="qkv_proj",
    )(x.reshape(b * t, d), g.reshape(1, d), wqk, wv)


def stickbreak_attention(qk, v_t, n_heads, tq):
    b, t, _ = qk.shape
    n_pair = n_heads // 2
    width = n_heads * HEAD_DIM
    gw = SB_PAIRS * PAIR
    n_grp = n_pair // SB_PAIRS
    qkv = qk
    q_spec = pl.BlockSpec((1, tq, gw), lambda i, p, j: (i, j, p))
    k_spec = pl.BlockSpec((1, t, gw), lambda i, p, j: (i, 0, n_grp + p))
    v_spec = pl.BlockSpec((1, SB_PAIRS, t // tq, PAIR, tq), lambda i, p, j: (i, p, 0, 0, 0))
    return pl.pallas_call(
        functools.partial(_stickbreak_kernel, tq=tq, n_pairs=SB_PAIRS), grid=(b, n_grp, t // tq),
        in_specs=[q_spec, k_spec, v_spec], out_specs=q_spec,
        out_shape=jax.ShapeDtypeStruct((b, t, width), BF16),
        scratch_shapes=[pltpu.VMEM((SB_PAIRS, PAIR, tq), F32), pltpu.VMEM((SB_PAIRS, 2, 1, tq), F32)],
        compiler_params=_cparams(("parallel", "parallel", "arbitrary")), name="stickbreak",
    )(qkv, qkv, v_t)


def _proj_residual_kernel(x_ref, y_ref, w_ref, o_ref):
    o_ref[...] = x_ref[...] + _dot(y_ref[...], w_ref[...])


def proj_residual(x2d, y, w, tm=256):
    n, d = x2d.shape
    c = y.shape[1]
    return pl.pallas_call(
        _proj_residual_kernel, grid=(n // tm,),
        in_specs=[pl.BlockSpec((tm, d), lambda i: (i, 0)), pl.BlockSpec((tm, c), lambda i: (i, 0)),
                  pl.BlockSpec(w.shape, lambda i: (0, 0))],
        out_specs=pl.BlockSpec((tm, d), lambda i: (i, 0)), out_shape=jax.ShapeDtypeStruct((n, d), F32),
        compiler_params=_cparams(("parallel",)), name="proj_residual",
    )(x2d, y, w)


def _router_kernel(x_ref, g_ref, wr_hi_ref, wr_lo_ref, h_ref, ids_ref, gates_ref):
    h = _rms(x_ref[...], g_ref[...])
    h_hi = h.astype(BF16)
    h_ref[...] = h_hi
    h_lo = (h - h_hi.astype(F32)).astype(BF16)
    logits = _dot(h_hi, wr_hi_ref[...]) + _dot(h_hi, wr_lo_ref[...]) + _dot(h_lo, wr_hi_ref[...])
    lane = lax.broadcasted_iota(I32, logits.shape, 1)
    lane_f = lane.astype(F32)

    def top1(vals):
        mx = jnp.max(vals, axis=1, keepdims=True)
        return mx, jnp.min(jnp.where(vals == mx, lane_f, 1e9), axis=1, keepdims=True).astype(I32)

    is_g = lane < N_GROUPS
    g_max, g_sel = top1(jnp.where(is_g, logits, -jnp.inf))
    g_gate = 1.0 / jnp.sum(jnp.where(is_g, jnp.exp(logits - g_max), 0.0), axis=1, keepdims=True)
    e_lane = lane - N_GROUPS
    in_grp = (e_lane >= g_sel * EXPERTS_PER_GROUP) & (e_lane < (g_sel + 1) * EXPERTS_PER_GROUP)
    el = jnp.where(in_grp, logits, -jnp.inf)
    m1, i1 = top1(el)
    m2, i2 = top1(jnp.where(lane == i1, -jnp.inf, el))
    e21 = jnp.exp(m2 - m1)
    p1 = 1.0 / (1.0 + e21)
    ids_ref[...] = jnp.where(lane == 0, i1 - N_GROUPS, jnp.where(lane == 1, i2 - N_GROUPS, 0))
    gates_ref[...] = jnp.where(lane == 0, g_gate * p1, jnp.where(lane == 1, g_gate * (e21 * p1), 0.0))


def moe_router(x2d, g, w_router, tm=256):
    n, d = x2d.shape
    tile = lambda c: pl.BlockSpec((tm, c), lambda i: (i, 0))
    w_spec = pl.BlockSpec(w_router.shape, lambda i: (0, 0))
    w_hi = w_router.astype(BF16)
    w_lo = (w_router - w_hi.astype(F32)).astype(BF16)
    return pl.pallas_call(
        _router_kernel, grid=(n // tm,),
        in_specs=[tile(d), pl.BlockSpec((1, d), lambda i: (0, 0)), w_spec, w_spec],
        out_specs=[tile(d), tile(LANES), tile(LANES)],
        out_shape=[jax.ShapeDtypeStruct((n, d), BF16), jax.ShapeDtypeStruct((n, LANES), I32),
                   jax.ShapeDtypeStruct((n, LANES), F32)],
        compiler_params=_cparams(("parallel",)), name="moe_router",
    )(x2d, g.reshape(1, d), w_hi, w_lo)


MOE_TT = 2048
MOE_CH = 128
_PLAN_BLK = 256


def _moe_plan_kernel(ids_ref, idst_ref, rank_ref, dest_ref, meta_ref, *, tt):
    nb = tt // _PLAN_BLK
    row = lax.broadcasted_iota(I32, (_PLAN_BLK, _PLAN_BLK), 0)
    col = lax.broadcasted_iota(I32, (_PLAN_BLK, _PLAN_BLK), 1)
    earlier_t = (row < col).astype(BF16)
    earlier = (col < row).astype(BF16)
    e_sub = lax.broadcasted_iota(I32, (N_EXPERTS, _PLAN_BLK), 0)
    carry_t = jnp.zeros((N_EXPERTS, 1), F32)
    for b in range(nb):
        sl = slice(b * _PLAN_BLK, (b + 1) * _PLAN_BLK)
        member = (e_sub == idst_ref[0:1, sl]) | (e_sub == idst_ref[1:2, sl])
        m_t = jnp.where(member, 1.0, 0.0)
        before = _dot(m_t.astype(BF16), earlier_t) + carry_t
        rank_ref[0, :, sl] = jnp.where(member, before, -1.0)
        carry_t = carry_t + jnp.sum(m_t, axis=1, keepdims=True)
    lane = lax.broadcasted_iota(I32, (_PLAN_BLK, LANES), 1)
    carry = jnp.zeros((1, LANES), F32)
    prefix = []
    for b in range(nb):
        sl = slice(b * _PLAN_BLK, (b + 1) * _PLAN_BLK)
        m = jnp.where((lane == ids_ref[sl, 0:1]) | (lane == ids_ref[sl, 1:2]), 1.0, 0.0)
        prefix.append(_dot(earlier, m.astype(BF16)) + carry)
        carry = carry + jnp.sum(m, axis=0, keepdims=True)
    counts = carry
    padded = jnp.ceil(counts * (1.0 / MOE_CH)) * MOE_CH
    l_row = lax.broadcasted_iota(I32, (LANES, LANES), 0)
    l_col = lax.broadcasted_iota(I32, (LANES, LANES), 1)
    offs = _dot_hi(padded, (l_row < l_col).astype(F32))
    for b in range(nb):
        sl = slice(b * _PLAN_BLK, (b + 1) * _PLAN_BLK)
        where_row = prefix[b] + offs
        d0 = jnp.sum(jnp.where(lane == ids_ref[sl, 0:1], where_row, 0.0), axis=1, keepdims=True)
        d1 = jnp.sum(jnp.where(lane == ids_ref[sl, 1:2], where_row, 0.0), axis=1, keepdims=True)
        dest_ref[sl, :] = jnp.where(lane == 0, d0, jnp.where(lane == 1, d1, 0.0)).astype(I32)
    r_used = jnp.sum(padded, axis=1, keepdims=True)
    sub8 = lax.broadcasted_iota(I32, (8, LANES), 0)
    meta_ref[0] = jnp.where(sub8 == 0, counts, jnp.where(sub8 == 1, offs, jnp.where(sub8 == 2, r_used, 0.0))).astype(I32)


def moe_plan(ids, ids_t, tt):
    n = ids.shape[0]
    n_tiles = n // tt
    return pl.pallas_call(
        functools.partial(_moe_plan_kernel, tt=tt), grid=(n_tiles,),
        in_specs=[pl.BlockSpec((tt, LANES), lambda i: (i, 0)), pl.BlockSpec((8, tt), lambda i: (0, i))],
        out_specs=[pl.BlockSpec((1, N_EXPERTS, tt), lambda i: (i, 0, 0)), pl.BlockSpec((tt, LANES), lambda i: (i, 0)),
                   pl.BlockSpec((1, 8, LANES), lambda i: (i, 0, 0))],
        out_shape=[jax.ShapeDtypeStruct((n_tiles, N_EXPERTS, tt), F32), jax.ShapeDtypeStruct((n, LANES), I32),
                   jax.ShapeDtypeStruct((n_tiles, 8, LANES), I32)],
        compiler_params=_cparams(("parallel",)), name="moe_plan",
    )(ids, ids_t)


def _pack_bf16_halves(y):
    half = y.shape[1] // 2
    lo = pltpu.bitcast(y[:, :half].astype(BF16).astype(F32), I32)
    hi = pltpu.bitcast(y[:, half:].astype(BF16).astype(F32), I32)
    return lax.shift_right_logical(lo, 16) | hi


def _unpack_bf16_halves(w):
    return pltpu.bitcast(w << 16, F32), pltpu.bitcast(w & jnp.int32(-65536), F32)


def _moe_ffn_kernel(cnt_ref, off_ref, h_ref, rank_ref, idst_ref, gt_ref, w1_ref, w3_ref, w2_ref, y_ref, *, tt):
    t, e = pl.program_id(0), pl.program_id(1)

    @pl.when(e == 0)
    def _():
        y_ref[...] = jnp.zeros_like(y_ref)

    cnt = cnt_ref[t * N_EXPERTS + e]
    off = off_ref[t * N_EXPERTS + e]
    rank_row = rank_ref[0, pl.ds(e, 1), :]
    gate_row = jnp.where(idst_ref[0:1, :] == e, gt_ref[0:1, :], gt_ref[1:2, :])
    sub = lax.broadcasted_iota(I32, (MOE_CH, 1), 0)

    def chunk(c, carry):
        r0 = c * MOE_CH
        pick = rank_row == (r0 + sub).astype(F32)
        x = _dot(jnp.where(pick, 1.0, 0.0).astype(BF16), h_ref[...]).astype(BF16)
        a = _dot(x, w1_ref[0, 0])
        hdn = (a * jax.nn.sigmoid(a)) * _dot(x, w3_ref[0, 0])
        y = _dot(hdn.astype(BF16), w2_ref[0, 0])
        gate = jnp.sum(jnp.where(pick, gate_row, 0.0), axis=1, keepdims=True)
        y_ref[0, pl.ds(pl.multiple_of(off + r0, MOE_CH), MOE_CH), :] = _pack_bf16_halves(y * gate)
        return carry

    lax.fori_loop(0, (cnt + MOE_CH - 1) // MOE_CH, chunk, 0)


def moe_ffn(h, rank_t, ids_t, gates_t, counts, offs, w1, w3, w2, layer, tt):
    n, d = h.shape
    ff = w1.shape[-1]
    n_tiles = n // tt
    r_max = TOP_K_EXPERTS * tt + N_EXPERTS * MOE_CH
    grid_spec = pltpu.PrefetchScalarGridSpec(
        num_scalar_prefetch=2, grid=(n_tiles, N_EXPERTS),
        in_specs=[pl.BlockSpec((tt, d), lambda i, e, c, o: (i, 0)),
                  pl.BlockSpec((1, N_EXPERTS, tt), lambda i, e, c, o: (i, 0, 0)),
                  pl.BlockSpec((8, tt), lambda i, e, c, o: (0, i)),
                  pl.BlockSpec((8, tt), lambda i, e, c, o: (0, i)),
                  pl.BlockSpec((1, 1, d, ff), lambda i, e, c, o: (layer, e, 0, 0)),
                  pl.BlockSpec((1, 1, d, ff), lambda i, e, c, o: (layer, e, 0, 0)),
                  pl.BlockSpec((1, 1, ff, d), lambda i, e, c, o: (layer, e, 0, 0))],
        out_specs=pl.BlockSpec((1, r_max, d // 2), lambda i, e, c, o: (i, 0, 0)))
    return pl.pallas_call(
        functools.partial(_moe_ffn_kernel, tt=tt), grid_spec=grid_spec,
        out_shape=jax.ShapeDtypeStruct((n_tiles, r_max, d // 2), I32),
        compiler_params=_cparams(("arbitrary", "arbitrary")), name="moe_ffn",
    )(counts, offs, h, rank_t, ids_t, gates_t, w1, w3, w2)


def _moe_combine_kernel(dest_ref, x_ref, y_ref, o_ref, ga_ref, gb_ref, *, tm, per):
    base = (pl.program_id(0) * per + pl.program_id(1)) * tm

    def fetch(i, carry):
        tok = (base + i) * TOP_K_EXPERTS
        ga_ref[pl.ds(i, 1), :] = y_ref[0, pl.ds(dest_ref[tok], 1), :]
        gb_ref[pl.ds(i, 1), :] = y_ref[0, pl.ds(dest_ref[tok + 1], 1), :]
        return carry

    lax.fori_loop(0, tm, fetch, 0, unroll=8)
    half = x_ref.shape[1] // 2
    lo_a, hi_a = _unpack_bf16_halves(ga_ref[...])
    lo_b, hi_b = _unpack_bf16_halves(gb_ref[...])
    o_ref[:, :half] = x_ref[:, :half] + (lo_a + lo_b)
    o_ref[:, half:] = x_ref[:, half:] + (hi_a + hi_b)


def moe_combine(x2d, dest_flat, y_all, tt, tm=256):
    n, d = x2d.shape
    r_max = y_all.shape[1]
    per = tt // tm
    grid_spec = pltpu.PrefetchScalarGridSpec(
        num_scalar_prefetch=1, grid=(n // tt, per),
        in_specs=[pl.BlockSpec((tm, d), lambda i, j, u: (i * per + j, 0)),
                  pl.BlockSpec((1, r_max, d // 2), lambda i, j, u: (i, 0, 0))],
        out_specs=pl.BlockSpec((tm, d), lambda i, j, u: (i * per + j, 0)),
        scratch_shapes=[pltpu.VMEM((tm, d // 2), I32), pltpu.VMEM((tm, d // 2), I32)])
    return pl.pallas_call(
        functools.partial(_moe_combine_kernel, tm=tm, per=per), grid_spec=grid_spec,
        out_shape=jax.ShapeDtypeStruct((n, d), F32),
        compiler_params=_cparams(("arbitrary", "arbitrary")), name="moe_combine",
    )(dest_flat, x2d, y_all)


def _final_norm_kernel(x_ref, g_ref, o_ref):
    o_ref[...] = _rms(x_ref[...], g_ref[...])


def final_norm(x2d, g, tm=512):
    n, d = x2d.shape
    return pl.pallas_call(
        _final_norm_kernel, grid=(n // tm,),
        in_specs=[pl.BlockSpec((tm, d), lambda i: (i, 0)), pl.BlockSpec((1, d), lambda i: (0, 0))],
        out_specs=pl.BlockSpec((tm, d), lambda i: (i, 0)), out_shape=jax.ShapeDtypeStruct((n, d), F32),
        compiler_params=_cparams(("parallel",)), name="final_norm",
    )(x2d, g.reshape(1, d))


def moe_layer(x2d, norm_g, router_group, router_expert, w1, w3, w2, layer):
    n, d = x2d.shape
    tt = min(MOE_TT, n)
    w_router = jnp.concatenate([router_group, jnp.moveaxis(router_expert, 0, 1).reshape(d, N_EXPERTS)], axis=1)
    h, ids, gates = moe_router(x2d, norm_g, _pad_cols(w_router, LANES))
    ids_t = jnp.pad(ids[:, :TOP_K_EXPERTS].T, ((0, 8 - TOP_K_EXPERTS), (0, 0)))
    gates_t = jnp.pad(gates[:, :TOP_K_EXPERTS].T, ((0, 8 - TOP_K_EXPERTS), (0, 0)))
    rank_t, dest, meta = moe_plan(ids, ids_t, tt)
    counts = meta[:, 0, :N_EXPERTS].reshape(-1)
    offs = meta[:, 1, :N_EXPERTS].reshape(-1)
    y_all = moe_ffn(h, rank_t, ids_t, gates_t, counts, offs, w1, w3, w2, layer, tt)
    return moe_combine(x2d, dest[:, :TOP_K_EXPERTS].reshape(-1), y_all, tt)


def even_layer(x, v_first, p):
    b, t, d = x.shape
    n = b * t
    w_in = p['w_in']
    aw = p['k_k'].shape[0]
    bw = p['w_kv_up'].shape[1] // 2
    rank = p['w_kv_up'].shape[0]
    n_idx = w_in.shape[1] - (3 * aw + bw + rank + bw + HEAD_DIM)
    o = 0
    segs = []
    for width in (3 * aw, bw, rank, bw, HEAD_DIM, n_idx):
        segs.append(w_in[:, o:o + width])
        o += width
    segs[4] = jnp.concatenate([segs[4], segs[4]], axis=1)
    segs[5] = _pad_cols(segs[5], LANES)
    rkv, q_b, c_kv, q_idx, k_dup, w_idx = norm_proj(
        x.reshape(n, d), p['norm'], [s.astype(BF16) for s in segs], [F32] * 6)

    r, lw, k_mod, v, kk, iclr, gate, bonus = rwkv_prep(x, rkv.reshape(b, t, 3 * aw), v_first, p)
    y = rwkv_scan(r, lw, k_mod, v, kk, iclr)

    cn, kn = kv_prep(c_kv, k_dup, p['kv_norm'], jnp.concatenate([p['idx_k_norm']] * 2))
    wk_t = p['w_kv_up'][:, :bw].astype(BF16)
    wv = p['w_kv_up'][:, bw:].astype(BF16)
    y_b = dsa_attention(q_idx.reshape(b, t, bw), w_idx.reshape(b, t, LANES), q_b.reshape(b, t, bw),
                        kn.reshape(b, t, LANES), cn.reshape(b, t, rank), wk_t, wv)

    x_new = even_out(x.reshape(n, d), y.reshape(n, aw), bonus.reshape(n, aw), gate.reshape(n, aw),
                     y_b.reshape(n, bw), p['ln_g'], p['ln_b'], p['w_out'])
    return x_new.reshape(b, t, d), v


def odd_layer(x, norm_g, w_qkv, w_out):
    b, t, d = x.shape
    n = b * t
    cw = w_qkv.shape[1] // 3
    tq = min(256, t)
    qk, v_t = qkv_proj(x, norm_g, w_qkv, tq)
    y_c = stickbreak_attention(qk.reshape(b, t, 2 * cw), v_t, cw // HEAD_DIM, tq)
    return proj_residual(x.reshape(n, d), y_c.reshape(n, cw), w_out.astype(BF16)).reshape(b, t, d)


@jax.jit
def _forward(x, ev_norm, ev_w_in, ev_w_out, a_mu_rkv, a_mu_lora, a_w0, a_w1, a_w2, a_a0, a_a1, a_a2, a_g1,
             a_g2, a_k_k, a_k_a, a_r_k, a_ln_g, a_ln_b, a_mu_vres, a_v0, a_v1, a_v2, b_kv_norm, b_w_kv_up,
             b_idx_k_norm, od_norm, od_w_qkv, od_w_out, moe_norm, moe_router_group, moe_router_expert,
             moe_w1, moe_w3, moe_w2, final_norm_g):
    b, t, d = x.shape
    depth = moe_norm.shape[0]
    w1_bf16, w3_bf16, w2_bf16 = moe_w1.astype(BF16), moe_w3.astype(BF16), moe_w2.astype(BF16)
    v_first = None
    for i in range(depth):
        if i % 2 == 0:
            e = i // 2
            p = dict(norm=ev_norm[e], w_in=ev_w_in[e], w_out=ev_w_out[e], mu_rkv=a_mu_rkv[e], mu_lora=a_mu_lora[e],
                     w0=a_w0[e], w1=a_w1[e], w2=a_w2[e], a0=a_a0[e], a1=a_a1[e], a2=a_a2[e], g1=a_g1[e], g2=a_g2[e],
                     k_k=a_k_k[e], k_a=a_k_a[e], r_k=a_r_k[e].reshape(-1), ln_g=a_ln_g[e], ln_b=a_ln_b[e],
                     kv_norm=b_kv_norm[e], w_kv_up=b_w_kv_up[e], idx_k_norm=b_idx_k_norm[e])
            if e > 0:
                p.update(mu_v=a_mu_vres[e - 1], v0=a_v0[e - 1], v1=a_v1[e - 1], v2=a_v2[e - 1])
            x, v_used = even_layer(x, v_first if e > 0 else None, p)
            if e == 0:
                v_first = v_used
        else:
            o = i // 2
            x = odd_layer(x, od_norm[o], od_w_qkv[o], od_w_out[o])
        x = moe_layer(x.reshape(b * t, d), moe_norm[i], moe_router_group[i], moe_router_expert[i],
                      w1_bf16, w3_bf16, w2_bf16, i).reshape(b, t, d)
    return final_norm(x.reshape(b * t, d), final_norm_g).reshape(b, t, d)


def kernel(x, ev_norm, ev_w_in, ev_w_out, a_mu_rkv, a_mu_lora, a_w0, a_w1, a_w2, a_a0, a_a1, a_a2, a_g1, a_g2, a_k_k, a_k_a, a_r_k, a_ln_g, a_ln_b, a_mu_vres, a_v0, a_v1, a_v2, b_kv_norm, b_w_kv_up, b_idx_k_norm, od_norm, od_w_qkv, od_w_out, moe_norm, moe_router_group, moe_router_expert, moe_w1, moe_w3, moe_w2, final_norm):
    return _forward(x, ev_norm, ev_w_in, ev_w_out, a_mu_rkv, a_mu_lora, a_w0, a_w1, a_w2, a_a0, a_a1, a_a2, a_g1,
                    a_g2, a_k_k, a_k_a, a_r_k, a_ln_g, a_ln_b, a_mu_vres, a_v0, a_v1, a_v2, b_kv_norm, b_w_kv_up,
                    b_idx_k_norm, od_norm, od_w_qkv, od_w_out, moe_norm, moe_router_group, moe_router_expert,
                    moe_w1, moe_w3, moe_w2, final_norm)
```

```python
import functools
import math

import jax
import jax.numpy as jnp
from jax import lax
from jax.experimental import pallas as pl
from jax.experimental.pallas import tpu as pltpu

F32 = jnp.float32
BF16 = jnp.bfloat16
I32 = jnp.int32
HI = lax.Precision.HIGHEST

LANES = 128
HEAD_DIM = 64
PAIR = 2 * HEAD_DIM
RMS_EPS = 1e-6
GN_EPS = 64e-5
N_GROUPS = 4
EXPERTS_PER_GROUP = 8
N_EXPERTS = N_GROUPS * EXPERTS_PER_GROUP
TOP_K_EXPERTS = 2
IDX_TOPK_MAX = 256
INT_MIN = -2147483648
NEG_BIG = -1e30
VMEM_LIMIT = 56 * 1024 * 1024

_NT = (((1,), (1,)), ((), ()))
_TN = (((0,), (0,)), ((), ()))


def _cparams(sem):
    return pltpu.CompilerParams(dimension_semantics=sem, vmem_limit_bytes=VMEM_LIMIT)


def _rms(x, g):
    return x * lax.rsqrt(jnp.mean(x * x, axis=-1, keepdims=True) + RMS_EPS) * g


def _dot(a, b):
    return jnp.dot(a, b, preferred_element_type=F32)


def _dot_hi(a, b):
    return jnp.dot(a, b, preferred_element_type=F32, precision=HI)


def _seg_dot(x, seg):
    hi = x.astype(BF16)
    return _dot(hi, seg) + _dot((x - hi.astype(F32)).astype(BF16), seg)


def _dot_nt(a, b):
    return lax.dot_general(a, b, _NT, preferred_element_type=F32)


def _lane_is_first_head(width=PAIR):
    return lax.broadcasted_iota(I32, (1, width), 1) % PAIR < HEAD_DIM


def _norm_proj_kernel(x_ref, g_ref, *refs, n_seg):
    w_refs, o_refs = refs[:n_seg], refs[n_seg:]
    h = _rms(x_ref[...], g_ref[...]).astype(BF16)
    for w_ref, o_ref in zip(w_refs, o_refs):
        o_ref[...] = _dot(h, w_ref[...]).astype(o_ref.dtype)


def norm_proj(x2d, g, ws, out_dtypes, tm=256):
    n, d = x2d.shape
    in_specs = [pl.BlockSpec((tm, d), lambda i: (i, 0)), pl.BlockSpec((1, d), lambda i: (0, 0))]
    in_specs += [pl.BlockSpec(w.shape, lambda i: (0, 0)) for w in ws]
    out_specs = [pl.BlockSpec((tm, w.shape[1]), lambda i: (i, 0)) for w in ws]
    out_shape = [jax.ShapeDtypeStruct((n, w.shape[1]), dt) for w, dt in zip(ws, out_dtypes)]
    return pl.pallas_call(
        functools.partial(_norm_proj_kernel, n_seg=len(ws)),
        grid=(n // tm,), in_specs=in_specs, out_specs=out_specs, out_shape=out_shape,
        compiler_params=_cparams(("parallel",)), name="norm_proj",
    )(x2d, g.reshape(1, d), *ws)


def _shift_rows(z, carry_ref):
    tm = z.shape[0]
    first = lax.broadcasted_iota(I32, (tm, 1), 0) == 0
    prev = jnp.where(first, carry_ref[...], pltpu.roll(z, 1, 0))
    carry_ref[...] = z[tm - 1:tm, :]
    return prev


def _rwkv_prep_kernel(*refs, has_vres):
    it = iter(refs)
    x_ref, rkv_ref = next(it), next(it)
    vfirst_ref = next(it) if has_vres else None
    g_ref, mu_rkv_ref, mu_lora_ref = next(it), next(it), next(it)
    w0_ref, w1_ref, w2_ref = next(it), next(it), next(it)
    a0_ref, a1_ref, a2_ref = next(it), next(it), next(it)
    g1_ref, g2_ref = next(it), next(it)
    if has_vres:
        muv_ref, v0_ref, v1_ref, v2_ref = next(it), next(it), next(it), next(it)
    kk_w_ref, ka_ref, rk_ref, seg_ref = next(it), next(it), next(it), next(it)
    r_o, lw_o, k_o, v_o, kk_o, a_o, gate_o, bonus_o = (next(it) for _ in range(8))
    hcarry, rkvcarry = next(it), next(it)

    @pl.when(pl.program_id(1) == 0)
    def _():
        hcarry[...] = jnp.zeros_like(hcarry)
        rkvcarry[...] = jnp.zeros_like(rkvcarry)

    aw = r_o.shape[-1]
    h = _rms(x_ref[0], g_ref[...])
    dh = _shift_rows(h, hcarry) - h
    rkv = rkv_ref[0]
    rkv = rkv + (_shift_rows(rkv, rkvcarry) - rkv) * mu_rkv_ref[...]
    r, k, v = rkv[:, :aw], rkv[:, aw:2 * aw], rkv[:, 2 * aw:]

    def lora_in(row):
        return (h + dh * mu_lora_ref[row:row + 1, :]).astype(BF16)

    dec = w0_ref[...] + _dot(jnp.tanh(_dot(lora_in(0), w1_ref[...])).astype(BF16), w2_ref[...])
    logw = -math.exp(-0.5) * jax.nn.sigmoid(dec)
    iclr = jax.nn.sigmoid(a0_ref[...] + _dot(_dot(lora_in(1), a1_ref[...]).astype(BF16), a2_ref[...]))
    gate = _dot(jax.nn.sigmoid(_dot(lora_in(2), g1_ref[...])).astype(BF16), g2_ref[...])
    if has_vres:
        xv = (h + dh * muv_ref[...]).astype(BF16)
        mix = jax.nn.sigmoid(v0_ref[...] + _dot(_dot(xv, v1_ref[...]).astype(BF16), v2_ref[...]))
        v = v + (vfirst_ref[0] - v) * mix

    seg = seg_ref[...]
    kk = k * kk_w_ref[...]
    kk = kk * lax.rsqrt(jnp.maximum(_seg_dot(kk * kk, seg), 1e-12))
    k_mod = k * (1.0 + (iclr - 1.0) * ka_ref[...])
    bonus = _seg_dot(r * k_mod * rk_ref[...], seg) * v

    r_o[0], lw_o[0], k_o[0], v_o[0] = r, logw, k_mod, v
    kk_o[0], a_o[0], gate_o[0], bonus_o[0] = kk, iclr, gate, bonus


def _pad_cols(w, n):
    return jnp.pad(w, ((0, 0), (0, n - w.shape[1])))


def _pad_rows(w, n):
    return jnp.pad(w, ((0, n - w.shape[0]), (0, 0)))


def _seg_ones(width):
    i = jnp.arange(width) // HEAD_DIM
    return (i[:, None] == i[None, :]).astype(BF16)


def rwkv_prep(x, rkv, v_first, p, tm=256):
    b, t, d = x.shape
    aw = rkv.shape[-1] // 3
    has_vres = v_first is not None
    row = lambda a: a.reshape(1, -1)
    const = lambda a: pl.BlockSpec(a.shape, lambda i, j: (0,) * a.ndim)
    tile = lambda c: pl.BlockSpec((1, tm, c), lambda i, j: (i, j, 0))

    args, specs = [x, rkv], [tile(d), tile(3 * aw)]
    if has_vres:
        args.append(v_first)
        specs.append(tile(aw))
    consts = [row(p['norm']), p['mu_rkv'].reshape(1, 3 * aw), p['mu_lora'],
              row(p['w0']), _pad_cols(p['w1'], LANES).astype(BF16), _pad_rows(p['w2'], LANES).astype(BF16),
              row(p['a0']), _pad_cols(p['a1'], LANES).astype(BF16), _pad_rows(p['a2'], LANES).astype(BF16),
              p['g1'].astype(BF16), p['g2'].astype(BF16)]
    if has_vres:
        consts += [row(p['mu_v']), row(p['v0']), _pad_cols(p['v1'], LANES).astype(BF16),
                   _pad_rows(p['v2'], LANES).astype(BF16)]
    consts += [row(p['k_k']), row(p['k_a']), row(p['r_k']), _seg_ones(aw)]
    args += consts
    specs += [const(a) for a in consts]
    out_shape = [jax.ShapeDtypeStruct((b, t, aw), F32)] * 8
    return pl.pallas_call(
        functools.partial(_rwkv_prep_kernel, has_vres=has_vres),
        grid=(b, t // tm), in_specs=specs, out_specs=[tile(aw)] * 8, out_shape=out_shape,
        scratch_shapes=[pltpu.VMEM((1, d), F32), pltpu.VMEM((1, 3 * aw), F32)],
        compiler_params=_cparams(("arbitrary", "arbitrary")), name="rwkv_prep",
    )(*args)


def _split_bf16(x):
    hi = x.astype(BF16)
    return hi, (x - hi.astype(F32)).astype(BF16)


def _mm3(a, b, dims=None):
    a_hi, a_lo = _split_bf16(a)
    b_hi, b_lo = _split_bf16(b)
    if dims is None:
        f = lambda p, q: jnp.dot(p, q, preferred_element_type=F32)
    else:
        f = lambda p, q: lax.dot_general(p, q, dims, preferred_element_type=F32)
    return f(a_hi, b_hi) + f(a_hi, b_lo) + f(a_lo, b_hi)


def _mmc(a, b, dims=None):
    a, b = a.astype(BF16), b.astype(BF16)
    if dims is None:
        return jnp.dot(a, b, preferred_element_type=F32)
    return lax.dot_general(a, b, dims, preferred_element_type=F32)


def _cumsum_rows(x):
    n = x.shape[0]
    row = lax.broadcasted_iota(I32, (n, 1), 0)
    s = 1
    while s < n:
        x = x + jnp.where(row >= s, pltpu.roll(x, s, 0), 0.0)
        s *= 2
    return x


def _rwkv_chunk_kernel(r_ref, lw_ref, k_ref, v_ref, kk_ref, a_ref, qh_ref, y0_ref, m_ref, c0_ref, *, chunk, cps):
    L = chunk
    n_pair = r_ref.shape[-1] // PAIR
    row = lax.broadcasted_iota(I32, (L, L), 0)
    col = lax.broadcasted_iota(I32, (L, L), 1)
    strict = col < row
    incl = col <= row
    eye = (col == row).astype(F32)
    first = _lane_is_first_head()
    p_row = lax.broadcasted_iota(I32, (PAIR, PAIR), 0)
    p_col = lax.broadcasted_iota(I32, (PAIR, PAIR), 1)
    blockdiag = p_row // HEAD_DIM == p_col // HEAD_DIM
    pair_eye = p_row == p_col

    units = [(ci, p) for ci in range(cps) for p in range(n_pair)]
    heads = (first, ~first)
    pre = {}
    for u in units:
        ci, p = u
        rows = slice(ci * L, (ci + 1) * L)
        sl = slice(p * PAIR, (p + 1) * PAIR)
        lw = lw_ref[0, rows, sl]
        c = _cumsum_rows(lw)
        g = jnp.exp(c)
        g_inv = jnp.exp(-c)
        kk = kk_ref[0, rows, sl]
        pre[u] = dict(rows=rows, sl=sl, g_last=g[L - 1:L, :],
                      at=-kk * jnp.exp(c - lw),
                      bt=kk * a_ref[0, rows, sl] * g_inv,
                      kt=k_ref[0, rows, sl] * g_inv,
                      rt=r_ref[0, rows, sl] * g,
                      vv=v_ref[0, rows, sl])
    uh = [(u, h) for u in units for h in range(2)]

    gm = {}
    for u, h in uh:
        d = pre[u]
        lhs = jnp.concatenate([jnp.where(heads[h], d['at'], 0.0), jnp.where(heads[h], d['rt'], 0.0)], axis=0)
        gm[u, h] = _mm3(lhs, jnp.concatenate([d['bt'], d['kt']], axis=0), _NT)
    a_ak = {x: jnp.where(strict, gm[x][:L, L:], 0.0) for x in uh}
    a_rb = {x: jnp.where(incl, gm[x][L:, :L], 0.0) for x in uh}
    a_rk = {x: jnp.where(incl, gm[x][L:, L:], 0.0) for x in uh}
    apow = {x: jnp.where(strict, gm[x][:L, :L], 0.0) for x in uh}
    inv = {x: eye + apow[x] for x in uh}
    for _ in range(max(1, (L - 1).bit_length()) - 1):
        apow = {x: _mmc(apow[x], apow[x]) for x in uh}
        inv = {x: inv[x] + _mmc(inv[x], apow[x]) for x in uh}
    w_h = {(u, h): _mmc(inv[u, h], pre[u]['at']) for u, h in uh}
    akv = {(u, h): _mmc(a_ak[u, h], pre[u]['vv']) for u, h in uh}
    u0_h = {x: _mmc(inv[x], akv[x]) for x in uh}
    w = {u: jnp.where(first, w_h[u, 0], w_h[u, 1]) for u in units}
    u0 = {u: jnp.where(first, u0_h[u, 0], u0_h[u, 1]) for u in units}
    wu = {u: jnp.concatenate([w[u], u0[u]], axis=1) for u in units}
    arb = {(u, h): _mmc(a_rb[u, h], wu[u]) for u, h in uh}
    ark = {(u, h): _mmc(a_rk[u, h], pre[u]['vv']) for u, h in uh}
    first2 = jnp.concatenate([first, first], axis=1)
    for u in units:
        d = pre[u]
        ci, p = u
        arb_w = jnp.where(first2, arb[u, 0], arb[u, 1])
        qh_ref[0, d['rows'], d['sl']] = d['rt'] + arb_w[:, :PAIR]
        y0_ref[0, d['rows'], d['sl']] = arb_w[:, PAIR:] + jnp.where(first, ark[u, 0], ark[u, 1])
        bg = d['bt'] * d['g_last']
        m_kk = _mmc(bg, w[u], _TN)
        m_ref[0, ci, p] = jnp.where(pair_eye, d['g_last'], 0.0) + jnp.where(blockdiag, m_kk, 0.0)
        c0 = _mmc(jnp.concatenate([u0[u], d['vv']], axis=0),
                  jnp.concatenate([bg, d['kt'] * d['g_last']], axis=0), _TN)
        c0_ref[0, ci, p] = jnp.where(blockdiag, c0, 0.0)


def _rwkv_state_kernel(qh_ref, y0_ref, m_ref, c0_ref, y_ref, s_ref, *, chunk, cps):
    L = chunk

    @pl.when(pl.program_id(0) == 0)
    def _():
        s_ref[...] = jnp.zeros_like(s_ref)

    n_batch = qh_ref.shape[0]
    n_pair = qh_ref.shape[-1] // PAIR
    streams = [(b, p) for b in range(n_batch) for p in range(n_pair)]
    s2 = {x: s_ref[x[0], x[1]] for x in streams}
    for ci in range(cps):
        rows = slice(ci * L, (ci + 1) * L)
        for b, p in streams:
            sl = slice(p * PAIR, (p + 1) * PAIR)
            y_ref[b, rows, sl] = _mm3(qh_ref[b, rows, sl], s2[b, p], _NT) + y0_ref[b, rows, sl]
            s2[b, p] = _mm3(s2[b, p], m_ref[b, ci, p], _NT) + c0_ref[b, ci, p]
    for b, p in streams:
        s_ref[b, p] = s2[b, p]


def rwkv_scan(r, lw, k, v, kk, a, chunk=64, cps_a=4, cps_b=4):
    b, t, aw = r.shape
    n_pair = aw // PAIR
    n_chunk = t // chunk
    cps_a, cps_b = min(cps_a, n_chunk), min(cps_b, n_chunk)
    rows = lambda c: pl.BlockSpec((1, c * chunk, aw), lambda i, j: (i, j, 0))
    mats = lambda c: pl.BlockSpec((1, c, n_pair, PAIR, PAIR), lambda i, j: (i, j, 0, 0, 0))
    seq = jax.ShapeDtypeStruct((b, t, aw), F32)
    mat = jax.ShapeDtypeStruct((b, n_chunk, n_pair, PAIR, PAIR), F32)
    qh, y0, m, c0 = pl.pallas_call(
        functools.partial(_rwkv_chunk_kernel, chunk=chunk, cps=cps_a),
        grid=(b, n_chunk // cps_a), in_specs=[rows(cps_a)] * 6,
        out_specs=[rows(cps_a), rows(cps_a), mats(cps_a), mats(cps_a)], out_shape=[seq, seq, mat, mat],
        compiler_params=_cparams(("parallel", "parallel")), name="rwkv_chunk",
    )(r, lw, k, v, kk, a)
    rows_b = pl.BlockSpec((b, cps_b * chunk, aw), lambda j: (0, j, 0))
    mats_b = pl.BlockSpec((b, cps_b, n_pair, PAIR, PAIR), lambda j: (0, j, 0, 0, 0))
    return pl.pallas_call(
        functools.partial(_rwkv_state_kernel, chunk=chunk, cps=cps_b),
        grid=(n_chunk // cps_b,), in_specs=[rows_b, rows_b, mats_b, mats_b],
        out_specs=rows_b, out_shape=seq,
        scratch_shapes=[pltpu.VMEM((b, n_pair, PAIR, PAIR), F32)],
        compiler_params=_cparams(("arbitrary",)), name="rwkv_state",
    )(qh, y0, m, c0)


def _kv_prep_kernel(c_ref, k_ref, gc_ref, gk_ref, cn_ref, kn_ref):
    cn_ref[...] = _rms(c_ref[...], gc_ref[...]).astype(cn_ref.dtype)
    kn_ref[...] = _rms(k_ref[...], gk_ref[...]).astype(kn_ref.dtype)


def kv_prep(c_kv, k_dup, g_c, g_k_dup, tm=512):
    n, w = c_kv.shape
    tile = pl.BlockSpec((tm, w), lambda i: (i, 0))
    const = pl.BlockSpec((1, w), lambda i: (0, 0))
    return pl.pallas_call(
        _kv_prep_kernel, grid=(n // tm,), in_specs=[tile, tile, const, const], out_specs=[tile, tile],
        out_shape=[jax.ShapeDtypeStruct((n, w), BF16)] * 2,
        compiler_params=_cparams(("parallel",)), name="kv_prep",
    )(c_kv, k_dup, g_c.reshape(1, w), g_k_dup.reshape(1, w))


SEARCH_VALUE_STEPS = 24
SEARCH_STEPS_PER_CHECK = 2


def _tree_reduce(combine, xs):
    while len(xs) > 1:
        xs = [combine(xs[i], xs[i + 1]) for i in range(0, len(xs) - 1, 2)] + ([xs[-1]] if len(xs) % 2 else [])
    return xs[0]


def _key_to_f32(key):
    return pltpu.bitcast(jnp.where(key >= 0, key, key ^ 0x7FFFFFFF), F32)


def _f32_to_key(x):
    bits = pltpu.bitcast(x, I32)
    return jnp.where(bits >= 0, bits, bits ^ 0x7FFFFFFF)


def _dsa_kernel(qi_ref, wit_ref, q_ref, kn_ref, cn_ref, cnt_ref, wk_ref, wv_ref, o_ref,
                skey_ref, qpt_ref, acc_ref, m_ref, l_ref, *, qb, kt, ka, topk, n_heads, idx_bits):
    blk = pl.program_id(1)
    nt = ((blk + 1) * qb + kt - 1) // kt
    first = _lane_is_first_head()
    first_sub = lax.broadcasted_iota(I32, (PAIR, 1), 0) < HEAD_DIM
    q_pos = blk * qb + lax.broadcasted_iota(I32, (1, qb), 1)
    row0 = lax.broadcasted_iota(I32, (kt, 1), 0)
    heads = range(n_heads)

    qi = qi_ref[0]
    wit = wit_ref[0]
    qi_t = []
    for p in range(n_heads // 2):
        pair_t = qi[:, p * PAIR:(p + 1) * PAIR].T * HEAD_DIM ** -0.5
        qi_t += [jnp.where(first_sub, pair_t, 0.0).astype(BF16), jnp.where(first_sub, 0.0, pair_t).astype(BF16)]

    def score_tile(j, carry):
        off = pl.multiple_of(j * kt, kt)
        kn = kn_ref[0, pl.ds(off, kt), :]
        sc = jnp.zeros((kt, qb), F32)
        for h in heads:
            sc = sc + wit[h:h + 1, :] * jnp.maximum(_dot(kn, qi_t[h]), 0.0)
        sc = sc * n_heads ** -0.5
        key = jnp.where(sc == 0.0, 0, _f32_to_key(sc))
        causal = off + row0 <= q_pos
        skey_ref[j] = jnp.where(causal, key, INT_MIN)
        hi8 = _tree_reduce(jnp.maximum, list(jnp.where(causal, sc, -jnp.inf).reshape(kt // 8, 8, qb)))
        lo8 = _tree_reduce(jnp.minimum, list(jnp.where(causal, sc, jnp.inf).reshape(kt // 8, 8, qb)))
        return jnp.maximum(carry[0], hi8), jnp.minimum(carry[1], lo8)

    hi8, lo8 = lax.fori_loop(0, nt, score_tile,
                             (jnp.full((8, qb), -jnp.inf, F32), jnp.full((8, qb), jnp.inf, F32)))

    def count(pred):
        def hits(j):
            part = jnp.where(pred(skey_ref[j], j * kt + row0), 1.0, 0.0).reshape(kt // 8, 8, qb)
            return _tree_reduce(jnp.add, [part[i] for i in range(kt // 8)])

        def body(i, acc):
            j1 = 2 * i + 1
            second = jnp.where(j1 < nt, 1.0, 0.0)
            return acc + hits(2 * i) + hits(jnp.minimum(j1, nt - 1)) * second

        acc = lax.fori_loop(0, (nt + 1) // 2, body, jnp.zeros((8, qb), F32))
        return jnp.sum(acc, axis=0, keepdims=True)

    few = q_pos + 1 <= topk
    v_max = jnp.max(hi8, axis=0, keepdims=True)
    v_min = jnp.min(lo8, axis=0, keepdims=True)
    lo0 = _f32_to_key(jnp.where(v_min == 0.0, 0.0, v_min))
    hi0 = _f32_to_key(jnp.where(v_max == 0.0, 0.0, v_max)) + 1
    cnt0 = (q_pos + 1).astype(F32)

    def unresolved(lo, hi, cnt_lo):
        return ~(few | (cnt_lo == topk) | (hi <= lo + 1))

    def any_open(lo, hi, cnt_lo):
        return jnp.max(jnp.where(unresolved(lo, hi, cnt_lo), 1.0, 0.0))

    def search_cond(st):
        return (st[0] < SEARCH_VALUE_STEPS + 34) & (st[4] > 0.0)

    def search_step(st):
        it, lo, hi, cnt_lo, _ = st
        for _ in range(SEARCH_STEPS_PER_CHECK):
            mid_v = _f32_to_key(0.5 * _key_to_f32(lo) + 0.5 * _key_to_f32(hi))
            mid_k = (lo >> 1) + (hi >> 1) + (lo & hi & 1)
            mid = jnp.where((it < SEARCH_VALUE_STEPS) & (mid_v > lo) & (mid_v < hi), mid_v, mid_k)
            mid = jnp.where((it == 0) & (lo < 0) & (hi > 0), 0, mid)
            mid = jnp.where((it == 1) & (lo == 0) & (hi > 1), 1, mid)
            cnt = count(lambda sk, _: sk >= mid)
            open_ = unresolved(lo, hi, cnt_lo)
            up = open_ & (cnt >= topk)
            lo, cnt_lo = jnp.where(up, mid, lo), jnp.where(up, cnt, cnt_lo)
            hi = jnp.where(open_ & ~up, mid, hi)
            it = it + 1
        return it, lo, hi, cnt_lo, any_open(lo, hi, cnt_lo)

    st0 = (jnp.int32(0), lo0, hi0, cnt0, any_open(lo0, hi0, cnt0))
    _, lo, _, cnt_lo, _ = lax.while_loop(search_cond, search_step, st0)
    thr = jnp.where(few, INT_MIN, lo)
    tie = (cnt_lo > topk) & ~few
    any_tie = jnp.max(jnp.where(tie, 1, 0))
    n_gt = lax.fori_loop(0, any_tie, lambda i, c: count(lambda sk, _: sk > thr), jnp.zeros((1, qb), F32))
    need = topk - n_gt

    def idx_bit(i, prefix):
        cand = prefix | jnp.left_shift(jnp.int32(1), idx_bits - 1 - i)
        cnt = count(lambda sk, kidx: (sk == thr) & (kidx < cand))
        return jnp.where(cnt < need, cand, prefix)

    jmax = lax.fori_loop(0, idx_bits * any_tie, idx_bit, jnp.zeros((1, qb), I32))
    jmax = jnp.where(tie, jmax, 0x7FFFFFFF)

    q = q_ref[0]
    for h in heads:
        pr = slice((h // 2) * PAIR, (h // 2 + 1) * PAIR)
        wk_h = jnp.where(first if h % 2 == 0 else ~first, wk_ref[:, pr], 0.0)
        qpt_ref[h] = (_dot_nt(wk_h, q[:, pr].astype(BF16)) * (HEAD_DIM ** -0.5 * math.log2(math.e))).astype(BF16)
    m_ref[...] = jnp.full_like(m_ref, NEG_BIG)
    l_ref[...] = jnp.zeros_like(l_ref)
    acc_ref[...] = jnp.zeros_like(acc_ref)

    def attn_tile(j, carry):
        sk_all = skey_ref[j]
        for part in range(kt // ka):
            off = pl.multiple_of(j * kt + part * ka, ka)
            sk = sk_all[part * ka:(part + 1) * ka, :]
            kidx = off + row0[:ka]
            sel = ((sk > thr) | ((sk == thr) & (kidx <= jmax))) & (kidx <= q_pos)
            bias = jnp.where(sel, 0.0, NEG_BIG)
            cn = cn_ref[0, pl.ds(off, ka), :]
            cn_t = cnt_ref[0, j * (kt // ka) + part]
            s = [_dot(cn, qpt_ref[h]) + bias for h in heads]
            m_old = [m_ref[h] for h in heads]
            m_new = [jnp.maximum(m_old[h], jnp.max(s[h], axis=0, keepdims=True)) for h in heads]
            pexp = [jnp.exp2(s[h] - m_new[h]) for h in heads]
            alpha = [jnp.exp2(m_old[h] - m_new[h]) for h in heads]
            pv = [_dot(cn_t, pexp[h].astype(BF16)) for h in heads]
            for h in heads:
                l_ref[h] = alpha[h] * l_ref[h] + jnp.sum(pexp[h], axis=0, keepdims=True)
                acc_ref[h] = alpha[h] * acc_ref[h] + pv[h]
                m_ref[h] = m_new[h]
        return carry

    lax.fori_loop(0, nt, attn_tile, 0)

    rank = cn_ref.shape[-1]
    for p in range(n_heads // 2):
        pr = slice(p * PAIR, (p + 1) * PAIR)
        out = jnp.zeros((qb, PAIR), F32)
        for h in (2 * p, 2 * p + 1):
            lat_t = (acc_ref[h] / l_ref[h]).astype(BF16)
            wv_h = jnp.where(first if h % 2 == 0 else ~first, wv_ref[:, pr], 0.0)
            out = out + lax.dot_general(lat_t, wv_h, _TN, preferred_element_type=F32)
        o_ref[0, :, pr] = out.astype(o_ref.dtype)


def dsa_attention(q_idx, w_idx, q_b, kn, cn, wk_t, wv, qb=128, kt=512, ka=512):
    b, t, width = q_b.shape
    n_heads = width // HEAD_DIM
    rank = cn.shape[-1]
    kt, ka = min(kt, t), min(ka, t)
    topk = min(IDX_TOPK_MAX, t // 4)
    w_t = jnp.swapaxes(w_idx[:, :, :n_heads], 1, 2)
    cn_t = jnp.swapaxes(cn.reshape(b, t // ka, ka, rank), 2, 3)
    qtile = lambda c: pl.BlockSpec((1, qb, c), lambda i, j: (i, j, 0))
    full = lambda c: pl.BlockSpec((1, t, c), lambda i, j: (i, 0, 0))
    const = lambda a: pl.BlockSpec(a.shape, lambda i, j: (0, 0))
    kern = functools.partial(_dsa_kernel, qb=qb, kt=kt, ka=ka, topk=topk, n_heads=n_heads,
                             idx_bits=max(1, (t - 1).bit_length()))
    return pl.pallas_call(
        kern, grid=(b, t // qb),
        in_specs=[qtile(width), pl.BlockSpec((1, n_heads, qb), lambda i, j: (i, 0, j)), qtile(width),
                  full(LANES), full(rank), pl.BlockSpec((1, t // ka, rank, ka), lambda i, j: (i, 0, 0, 0)),
                  const(wk_t), const(wv)],
        out_specs=qtile(width), out_shape=jax.ShapeDtypeStruct((b, t, width), BF16),
        scratch_shapes=[pltpu.VMEM((t // kt, kt, qb), I32), pltpu.VMEM((n_heads, rank, qb), BF16),
                        pltpu.VMEM((n_heads, rank, qb), F32), pltpu.VMEM((n_heads, 1, qb), F32),
                        pltpu.VMEM((n_heads, 1, qb), F32)],
        compiler_params=_cparams(("parallel", "arbitrary")), name="dsa_attention",
    )(q_idx, w_t, q_b, kn, cn, cn_t, wk_t, wv)


def _even_out_kernel(x_ref, y_ref, bonus_ref, gate_ref, yb_ref, lng_ref, lnb_ref, seg_ref, wa_ref, wb_ref, o_ref):
    seg = seg_ref[...]
    y = y_ref[...]
    yc = y - _seg_dot(y, seg) * (1.0 / HEAD_DIM)
    yn = yc * lax.rsqrt(_seg_dot(yc * yc, seg) * (1.0 / HEAD_DIM) + GN_EPS)
    ya = (yn * lng_ref[...] + lnb_ref[...] + bonus_ref[...]) * gate_ref[...]
    o_ref[...] = x_ref[...] + _dot(ya.astype(BF16), wa_ref[...]) + _dot(yb_ref[...], wb_ref[...])


def even_out(x2d, y, bonus, gate, y_b, ln_g, ln_b, w_out, tm=256):
    n, d = x2d.shape
    aw = y.shape[1]
    tile = lambda c: pl.BlockSpec((tm, c), lambda i: (i, 0))
    const = lambda a: pl.BlockSpec(a.shape, lambda i: (0, 0))
    consts = [ln_g.reshape(1, aw), ln_b.reshape(1, aw), _seg_ones(aw),
              w_out[:aw].astype(BF16), w_out[aw:].astype(BF16)]
    return pl.pallas_call(
        _even_out_kernel, grid=(n // tm,),
        in_specs=[tile(d), tile(aw), tile(aw), tile(aw), tile(y_b.shape[1])] + [const(a) for a in consts],
        out_specs=tile(d), out_shape=jax.ShapeDtypeStruct((n, d), F32),
        compiler_params=_cparams(("parallel",)), name="even_out",
    )(x2d, y, bonus, gate, y_b, *consts)


SB_PAIRS = 4


def _stickbreak_kernel(q_ref, k_ref, vt_ref, o_ref, acc_ref, carry_ref, *, tq, n_pairs):
    qi = pl.program_id(2)
    first_sub = lax.broadcasted_iota(I32, (PAIR, 1), 0) < HEAD_DIM
    key_i = lax.broadcasted_iota(I32, (tq, tq), 0)
    qry_i = lax.broadcasted_iota(I32, (tq, tq), 1)
    before = key_i < qry_i
    later = (qry_i > key_i).astype(BF16)
    heads = [(p, x) for p in range(n_pairs) for x in range(2)]
    q_h = {}
    for p in range(n_pairs):
        q_t = q_ref[0, :, p * PAIR:(p + 1) * PAIR].astype(F32).T * HEAD_DIM ** -0.5
        q_h[p, 0] = jnp.where(first_sub, q_t, 0.0).astype(BF16)
        q_h[p, 1] = jnp.where(first_sub, 0.0, q_t).astype(BF16)
    acc_ref[...] = jnp.zeros_like(acc_ref)
    carry_ref[...] = jnp.zeros_like(carry_ref)

    def tile(j, diag):
        off = pl.multiple_of(j * tq, tq)
        k2 = [k_ref[0, pl.ds(off, tq), p * PAIR:(p + 1) * PAIR] for p in range(n_pairs)]
        z = {h: _dot(k2[h[0]], q_h[h]) * math.log2(math.e) for h in heads}
        lk = {h: -(jnp.maximum(z[h], 0.0) + jnp.log2(1.0 + jnp.exp2(-jnp.abs(z[h])))) for h in heads}
        if diag:
            lk = {h: jnp.where(before, lk[h], 0.0) for h in heads}
        hi = {h: lk[h].astype(BF16) for h in heads}
        lo = {h: (lk[h] - hi[h].astype(F32)).astype(BF16) for h in heads}
        tail = {h: _dot(later, hi[h]) + _dot(later, lo[h]) + carry_ref[h[0], h[1]] for h in heads}
        w = {h: jnp.exp2(z[h] + lk[h] + tail[h]) for h in heads}
        if diag:
            w = {h: jnp.where(before, w[h], 0.0) for h in heads}
        for p in range(n_pairs):
            v_t = vt_ref[0, p, j]
            acc_ref[p] += (_dot(jnp.where(first_sub, v_t, 0.0), w[p, 0].astype(BF16))
                           + _dot(jnp.where(first_sub, 0.0, v_t), w[p, 1].astype(BF16)))
        for h in heads:
            carry_ref[h[0], h[1]] += jnp.sum(lk[h], axis=0, keepdims=True)
        return (jnp.max(carry_ref[...]) > -110.0 * math.log2(math.e)).astype(I32)

    live = tile(qi, True)

    def cond(st):
        return (st[0] >= 0) & (st[1] > 0)

    def body(st):
        return st[0] - 1, tile(st[0], False)

    lax.while_loop(cond, body, (qi - 1, live))
    for p in range(n_pairs):
        o_ref[0, :, p * PAIR:(p + 1) * PAIR] = acc_ref[p].T.astype(o_ref.dtype)


def _qkv_proj_kernel(x_ref, g_ref, wqk_ref, wv_ref, qk_ref, vt_ref):
    h = _rms(x_ref[...], g_ref[...]).astype(BF16)
    qk_ref[...] = _dot(h, wqk_ref[...]).astype(qk_ref.dtype)
    v = _dot(h, wv_ref[...])
    for p in range(v.shape[1] // PAIR):
        vt_ref[0, p, 0] = v[:, p * PAIR:(p + 1) * PAIR].T.astype(vt_ref.dtype)


def qkv_proj(x, g, w_qkv, tq):
    b, t, d = x.shape
    width = w_qkv.shape[1] // 3
    n_pair = width // PAIR
    nq = t // tq
    wqk, wv = w_qkv[:, :2 * width].astype(BF16), w_qkv[:, 2 * width:].astype(BF16)
    return pl.pallas_call(
        _qkv_proj_kernel, grid=(b * nq,),
        in_specs=[pl.BlockSpec((tq, d), lambda i: (i, 0)), pl.BlockSpec((1, d), lambda i: (0, 0)),
                  pl.BlockSpec(wqk.shape, lambda i: (0, 0)), pl.BlockSpec(wv.shape, lambda i: (0, 0))],
        out_specs=[pl.BlockSpec((tq, 2 * width), lambda i: (i, 0)),
                   pl.BlockSpec((1, n_pair, 1, PAIR, tq), lambda i: (i // nq, 0, i % nq, 0, 0))],
        out_shape=[jax.ShapeDtypeStruct((b * t, 2 * width), BF16),
                   jax.ShapeDtypeStruct((b, n_pair, nq, PAIR, tq), BF16)],
        compiler_params=_cparams(("parallel",)), name="qkv_proj",
    )(x.reshape(b * t, d), g.reshape(1, d), wqk, wv)


def stickbreak_attention(qk, v_t, n_heads, tq):
    b, t, _ = qk.shape
    n_pair = n_heads // 2
    width = n_heads * HEAD_DIM
    gw = SB_PAIRS * PAIR
    n_grp = n_pair // SB_PAIRS
    qkv = qk
    q_spec = pl.BlockSpec((1, tq, gw), lambda i, p, j: (i, j, p))
    k_spec = pl.BlockSpec((1, t, gw), lambda i, p, j: (i, 0, n_grp + p))
    v_spec = pl.BlockSpec((1, SB_PAIRS, t // tq, PAIR, tq), lambda i, p, j: (i, p, 0, 0, 0))
    return pl.pallas_call(
        functools.partial(_stickbreak_kernel, tq=tq, n_pairs=SB_PAIRS), grid=(b, n_grp, t // tq),
        in_specs=[q_spec, k_spec, v_spec], out_specs=q_spec,
        out_shape=jax.ShapeDtypeStruct((b, t, width), BF16),
        scratch_shapes=[pltpu.VMEM((SB_PAIRS, PAIR, tq), F32), pltpu.VMEM((SB_PAIRS, 2, 1, tq), F32)],
        compiler_params=_cparams(("parallel", "parallel", "arbitrary")), name="stickbreak",
    )(qkv, qkv, v_t)


def _proj_residual_kernel(x_ref, y_ref, w_ref, o_ref):
    o_ref[...] = x_ref[...] + _dot(y_ref[...], w_ref[...])


def proj_residual(x2d, y, w, tm=256):
    n, d = x2d.shape
    c = y.shape[1]
    return pl.pallas_call(
        _proj_residual_kernel, grid=(n // tm,),
        in_specs=[pl.BlockSpec((tm, d), lambda i: (i, 0)), pl.BlockSpec((tm, c), lambda i: (i, 0)),
                  pl.BlockSpec(w.shape, lambda i: (0, 0))],
        out_specs=pl.BlockSpec((tm, d), lambda i: (i, 0)), out_shape=jax.ShapeDtypeStruct((n, d), F32),
        compiler_params=_cparams(("parallel",)), name="proj_residual",
    )(x2d, y, w)


def _router_kernel(x_ref, g_ref, wr_hi_ref, wr_lo_ref, h_ref, ids_ref, gates_ref):
    h = _rms(x_ref[...], g_ref[...])
    h_hi = h.astype(BF16)
    h_ref[...] = h_hi
    h_lo = (h - h_hi.astype(F32)).astype(BF16)
    logits = _dot(h_hi, wr_hi_ref[...]) + _dot(h_hi, wr_lo_ref[...]) + _dot(h_lo, wr_hi_ref[...])
    lane = lax.broadcasted_iota(I32, logits.shape, 1)
    lane_f = lane.astype(F32)

    def top1(vals):
        mx = jnp.max(vals, axis=1, keepdims=True)
        return mx, jnp.min(jnp.where(vals == mx, lane_f, 1e9), axis=1, keepdims=True).astype(I32)

    is_g = lane < N_GROUPS
    g_max, g_sel = top1(jnp.where(is_g, logits, -jnp.inf))
    g_gate = 1.0 / jnp.sum(jnp.where(is_g, jnp.exp(logits - g_max), 0.0), axis=1, keepdims=True)
    e_lane = lane - N_GROUPS
    in_grp = (e_lane >= g_sel * EXPERTS_PER_GROUP) & (e_lane < (g_sel + 1) * EXPERTS_PER_GROUP)
    el = jnp.where(in_grp, logits, -jnp.inf)
    m1, i1 = top1(el)
    m2, i2 = top1(jnp.where(lane == i1, -jnp.inf, el))
    e21 = jnp.exp(m2 - m1)
    p1 = 1.0 / (1.0 + e21)
    ids_ref[...] = jnp.where(lane == 0, i1 - N_GROUPS, jnp.where(lane == 1, i2 - N_GROUPS, 0))
    gates_ref[...] = jnp.where(lane == 0, g_gate * p1, jnp.where(lane == 1, g_gate * (e21 * p1), 0.0))


def moe_router(x2d, g, w_router, tm=256):
    n, d = x2d.shape
    tile = lambda c: pl.BlockSpec((tm, c), lambda i: (i, 0))
    w_spec = pl.BlockSpec(w_router.shape, lambda i: (0, 0))
    w_hi = w_router.astype(BF16)
    w_lo = (w_router - w_hi.astype(F32)).astype(BF16)
    return pl.pallas_call(
        _router_kernel, grid=(n // tm,),
        in_specs=[tile(d), pl.BlockSpec((1, d), lambda i: (0, 0)), w_spec, w_spec],
        out_specs=[tile(d), tile(LANES), tile(LANES)],
        out_shape=[jax.ShapeDtypeStruct((n, d), BF16), jax.ShapeDtypeStruct((n, LANES), I32),
                   jax.ShapeDtypeStruct((n, LANES), F32)],
        compiler_params=_cparams(("parallel",)), name="moe_router",
    )(x2d, g.reshape(1, d), w_hi, w_lo)


MOE_TT = 2048
MOE_CH = 128
_PLAN_BLK = 256


def _moe_plan_kernel(ids_ref, idst_ref, rank_ref, dest_ref, meta_ref, *, tt):
    nb = tt // _PLAN_BLK
    row = lax.broadcasted_iota(I32, (_PLAN_BLK, _PLAN_BLK), 0)
    col = lax.broadcasted_iota(I32, (_PLAN_BLK, _PLAN_BLK), 1)
    earlier_t = (row < col).astype(BF16)
    earlier = (col < row).astype(BF16)
    e_sub = lax.broadcasted_iota(I32, (N_EXPERTS, _PLAN_BLK), 0)
    carry_t = jnp.zeros((N_EXPERTS, 1), F32)
    for b in range(nb):
        sl = slice(b * _PLAN_BLK, (b + 1) * _PLAN_BLK)
        member = (e_sub == idst_ref[0:1, sl]) | (e_sub == idst_ref[1:2, sl])
        m_t = jnp.where(member, 1.0, 0.0)
        before = _dot(m_t.astype(BF16), earlier_t) + carry_t
        rank_ref[0, :, sl] = jnp.where(member, before, -1.0)
        carry_t = carry_t + jnp.sum(m_t, axis=1, keepdims=True)
    lane = lax.broadcasted_iota(I32, (_PLAN_BLK, LANES), 1)
    carry = jnp.zeros((1, LANES), F32)
    prefix = []
    for b in range(nb):
        sl = slice(b * _PLAN_BLK, (b + 1) * _PLAN_BLK)
        m = jnp.where((lane == ids_ref[sl, 0:1]) | (lane == ids_ref[sl, 1:2]), 1.0, 0.0)
        prefix.append(_dot(earlier, m.astype(BF16)) + carry)
        carry = carry + jnp.sum(m, axis=0, keepdims=True)
    counts = carry
    padded = jnp.ceil(counts * (1.0 / MOE_CH)) * MOE_CH
    l_row = lax.broadcasted_iota(I32, (LANES, LANES), 0)
    l_col = lax.broadcasted_iota(I32, (LANES, LANES), 1)
    offs = _dot_hi(padded, (l_row < l_col).astype(F32))
    for b in range(nb):
        sl = slice(b * _PLAN_BLK, (b + 1) * _PLAN_BLK)
        where_row = prefix[b] + offs
        d0 = jnp.sum(jnp.where(lane == ids_ref[sl, 0:1], where_row, 0.0), axis=1, keepdims=True)
        d1 = jnp.sum(jnp.where(lane == ids_ref[sl, 1:2], where_row, 0.0), axis=1, keepdims=True)
        dest_ref[sl, :] = jnp.where(lane == 0, d0, jnp.where(lane == 1, d1, 0.0)).astype(I32)
    r_used = jnp.sum(padded, axis=1, keepdims=True)
    sub8 = lax.broadcasted_iota(I32, (8, LANES), 0)
    meta_ref[0] = jnp.where(sub8 == 0, counts, jnp.where(sub8 == 1, offs, jnp.where(sub8 == 2, r_used, 0.0))).astype(I32)


def moe_plan(ids, ids_t, tt):
    n = ids.shape[0]
    n_tiles = n // tt
    return pl.pallas_call(
        functools.partial(_moe_plan_kernel, tt=tt), grid=(n_tiles,),
        in_specs=[pl.BlockSpec((tt, LANES), lambda i: (i, 0)), pl.BlockSpec((8, tt), lambda i: (0, i))],
        out_specs=[pl.BlockSpec((1, N_EXPERTS, tt), lambda i: (i, 0, 0)), pl.BlockSpec((tt, LANES), lambda i: (i, 0)),
                   pl.BlockSpec((1, 8, LANES), lambda i: (i, 0, 0))],
        out_shape=[jax.ShapeDtypeStruct((n_tiles, N_EXPERTS, tt), F32), jax.ShapeDtypeStruct((n, LANES), I32),
                   jax.ShapeDtypeStruct((n_tiles, 8, LANES), I32)],
        compiler_params=_cparams(("parallel",)), name="moe_plan",
    )(ids, ids_t)


def _pack_bf16_halves(y):
    half = y.shape[1] // 2
    lo = pltpu.bitcast(y[:, :half].astype(BF16).astype(F32), I32)
    hi = pltpu.bitcast(y[:, half:].astype(BF16).astype(F32), I32)
    return lax.shift_right_logical(lo, 16) | hi


def _unpack_bf16_halves(w):
    return pltpu.bitcast(w << 16, F32), pltpu.bitcast(w & jnp.int32(-65536), F32)


def _moe_ffn_kernel(cnt_ref, off_ref, h_ref, rank_ref, idst_ref, gt_ref, w1_ref, w3_ref, w2_ref, y_ref, *, tt):
    t, e = pl.program_id(0), pl.program_id(1)

    @pl.when(e == 0)
    def _():
        y_ref[...] = jnp.zeros_like(y_ref)

    cnt = cnt_ref[t * N_EXPERTS + e]
    off = off_ref[t * N_EXPERTS + e]
    rank_row = rank_ref[0, pl.ds(e, 1), :]
    gate_row = jnp.where(idst_ref[0:1, :] == e, gt_ref[0:1, :], gt_ref[1:2, :])
    sub = lax.broadcasted_iota(I32, (MOE_CH, 1), 0)

    def chunk(c, carry):
        r0 = c * MOE_CH
        pick = rank_row == (r0 + sub).astype(F32)
        x = _dot(jnp.where(pick, 1.0, 0.0).astype(BF16), h_ref[...]).astype(BF16)
        a = _dot(x, w1_ref[0, 0])
        hdn = (a * jax.nn.sigmoid(a)) * _dot(x, w3_ref[0, 0])
        y = _dot(hdn.astype(BF16), w2_ref[0, 0])
        gate = jnp.sum(jnp.where(pick, gate_row, 0.0), axis=1, keepdims=True)
        y_ref[0, pl.ds(pl.multiple_of(off + r0, MOE_CH), MOE_CH), :] = _pack_bf16_halves(y * gate)
        return carry

    lax.fori_loop(0, (cnt + MOE_CH - 1) // MOE_CH, chunk, 0)


def moe_ffn(h, rank_t, ids_t, gates_t, counts, offs, w1, w3, w2, layer, tt):
    n, d = h.shape
    ff = w1.shape[-1]
    n_tiles = n // tt
    r_max = TOP_K_EXPERTS * tt + N_EXPERTS * MOE_CH
    grid_spec = pltpu.PrefetchScalarGridSpec(
        num_scalar_prefetch=2, grid=(n_tiles, N_EXPERTS),
        in_specs=[pl.BlockSpec((tt, d), lambda i, e, c, o: (i, 0)),
                  pl.BlockSpec((1, N_EXPERTS, tt), lambda i, e, c, o: (i, 0, 0)),
                  pl.BlockSpec((8, tt), lambda i, e, c, o: (0, i)),
                  pl.BlockSpec((8, tt), lambda i, e, c, o: (0, i)),
                  pl.BlockSpec((1, 1, d, ff), lambda i, e, c, o: (layer, e, 0, 0)),
                  pl.BlockSpec((1, 1, d, ff), lambda i, e, c, o: (layer, e, 0, 0)),
                  pl.BlockSpec((1, 1, ff, d), lambda i, e, c, o: (layer, e, 0, 0))],
        out_specs=pl.BlockSpec((1, r_max, d // 2), lambda i, e, c, o: (i, 0, 0)))
    return pl.pallas_call(
        functools.partial(_moe_ffn_kernel, tt=tt), grid_spec=grid_spec,
        out_shape=jax.ShapeDtypeStruct((n_tiles, r_max, d // 2), I32),
        compiler_params=_cparams(("arbitrary", "arbitrary")), name="moe_ffn",
    )(counts, offs, h, rank_t, ids_t, gates_t, w1, w3, w2)


def _moe_combine_kernel(dest_ref, x_ref, y_ref, o_ref, ga_ref, gb_ref, *, tm, per):
    base = (pl.program_id(0) * per + pl.program_id(1)) * tm

    def fetch(i, carry):
        tok = (base + i) * TOP_K_EXPERTS
        ga_ref[pl.ds(i, 1), :] = y_ref[0, pl.ds(dest_ref[tok], 1), :]
        gb_ref[pl.ds(i, 1), :] = y_ref[0, pl.ds(dest_ref[tok + 1], 1), :]
        return carry

    lax.fori_loop(0, tm, fetch, 0, unroll=8)
    half = x_ref.shape[1] // 2
    lo_a, hi_a = _unpack_bf16_halves(ga_ref[...])
    lo_b, hi_b = _unpack_bf16_halves(gb_ref[...])
    o_ref[:, :half] = x_ref[:, :half] + (lo_a + lo_b)
    o_ref[:, half:] = x_ref[:, half:] + (hi_a + hi_b)


def moe_combine(x2d, dest_flat, y_all, tt, tm=256):
    n, d = x2d.shape
    r_max = y_all.shape[1]
    per = tt // tm
    grid_spec = pltpu.PrefetchScalarGridSpec(
        num_scalar_prefetch=1, grid=(n // tt, per),
        in_specs=[pl.BlockSpec((tm, d), lambda i, j, u: (i * per + j, 0)),
                  pl.BlockSpec((1, r_max, d // 2), lambda i, j, u: (i, 0, 0))],
        out_specs=pl.BlockSpec((tm, d), lambda i, j, u: (i * per + j, 0)),
        scratch_shapes=[pltpu.VMEM((tm, d // 2), I32), pltpu.VMEM((tm, d // 2), I32)])
    return pl.pallas_call(
        functools.partial(_moe_combine_kernel, tm=tm, per=per), grid_spec=grid_spec,
        out_shape=jax.ShapeDtypeStruct((n, d), F32),
        compiler_params=_cparams(("arbitrary", "arbitrary")), name="moe_combine",
    )(dest_flat, x2d, y_all)


def _final_norm_kernel(x_ref, g_ref, o_ref):
    o_ref[...] = _rms(x_ref[...], g_ref[...])


def final_norm(x2d, g, tm=512):
    n, d = x2d.shape
    return pl.pallas_call(
        _final_norm_kernel, grid=(n // tm,),
        in_specs=[pl.BlockSpec((tm, d), lambda i: (i, 0)), pl.BlockSpec((1, d), lambda i: (0, 0))],
        out_specs=pl.BlockSpec((tm, d), lambda i: (i, 0)), out_shape=jax.ShapeDtypeStruct((n, d), F32),
        compiler_params=_cparams(("parallel",)), name="final_norm",
    )(x2d, g.reshape(1, d))


def moe_layer(x2d, norm_g, router_group, router_expert, w1, w3, w2, layer):
    n, d = x2d.shape
    tt = min(MOE_TT, n)
    w_router = jnp.concatenate([router_group, jnp.moveaxis(router_expert, 0, 1).reshape(d, N_EXPERTS)], axis=1)
    h, ids, gates = moe_router(x2d, norm_g, _pad_cols(w_router, LANES))
    ids_t = jnp.pad(ids[:, :TOP_K_EXPERTS].T, ((0, 8 - TOP_K_EXPERTS), (0, 0)))
    gates_t = jnp.pad(gates[:, :TOP_K_EXPERTS].T, ((0, 8 - TOP_K_EXPERTS), (0, 0)))
    rank_t, dest, meta = moe_plan(ids, ids_t, tt)
    counts = meta[:, 0, :N_EXPERTS].reshape(-1)
    offs = meta[:, 1, :N_EXPERTS].reshape(-1)
    y_all = moe_ffn(h, rank_t, ids_t, gates_t, counts, offs, w1, w3, w2, layer, tt)
    return moe_combine(x2d, dest[:, :TOP_K_EXPERTS].reshape(-1), y_all, tt)


def even_layer(x, v_first, p):
    b, t, d = x.shape
    n = b * t
    w_in = p['w_in']
    aw = p['k_k'].shape[0]
    bw = p['w_kv_up'].shape[1] // 2
    rank = p['w_kv_up'].shape[0]
    n_idx = w_in.shape[1] - (3 * aw + bw + rank + bw + HEAD_DIM)
    o = 0
    segs = []
    for width in (3 * aw, bw, rank, bw, HEAD_DIM, n_idx):
        segs.append(w_in[:, o:o + width])
        o += width
    segs[4] = jnp.concatenate([segs[4], segs[4]], axis=1)
    segs[5] = _pad_cols(segs[5], LANES)
    rkv, q_b, c_kv, q_idx, k_dup, w_idx = norm_proj(
        x.reshape(n, d), p['norm'], [s.astype(BF16) for s in segs], [F32] * 6)

    r, lw, k_mod, v, kk, iclr, gate, bonus = rwkv_prep(x, rkv.reshape(b, t, 3 * aw), v_first, p)
    y = rwkv_scan(r, lw, k_mod, v, kk, iclr)

    cn, kn = kv_prep(c_kv, k_dup, p['kv_norm'], jnp.concatenate([p['idx_k_norm']] * 2))
    wk_t = p['w_kv_up'][:, :bw].astype(BF16)
    wv = p['w_kv_up'][:, bw:].astype(BF16)
    y_b = dsa_attention(q_idx.reshape(b, t, bw), w_idx.reshape(b, t, LANES), q_b.reshape(b, t, bw),
                        kn.reshape(b, t, LANES), cn.reshape(b, t, rank), wk_t, wv)

    x_new = even_out(x.reshape(n, d), y.reshape(n, aw), bonus.reshape(n, aw), gate.reshape(n, aw),
                     y_b.reshape(n, bw), p['ln_g'], p['ln_b'], p['w_out'])
    return x_new.reshape(b, t, d), v


def odd_layer(x, norm_g, w_qkv, w_out):
    b, t, d = x.shape
    n = b * t
    cw = w_qkv.shape[1] // 3
    tq = min(256, t)
    qk, v_t = qkv_proj(x, norm_g, w_qkv, tq)
    y_c = stickbreak_attention(qk.reshape(b, t, 2 * cw), v_t, cw // HEAD_DIM, tq)
    return proj_residual(x.reshape(n, d), y_c.reshape(n, cw), w_out.astype(BF16)).reshape(b, t, d)


@jax.jit
def _forward(x, ev_norm, ev_w_in, ev_w_out, a_mu_rkv, a_mu_lora, a_w0, a_w1, a_w2, a_a0, a_a1, a_a2, a_g1,
             a_g2, a_k_k, a_k_a, a_r_k, a_ln_g, a_ln_b, a_mu_vres, a_v0, a_v1, a_v2, b_kv_norm, b_w_kv_up,
             b_idx_k_norm, od_norm, od_w_qkv, od_w_out, moe_norm, moe_router_group, moe_router_expert,
             moe_w1, moe_w3, moe_w2, final_norm_g):
    b, t, d = x.shape
    depth = moe_norm.shape[0]
    w1_bf16, w3_bf16, w2_bf16 = moe_w1.astype(BF16), moe_w3.astype(BF16), moe_w2.astype(BF16)
    v_first = None
    for i in range(depth):
        if i % 2 == 0:
            e = i // 2
            p = dict(norm=ev_norm[e], w_in=ev_w_in[e], w_out=ev_w_out[e], mu_rkv=a_mu_rkv[e], mu_lora=a_mu_lora[e],
                     w0=a_w0[e], w1=a_w1[e], w2=a_w2[e], a0=a_a0[e], a1=a_a1[e], a2=a_a2[e], g1=a_g1[e], g2=a_g2[e],
                     k_k=a_k_k[e], k_a=a_k_a[e], r_k=a_r_k[e].reshape(-1), ln_g=a_ln_g[e], ln_b=a_ln_b[e],
                     kv_norm=b_kv_norm[e], w_kv_up=b_w_kv_up[e], idx_k_norm=b_idx_k_norm[e])
            if e > 0:
                p.update(mu_v=a_mu_vres[e - 1], v0=a_v0[e - 1], v1=a_v1[e - 1], v2=a_v2[e - 1])
            x, v_used = even_layer(x, v_first if e > 0 else None, p)
            if e == 0:
                v_first = v_used
        else:
            o = i // 2
            x = odd_layer(x, od_norm[o], od_w_qkv[o], od_w_out[o])
        x = moe_layer(x.reshape(b * t, d), moe_norm[i], moe_router_group[i], moe_router_expert[i],
                      w1_bf16, w3_bf16, w2_bf16, i).reshape(b, t, d)
    return final_norm(x.reshape(b * t, d), final_norm_g).reshape(b, t, d)


def kernel(x, ev_norm, ev_w_in, ev_w_out, a_mu_rkv, a_mu_lora, a_w0, a_w1, a_w2, a_a0, a_a1, a_a2, a_g1, a_g2, a_k_k, a_k_a, a_r_k, a_ln_g, a_ln_b, a_mu_vres, a_v0, a_v1, a_v2, b_kv_norm, b_w_kv_up, b_idx_k_norm, od_norm, od_w_qkv, od_w_out, moe_norm, moe_router_group, moe_router_expert, moe_w1, moe_w3, moe_w2, final_norm):
    return _forward(x, ev_norm, ev_w_in, ev_w_out, a_mu_rkv, a_mu_lora, a_w0, a_w1, a_w2, a_a0, a_a1, a_a2, a_g1,
                    a_g2, a_k_k, a_k_a, a_r_k, a_ln_g, a_ln_b, a_mu_vres, a_v0, a_v1, a_v2, b_kv_norm, b_w_kv_up,
                    b_idx_k_norm, od_norm, od_w_qkv, od_w_out, moe_norm, moe_router_group, moe_router_expert,
                    moe_w1, moe_w3, moe_w2, final_norm)
```
